```python
import math
import jax, jax.numpy as jnp
from jax import lax
import numpy as np

D_MODEL = 2048
BATCH = 1
SEQ = 8192
DEPTH = 1

HEAD_DIM = 128
ROPE_THETA = 10000.0
Q_BLOCK = 128
LN_EPS = 1e-5

DIFF_HEADS = 4
DIFF_V_DIM = 2 * HEAD_DIM
DIFF_Q = DIFF_HEADS * 2 * HEAD_DIM
DIFF_K = DIFF_HEADS * 2 * HEAD_DIM
DIFF_V = DIFF_HEADS * DIFF_V_DIM

DIL_CONFIGS = ((128, 1), (512, 4), (2048, 16))
DIL_HEADS_PER_GROUP = 4
DIL_GROUPS = len(DIL_CONFIGS)
DIL_HEADS = DIL_GROUPS * DIL_HEADS_PER_GROUP
DIL_QKV = DIL_HEADS * HEAD_DIM
DIL_OUT = DIL_HEADS_PER_GROUP * HEAD_DIM

IN_SIZES = (DIFF_Q, DIFF_K, DIFF_V, DIL_QKV, DIL_QKV, DIL_QKV, D_MODEL, D_MODEL)
IN_COLS = sum(IN_SIZES)

N_EXPERTS = 256
TOP_K = 8
N_GROUPS = 8
TOPK_GROUPS = 4
EXPERT_FF = 512
SHARED_FF = 512
ROUTED_SCALE = 2.5
MOE_BLOCK = 128

DN_ALPHA = (2.0 * DEPTH) ** 0.25
DN_BETA = (8.0 * DEPTH) ** -0.25

kernel_name = "hybrid_diffattn_dilated_moe_deepnorm"


def layer_norm(x, g, b):
    xf = x.astype(jnp.float32)
    mu = jnp.mean(xf, axis=-1, keepdims=True)
    var = jnp.mean(jnp.square(xf - mu), axis=-1, keepdims=True)
    y = (xf - mu) * lax.rsqrt(var + LN_EPS) * g.astype(jnp.float32) + b.astype(jnp.float32)
    return y.astype(x.dtype)


def rms_norm(x, g):
    xf = x.astype(jnp.float32)
    y = xf * lax.rsqrt(jnp.mean(jnp.square(xf), axis=-1, keepdims=True) + LN_EPS) * g.astype(jnp.float32)
    return y.astype(x.dtype)


def rope(x, positions):
    half = x.shape[-1] // 2
    inv = ROPE_THETA ** (-jnp.arange(half, dtype=jnp.float32) / half)
    ang = positions.astype(jnp.float32)[:, None] * inv[None, :]
    cos = jnp.cos(ang)[None, :, None, :]
    sin = jnp.sin(ang)[None, :, None, :]
    x1 = x[..., :half].astype(jnp.float32)
    x2 = x[..., half:].astype(jnp.float32)
    out = jnp.concatenate([x1 * cos - x2 * sin, x2 * cos + x1 * sin], axis=-1)
    return out.astype(x.dtype)


def diff_attention(q, k, v, lam, norm_g, lambda_init):
    B, S = q.shape[0], q.shape[1]
    nb = S // Q_BLOCK
    scale = HEAD_DIM ** -0.5
    kpos = jnp.arange(S)
    neg = jnp.finfo(jnp.float32).min
    qb = q.reshape(B, nb, Q_BLOCK, 2 * DIFF_HEADS, HEAD_DIM).transpose(1, 0, 2, 3, 4)

    def block(args):
        i, qi = args
        s = jnp.einsum('bqhd,bkhd->bhqk', qi, k).astype(jnp.float32) * scale
        qpos = i * Q_BLOCK + jnp.arange(Q_BLOCK)
        causal = kpos[None, :] <= qpos[:, None]
        p = jax.nn.softmax(jnp.where(causal[None, None], s, neg), axis=-1)
        p = p.reshape(B, DIFF_HEADS, 2, Q_BLOCK, S)
        a = p[:, :, 0] - lam * p[:, :, 1]
        return jnp.einsum('bhqk,bkhd->bqhd', a.astype(v.dtype), v)

    o = lax.map(block, (jnp.arange(nb), qb))
    o = o.transpose(1, 0, 2, 3, 4).reshape(B, S, DIFF_HEADS, DIFF_V_DIM)
    o = rms_norm(o, norm_g) * (1.0 - lambda_init)
    return o.reshape(B, S, DIFF_V)


def dilated_group(q, k, v, window, dilation):
    B, S, Hg, Dh = q.shape
    span = window // dilation
    assert span <= Q_BLOCK
    L = S // dilation
    nb = -(-L // Q_BLOCK)
    Lp = nb * Q_BLOCK
    scale = Dh ** -0.5
    neg = jnp.finfo(jnp.float32).min

    def strided(t):
        t = t.reshape(B, L, dilation, Hg, Dh).transpose(0, 2, 1, 3, 4)
        t = jnp.pad(t, ((0, 0), (0, 0), (0, Lp - L), (0, 0), (0, 0)))
        return t.reshape(B, dilation, nb, Q_BLOCK, Hg, Dh)

    def with_prev(t):
        prev = jnp.concatenate([jnp.zeros_like(t[:, :, :1]), t[:, :, :-1]], axis=2)
        return jnp.concatenate([prev, t], axis=3)

    qs = strided(q)
    ks = with_prev(strided(k))
    vs = with_prev(strided(v))
    s = jnp.einsum('brnqhd,brnkhd->brnhqk', qs, ks).astype(jnp.float32) * scale
    qi = jnp.arange(Q_BLOCK)[:, None]
    kj = jnp.arange(2 * Q_BLOCK)[None, :]
    dist = qi + Q_BLOCK - kj
    key_idx = jnp.arange(nb)[:, None, None] * Q_BLOCK - Q_BLOCK + kj[None]
    valid = ((dist >= 0) & (dist <= span))[None] & (key_idx >= 0)
    s = jnp.where(valid[None, None, :, None], s, neg)
    lse = jax.nn.logsumexp(s, axis=-1)
    p = jnp.exp(s - lse[..., None])
    o = jnp.einsum('brnhqk,brnkhd->brnqhd', p.astype(v.dtype), vs)
    o = o.reshape(B, dilation, Lp, Hg, Dh)[:, :, :L].transpose(0, 2, 1, 3, 4).reshape(B, S, Hg, Dh)
    lse = lse.transpose(0, 1, 2, 4, 3).reshape(B, dilation, Lp, Hg)[:, :, :L]
    lse = lse.transpose(0, 2, 1, 3).reshape(B, S, Hg)
    return o, lse


def dilated_attention(q, k, v):
    outs, lses = [], []
    for g, (window, dilation) in enumerate(DIL_CONFIGS):
        sl = slice(g * DIL_HEADS_PER_GROUP, (g + 1) * DIL_HEADS_PER_GROUP)
        o, lse = dilated_group(q[:, :, sl], k[:, :, sl], v[:, :, sl], window, dilation)
        outs.append(o.astype(jnp.float32))
        lses.append(lse)
    o = jnp.stack(outs, 0)
    w = jax.nn.softmax(jnp.stack(lses, 0), axis=0)
    o = jnp.sum(w[..., None] * o, axis=0).astype(q.dtype)
    B, S = q.shape[0], q.shape[1]
    return o.reshape(B, S, DIL_OUT)


def moe(h, router_w, router_bias, w_gate_e, w_up_e, w_down_e, w_gate_s, w_up_s, w_down_s):
    B, S, D = h.shape
    T = B * S
    xf = h.reshape(T, D)
    scores = jax.nn.sigmoid((xf @ router_w).astype(jnp.float32))
    biased = scores + router_bias.astype(jnp.float32)
    grp = biased.reshape(T, N_GROUPS, N_EXPERTS // N_GROUPS)
    grp_score = lax.top_k(grp, 2)[0].sum(-1)
    top_g = lax.top_k(grp_score, TOPK_GROUPS)[1]
    gmask = jnp.any(top_g[:, :, None] == jnp.arange(N_GROUPS)[None, None, :], axis=1)
    emask = jnp.repeat(gmask, N_EXPERTS // N_GROUPS, axis=1)
    top_e = lax.top_k(jnp.where(emask, biased, -jnp.inf), TOP_K)[1]
    w = jnp.take_along_axis(scores, top_e, axis=1)
    w = w / jnp.sum(w, axis=-1, keepdims=True) * ROUTED_SCALE

    A = T * TOP_K
    e_flat = top_e.reshape(A)
    t_flat = jnp.arange(A, dtype=jnp.int32) // TOP_K
    w_flat = w.reshape(A)
    order = jnp.argsort(e_flat)
    e_sorted = e_flat[order]
    counts = jnp.bincount(e_flat, length=N_EXPERTS)
    starts = jnp.cumsum(counts) - counts
    padded = (counts + MOE_BLOCK - 1) // MOE_BLOCK * MOE_BLOCK
    pstarts = jnp.cumsum(padded) - padded
    dest = pstarts[e_sorted] + (jnp.arange(A) - starts[e_sorted])
    n_blocks = -(-A // MOE_BLOCK) + N_EXPERTS
    P = n_blocks * MOE_BLOCK
    row_tok = jnp.full((P,), T, jnp.int32).at[dest].set(t_flat[order])
    row_w = jnp.zeros((P,), jnp.float32).at[dest].set(w_flat[order])
    block_end = jnp.cumsum(padded) // MOE_BLOCK
    block_expert = jnp.minimum(jnp.searchsorted(block_end, jnp.arange(n_blocks), side='right'), N_EXPERTS - 1)

    x_ext = jnp.concatenate([xf, jnp.zeros((1, D), xf.dtype)], axis=0)

    def step(acc, blk):
        e, tok, wt = blk
        xb = x_ext[tok]
        y = (jax.nn.silu(xb @ w_gate_e[e]) * (xb @ w_up_e[e])) @ w_down_e[e]
        return acc.at[tok].add(y * wt[:, None].astype(y.dtype)), None

    acc, _ = lax.scan(step, jnp.zeros((T + 1, D), xf.dtype),
                      (block_expert, row_tok.reshape(n_blocks, MOE_BLOCK), row_w.reshape(n_blocks, MOE_BLOCK)))
    shared = (jax.nn.silu(xf @ w_gate_s) * (xf @ w_up_s)) @ w_down_s
    return (acc[:T] + shared).reshape(B, S, D)


def setup_inputs(seed: int = 0) -> dict:
    key = jax.random.key(seed)
    ks = jax.random.split(key, 24)
    f32 = jnp.float32
    nrm = lambda k, shape, s: jax.random.normal(k, shape, f32) * s
    D = D_MODEL
    col_scale = np.ones((IN_COLS,), np.float32)
    off = np.cumsum((0,) + IN_SIZES)
    col_scale[off[2]:off[3]] = DN_BETA
    col_scale[off[5]:off[6]] = DN_BETA
    return {
        "x": nrm(ks[0], (BATCH, SEQ, D), 1.0),
        "w_in": nrm(ks[1], (DEPTH, D, IN_COLS), D ** -0.5) * jnp.asarray(col_scale),
        "b_gate": nrm(ks[2], (DEPTH, 2 * D), 0.02),
        "lambda_q1": nrm(ks[3], (DEPTH, HEAD_DIM), 0.1),
        "lambda_k1": nrm(ks[4], (DEPTH, HEAD_DIM), 0.1),
        "lambda_q2": nrm(ks[5], (DEPTH, HEAD_DIM), 0.1),
        "lambda_k2": nrm(ks[6], (DEPTH, HEAD_DIM), 0.1),
        "diff_norm_g": 1.0 + nrm(ks[7], (DEPTH, DIFF_V_DIM), 0.02),
        "w_branch_diff": nrm(ks[8], (DEPTH, DIFF_V, D), DIFF_V ** -0.5),
        "w_branch_dil": nrm(ks[9], (DEPTH, DIL_OUT, D), DIL_OUT ** -0.5),
        "w_out": nrm(ks[10], (DEPTH, D, D), D ** -0.5 * DN_BETA),
        "ln1_g": 1.0 + nrm(ks[11], (DEPTH, D), 0.02),
        "ln1_b": nrm(ks[12], (DEPTH, D), 0.02),
        "router_w": nrm(ks[13], (DEPTH, D, N_EXPERTS), D ** -0.5),
        "router_bias": nrm(ks[14], (DEPTH, N_EXPERTS), 0.01),
        "w_gate_e": nrm(ks[15], (DEPTH, N_EXPERTS, D, EXPERT_FF), D ** -0.5),
        "w_up_e": nrm(ks[16], (DEPTH, N_EXPERTS, D, EXPERT_FF), D ** -0.5 * DN_BETA),
        "w_down_e": nrm(ks[17], (DEPTH, N_EXPERTS, EXPERT_FF, D), EXPERT_FF ** -0.5 * DN_BETA),
        "w_gate_s": nrm(ks[18], (DEPTH, D, SHARED_FF), D ** -0.5),
        "w_up_s": nrm(ks[19], (DEPTH, D, SHARED_FF), D ** -0.5 * DN_BETA),
        "w_down_s": nrm(ks[20], (DEPTH, SHARED_FF, D), SHARED_FF ** -0.5 * DN_BETA),
        "ln2_g": 1.0 + nrm(ks[21], (DEPTH, D), 0.02),
        "ln2_b": nrm(ks[22], (DEPTH, D), 0.02),
    }


def reference(x, w_in, b_gate, lambda_q1, lambda_k1, lambda_q2, lambda_k2, diff_norm_g,
              w_branch_diff, w_branch_dil, w_out, ln1_g, ln1_b, router_w, router_bias,
              w_gate_e, w_up_e, w_down_e, w_gate_s, w_up_s, w_down_s, ln2_g, ln2_b):
    B, S, D = x.shape
    pos = jnp.arange(S)
    split_at = [int(v) for v in np.cumsum(IN_SIZES)[:-1]]
    for l in range(DEPTH):
        lambda_init = 0.8 - 0.6 * math.exp(-0.3 * l)
        proj = x @ w_in[l]
        qd, kd, vd, qg, kg, vg, ga, gb = jnp.split(proj, split_at, axis=-1)
        qd = rope(qd.reshape(B, S, 2 * DIFF_HEADS, HEAD_DIM), pos)
        kd = rope(kd.reshape(B, S, 2 * DIFF_HEADS, HEAD_DIM), pos)
        vd = vd.reshape(B, S, DIFF_HEADS, DIFF_V_DIM)
        lam = (jnp.exp(jnp.sum(lambda_q1[l].astype(jnp.float32) * lambda_k1[l].astype(jnp.float32)))
               - jnp.exp(jnp.sum(lambda_q2[l].astype(jnp.float32) * lambda_k2[l].astype(jnp.float32)))
               + lambda_init)
        y_diff = diff_attention(qd, kd, vd, lam, diff_norm_g[l], lambda_init)
        qg = rope(qg.reshape(B, S, DIL_HEADS, HEAD_DIM), pos)
        kg = rope(kg.reshape(B, S, DIL_HEADS, HEAD_DIM), pos)
        vg = vg.reshape(B, S, DIL_HEADS, HEAD_DIM)
        y_dil = dilated_attention(qg, kg, vg)
        gate_a = jax.nn.sigmoid(ga + b_gate[l, :D])
        gate_b = jax.nn.sigmoid(gb + b_gate[l, D:])
        merged = gate_a * (y_diff @ w_branch_diff[l]) + gate_b * (y_dil @ w_branch_dil[l])
        x = layer_norm(DN_ALPHA * x + merged @ w_out[l], ln1_g[l], ln1_b[l])
        m = moe(x, router_w[l], router_bias[l], w_gate_e[l], w_up_e[l], w_down_e[l],
                w_gate_s[l], w_up_s[l], w_down_s[l])
        x = layer_norm(DN_ALPHA * x + m, ln2_g[l], ln2_b[l])
    return x
```

```python
import functools
import math

import jax
import jax.numpy as jnp
import numpy as np
from jax import lax
from jax.experimental import pallas as pl
from jax.experimental.pallas import tpu as pltpu

F32 = jnp.float32
BF16 = jnp.bfloat16
I32 = jnp.int32

HEAD_DIM = 128
ROPE_THETA = 10000.0
LN_EPS = 1e-5
DIFF_HEADS = 4
DIL_CONFIGS = ((128, 1), (512, 4), (2048, 16))
DIL_HEADS_PER_GROUP = 4
DIL_OUT = DIL_HEADS_PER_GROUP * HEAD_DIM
N_EXPERTS = 256
TOP_K = 8
N_GROUPS = 8
TOPK_GROUPS = 4
GROUP_SIZE = N_EXPERTS // N_GROUPS
ROUTED_SCALE = 2.5
ATTN_SCALE = HEAD_DIM ** -0.5

PROJ_TILE = 512
DIL_SRC_TILE0 = 6
GATE_SRC_TILE0 = 15
N_GATE_TILES = 8
N_MAIN_TILES = N_GATE_TILES + 6
DIFF_COL0 = N_GATE_TILES * PROJ_TILE // (2 * HEAD_DIM)

INPROJ_TM = 1024
ATTN_T = 512
DIL_T = 128
MERGE_TM = 256
ROUTE_TM = 512
ROW_TM = 128
EXP_CHUNK = 128
SUBLANES = 8
NEG = -1e30
VMEM_LIMIT = 56 * 1024 * 1024


def _cparams(sem, vmem=None):
    return pltpu.CompilerParams(dimension_semantics=sem, vmem_limit_bytes=vmem)


def _rope(a, cos, sin):
    outs = []
    for h in range(a.shape[1] // HEAD_DIM):
        ah = a[:, h * HEAD_DIM:(h + 1) * HEAD_DIM]
        outs.append(ah * cos + pltpu.roll(ah, HEAD_DIM // 2, 1) * sin)
    return jnp.concatenate(outs, axis=1)


def _inproj_kernel(x_ref, w_ref, cos_ref, sin_ref, b_ref, o_ref):
    j = pl.program_id(1)
    acc = jnp.dot(x_ref[...], w_ref[...], preferred_element_type=F32)
    is_gate = j < N_GATE_TILES
    is_q = (j >= N_GATE_TILES) & (j < N_GATE_TILES + 2)
    is_rope = (j >= N_GATE_TILES) & (j < N_GATE_TILES + 4)

    @pl.when(is_gate)
    def _():
        o_ref[...] = jax.nn.sigmoid(acc + b_ref[...]).astype(BF16)

    @pl.when(is_rope)
    def _():
        scale = jnp.where(is_q, ATTN_SCALE, 1.0).astype(F32)
        o_ref[...] = _rope(acc, cos_ref[...] * scale, sin_ref[...] * scale).astype(BF16)

    @pl.when(j >= N_GATE_TILES + 4)
    def _():
        o_ref[...] = acc.astype(BF16)


def _inproj(x_bf, w_bf, cos, sin, b_gate):
    s, d = x_bf.shape
    tm = min(INPROJ_TM, s)
    src = lambda j: jnp.where(j < N_GATE_TILES, j + GATE_SRC_TILE0, j - N_GATE_TILES)
    return pl.pallas_call(
        _inproj_kernel,
        out_shape=jax.ShapeDtypeStruct((s, N_MAIN_TILES * PROJ_TILE), BF16),
        grid=(s // tm, N_MAIN_TILES),
        in_specs=[
            pl.BlockSpec((tm, d), lambda i, j: (i, 0)),
            pl.BlockSpec((d, PROJ_TILE), lambda i, j: (0, src(j))),
            pl.BlockSpec((tm, HEAD_DIM), lambda i, j: (i, 0)),
            pl.BlockSpec((tm, HEAD_DIM), lambda i, j: (i, 0)),
            pl.BlockSpec((1, PROJ_TILE), lambda i, j: (0, jnp.minimum(j, N_GATE_TILES - 1))),
        ],
        out_specs=pl.BlockSpec((tm, PROJ_TILE), lambda i, j: (i, j)),
        compiler_params=_cparams(("arbitrary", "arbitrary"), VMEM_LIMIT),
        name="inproj",
    )(x_bf, w_bf, cos, sin, b_gate)


def _inproj_dil_kernel(x_ref, w_ref, cos_ref, sin_ref, o_ref, scr, *, dilation):
    part = pl.program_id(1)
    acc = jnp.dot(x_ref[...], w_ref[...], preferred_element_type=F32)

    def put(val):
        for c in range(scr.shape[0]):
            scr[c] = val[:, c * HEAD_DIM:(c + 1) * HEAD_DIM]

    @pl.when(part < 2)
    def _():
        scale = jnp.where(part == 0, ATTN_SCALE, 1.0).astype(F32)
        put(_rope(acc, cos_ref[...] * scale, sin_ref[...] * scale))

    @pl.when(part == 2)
    def _():
        put(acc)

    n = scr.shape[1] // dilation
    for r in range(dilation):
        for c in range(scr.shape[0]):
            o_ref[r, :, c * HEAD_DIM:(c + 1) * HEAD_DIM] = (
                scr[c, pl.ds(r, n, stride=dilation), :].astype(BF16))


def _inproj_dil(x_bf, w_bf, cos, sin, g, dilation):
    s, d = x_bf.shape
    tm = min(INPROJ_TM, s)
    n = tm // dilation
    return pl.pallas_call(
        functools.partial(_inproj_dil_kernel, dilation=dilation),
        out_shape=jax.ShapeDtypeStruct((dilation, s // dilation, 3 * PROJ_TILE), BF16),
        grid=(s // tm, 3),
        in_specs=[
            pl.BlockSpec((tm, d), lambda i, p: (i, 0)),
            pl.BlockSpec((d, PROJ_TILE), lambda i, p: (0, DIL_SRC_TILE0 + g + 3 * p)),
            pl.BlockSpec((tm, HEAD_DIM), lambda i, p: (i, 0)),
            pl.BlockSpec((tm, HEAD_DIM), lambda i, p: (i, 0)),
        ],
        out_specs=pl.BlockSpec((dilation, n, PROJ_TILE), lambda i, p: (0, i, p)),
        scratch_shapes=[pltpu.VMEM((PROJ_TILE // HEAD_DIM, tm, HEAD_DIM), F32)],
        compiler_params=_cparams(("arbitrary", "arbitrary"), VMEM_LIMIT),
        name=f"inproj_dil{g}",
    )(x_bf, w_bf, cos, sin)


def _diff_kernel(qi_ref, kj_ref, q_ref, k_ref, v_ref, lam_ref, g_ref, o_ref, m_sc, l_sc, acc_sc,
                 *, lambda_init):
    step = pl.program_id(1)
    qi = qi_ref[step]
    kj = kj_ref[step]
    t = q_ref.shape[0]

    @pl.when(kj == 0)
    def _():
        m_sc[...] = jnp.full(m_sc.shape, NEG, F32)
        l_sc[...] = jnp.zeros(l_sc.shape, F32)
        acc_sc[...] = jnp.zeros(acc_sc.shape, F32)

    def update(masked):
        v = v_ref[...]
        if masked:
            row = lax.broadcasted_iota(I32, (t, t), 0)
            col = lax.broadcasted_iota(I32, (t, t), 1)
            keep = col <= row
        for mm in range(2):
            q = q_ref[:, mm * HEAD_DIM:(mm + 1) * HEAD_DIM]
            k = k_ref[:, mm * HEAD_DIM:(mm + 1) * HEAD_DIM]
            s = lax.dot_general(q, k, (((1,), (1,)), ((), ())), preferred_element_type=F32)
            if masked:
                s = jnp.where(keep, s, NEG)
            m_prev = m_sc[mm]
            m_new = jnp.maximum(m_prev, jnp.max(s, axis=-1, keepdims=True))
            alpha = jnp.exp(m_prev - m_new)
            p = jnp.exp(s - m_new)
            l_sc[mm] = alpha * l_sc[mm] + jnp.sum(p, axis=-1, keepdims=True)
            acc_sc[mm] = alpha * acc_sc[mm] + jnp.dot(p.astype(BF16), v, preferred_element_type=F32)
            m_sc[mm] = m_new

    @pl.when(kj < qi)
    def _():
        update(False)

    @pl.when(kj == qi)
    def _():
        update(True)
        lam_p = lam_ref[...]
        lam = (jnp.exp(jnp.sum(lam_p[0:1] * lam_p[1:2], axis=-1, keepdims=True))
               - jnp.exp(jnp.sum(lam_p[2:3] * lam_p[3:4], axis=-1, keepdims=True))
               + lambda_init)
        o = acc_sc[0] / l_sc[0] - lam * (acc_sc[1] / l_sc[1])
        o = o * lax.rsqrt(jnp.mean(o * o, axis=-1, keepdims=True) + LN_EPS) * g_ref[...]
        o_ref[...] = (o * (1.0 - lambda_init)).astype(o_ref.dtype)


def _diff_attention(proj, lam_p, norm_g, lambda_init):
    s = proj.shape[0]
    t = min(ATTN_T, s)
    nq = s // t
    qi_tab = np.concatenate([np.full(i + 1, i, np.int32) for i in range(nq)])
    kj_tab = np.concatenate([np.arange(i + 1, dtype=np.int32) for i in range(nq)])
    vw = 2 * HEAD_DIM
    grid_spec = pltpu.PrefetchScalarGridSpec(
        num_scalar_prefetch=2,
        grid=(DIFF_HEADS, len(qi_tab)),
        in_specs=[
            pl.BlockSpec((t, vw), lambda h, st, qi, kj: (qi[st], DIFF_COL0 + h)),
            pl.BlockSpec((t, vw), lambda h, st, qi, kj: (kj[st], DIFF_COL0 + DIFF_HEADS + h)),
            pl.BlockSpec((t, vw), lambda h, st, qi, kj: (kj[st], DIFF_COL0 + 2 * DIFF_HEADS + h)),
            pl.BlockSpec((4, HEAD_DIM), lambda h, st, qi, kj: (0, 0)),
            pl.BlockSpec((1, vw), lambda h, st, qi, kj: (0, 0)),
        ],
        out_specs=pl.BlockSpec((t, vw), lambda h, st, qi, kj: (qi[st], h)),
        scratch_shapes=[
            pltpu.VMEM((2, t, 1), F32),
            pltpu.VMEM((2, t, 1), F32),
            pltpu.VMEM((2, t, vw), F32),
        ],
    )
    return pl.pallas_call(
        functools.partial(_diff_kernel, lambda_init=lambda_init),
        out_shape=jax.ShapeDtypeStruct((s, DIFF_HEADS * vw), BF16),
        grid_spec=grid_spec,
        compiler_params=_cparams(("arbitrary", "arbitrary"), VMEM_LIMIT),
        name="diffattn",
    )(jnp.asarray(qi_tab), jnp.asarray(kj_tab), proj, proj, proj, lam_p, norm_g)


def _dil_kernel(q_ref, kp_ref, kc_ref, vp_ref, vc_ref, o_ref, lse_ref):
    n = pl.program_id(1)
    t = q_ref.shape[0]
    qi = lax.broadcasted_iota(I32, (t, t), 0)
    kj = lax.broadcasted_iota(I32, (t, t), 1)
    keep_prev = (kj >= qi) & (n > 0)
    keep_cur = kj <= qi
    for h in range(DIL_HEADS_PER_GROUP):
        sl = slice(h * HEAD_DIM, (h + 1) * HEAD_DIM)
        q = q_ref[:, sl]
        dn = (((1,), (1,)), ((), ()))
        sp = lax.dot_general(q, kp_ref[:, sl], dn, preferred_element_type=F32)
        sc = lax.dot_general(q, kc_ref[:, sl], dn, preferred_element_type=F32)
        sp = jnp.where(keep_prev, sp, NEG)
        sc = jnp.where(keep_cur, sc, NEG)
        m = jnp.maximum(jnp.max(sp, axis=-1, keepdims=True), jnp.max(sc, axis=-1, keepdims=True))
        ep = jnp.exp(sp - m)
        ec = jnp.exp(sc - m)
        den = jnp.sum(ep, axis=-1, keepdims=True) + jnp.sum(ec, axis=-1, keepdims=True)
        acc = (jnp.dot(ep.astype(BF16), vp_ref[:, sl], preferred_element_type=F32)
               + jnp.dot(ec.astype(BF16), vc_ref[:, sl], preferred_element_type=F32))
        o_ref[:, sl] = acc / den
        lse_ref[:, sl] = jnp.broadcast_to(m + jnp.log(den), (t, HEAD_DIM))


def _dilated_group(qkv, g):
    dilation, l, _ = qkv.shape
    t = DIL_T
    cur = lambda part: (lambda r, n: (r, n, part))
    prev = lambda part: (lambda r, n: (r, jnp.maximum(n - 1, 0), part))
    blk = (None, t, PROJ_TILE)
    return pl.pallas_call(
        _dil_kernel,
        out_shape=[jax.ShapeDtypeStruct((dilation, l, DIL_OUT), F32)] * 2,
        grid=(dilation, l // t),
        in_specs=[
            pl.BlockSpec(blk, cur(0)),
            pl.BlockSpec(blk, prev(1)),
            pl.BlockSpec(blk, cur(1)),
            pl.BlockSpec(blk, prev(2)),
            pl.BlockSpec(blk, cur(2)),
        ],
        out_specs=[pl.BlockSpec((None, t, DIL_OUT), lambda r, n: (r, n, 0))] * 2,
        compiler_params=_cparams(("arbitrary", "arbitrary")),
        name=f"dilattn{g}",
    )(qkv, qkv, qkv, qkv, qkv)


def _layer_norm(z, g, b):
    mu = jnp.mean(z, axis=-1, keepdims=True)
    zc = z - mu
    var = jnp.mean(zc * zc, axis=-1, keepdims=True)
    return zc * lax.rsqrt(var + LN_EPS) * g + b


def _merge_kernel(o0, o1, o2, l0, l1, l2, yd_ref, ga_ref, gb_ref, x_ref, wbd_ref, wbl_ref, wo_ref,
                  rwt_ref, g_ref, b_ref, h_ref, lt_ref, scr, *, alpha):
    def positions(ref, slot):
        d = ref.shape[0]
        if d == 1:
            return ref[0]
        heads = scr.shape[1]
        for r in range(d):
            for c in range(heads):
                scr[slot, c, pl.ds(r, ref.shape[1], stride=d), :] = (
                    ref[r, :, c * HEAD_DIM:(c + 1) * HEAD_DIM])
        return jnp.concatenate([scr[slot, c] for c in range(heads)], axis=1)

    la, lb, lc = positions(l0, 0), positions(l1, 0), positions(l2, 1)
    m = jnp.maximum(jnp.maximum(la, lb), lc)
    ea, eb, ec = jnp.exp(la - m), jnp.exp(lb - m), jnp.exp(lc - m)
    num = ea * positions(o0, 0) + eb * positions(o1, 2) + ec * positions(o2, 3)
    ydil = num / (ea + eb + ec)
    a = jnp.dot(yd_ref[...], wbd_ref[...], preferred_element_type=F32)
    b = jnp.dot(ydil.astype(BF16), wbl_ref[...], preferred_element_type=F32)
    merged = ga_ref[...].astype(F32) * a + gb_ref[...].astype(F32) * b
    z = alpha * x_ref[...] + jnp.dot(merged.astype(BF16), wo_ref[...], preferred_element_type=F32)
    h = _layer_norm(z, g_ref[...], b_ref[...])
    h_ref[...] = h
    lt_ref[...] = lax.dot_general(rwt_ref[...], h.astype(BF16), (((1,), (1,)), ((), ())),
                                  preferred_element_type=F32)


def _merge(dil_o, dil_l, y_diff, proj, x, wbd, wbl, wo, rwt, ln_g, ln_b, alpha):
    s, d = x.shape
    tm = min(MERGE_TM, s)
    row = lambda w: pl.BlockSpec((tm, w), lambda i: (i, 0))
    const = lambda a: pl.BlockSpec(a.shape, lambda i: (0, 0), pipeline_mode=pl.Buffered(1))
    res = lambda a: pl.BlockSpec((a.shape[0], tm // a.shape[0], DIL_OUT), lambda i: (0, i, 0))
    in_specs = ([res(a) for a in dil_o] + [res(a) for a in dil_l] + [
        row(y_diff.shape[1]),
        pl.BlockSpec((tm, d), lambda i: (i, 0)),
        pl.BlockSpec((tm, d), lambda i: (i, 1)),
        row(d), const(wbd), const(wbl), const(wo), const(rwt), const(ln_g), const(ln_b)])
    return pl.pallas_call(
        functools.partial(_merge_kernel, alpha=alpha),
        out_shape=[jax.ShapeDtypeStruct((s, d), F32),
                   jax.ShapeDtypeStruct((N_EXPERTS, s), F32)],
        grid=(s // tm,),
        in_specs=in_specs,
        out_specs=[row(d), pl.BlockSpec((N_EXPERTS, tm), lambda i: (0, i))],
        scratch_shapes=[pltpu.VMEM((4, DIL_HEADS_PER_GROUP, tm, HEAD_DIM), F32)],
        compiler_params=_cparams(("arbitrary",), VMEM_LIMIT),
        name="merge",
    )(*dil_o, *dil_l, y_diff, proj, proj, x, wbd, wbl, wo, rwt, ln_g, ln_b)


def _route_kernel(lt_ref, bias_ref, e_ref, w_ref, r_ref, cnt_ref, carry):
    i = pl.program_id(0)
    tm = lt_ref.shape[1]
    ninf = -jnp.inf

    @pl.when(i == 0)
    def _():
        carry[...] = jnp.zeros(carry.shape, F32)

    sc = jax.nn.sigmoid(lt_ref[...])
    biased = sc + bias_ref[...]
    iog = lax.broadcasted_iota(I32, (GROUP_SIZE, tm), 0)
    blocks, gscore = [], []
    for g in range(N_GROUPS):
        blk = biased[g * GROUP_SIZE:(g + 1) * GROUP_SIZE, :]
        m1 = jnp.max(blk, axis=0, keepdims=True)
        i1 = jnp.min(jnp.where(blk == m1, iog, GROUP_SIZE), axis=0, keepdims=True)
        m2 = jnp.max(jnp.where(iog == i1, ninf, blk), axis=0, keepdims=True)
        blocks.append(blk)
        gscore.append(m1 + m2)
    masked = []
    for g in range(N_GROUPS):
        ahead = jnp.zeros((1, tm), F32)
        for o in range(N_GROUPS):
            if o == g:
                continue
            wins = (gscore[o] >= gscore[g]) if o < g else (gscore[o] > gscore[g])
            ahead = ahead + jnp.where(wins, 1.0, 0.0)
        keep = jnp.broadcast_to(ahead, (GROUP_SIZE, tm)) < TOPK_GROUPS
        masked.append(jnp.where(keep, blocks[g], ninf))
    v = jnp.concatenate(masked, axis=0)
    ioe = lax.broadcasted_iota(I32, (N_EXPERTS, tm), 0)
    idxs, ws = [], []
    sel = jnp.zeros((N_EXPERTS, tm), jnp.bool_)
    for _ in range(TOP_K):
        mx = jnp.max(v, axis=0, keepdims=True)
        idx = jnp.min(jnp.where(v == mx, ioe, N_EXPERTS), axis=0, keepdims=True)
        hit = ioe == idx
        idxs.append(idx)
        ws.append(jnp.sum(jnp.where(hit, sc, 0.0), axis=0, keepdims=True))
        sel = sel | hit
        v = jnp.where(hit, ninf, v)
    wsum = ws[0]
    for wk in ws[1:]:
        wsum = wsum + wk

    self = jnp.where(sel, 1.0, 0.0)
    ta = lax.broadcasted_iota(I32, (tm, tm), 0)
    tb = lax.broadcasted_iota(I32, (tm, tm), 1)
    before = jnp.where(ta < tb, 1.0, 0.0).astype(BF16)
    pos = jnp.dot(self.astype(BF16), before, preferred_element_type=F32) + carry[...]
    for k in range(TOP_K):
        e_ref[k:k + 1, :] = idxs[k]
        w_ref[k:k + 1, :] = ws[k] / wsum * ROUTED_SCALE
        rk = jnp.sum(jnp.where(ioe == idxs[k], pos, 0.0), axis=0, keepdims=True)
        r_ref[k:k + 1, :] = rk.astype(I32)
    total = carry[...] + jnp.sum(self, axis=1, keepdims=True)
    carry[...] = total
    cnt_ref[...] = total.astype(I32)


def _route(logits_t, bias_col):
    e, s = logits_t.shape
    tm = min(ROUTE_TM, s)
    tok = pl.BlockSpec((TOP_K, tm), lambda i: (0, i))
    return pl.pallas_call(
        _route_kernel,
        out_shape=[jax.ShapeDtypeStruct((TOP_K, s), I32),
                   jax.ShapeDtypeStruct((TOP_K, s), F32),
                   jax.ShapeDtypeStruct((TOP_K, s), I32),
                   jax.ShapeDtypeStruct((e, 1), I32)],
        grid=(s // tm,),
        in_specs=[pl.BlockSpec((e, tm), lambda i: (0, i)),
                  pl.BlockSpec((e, 1), lambda i: (0, 0))],
        out_specs=[tok, tok, tok, pl.BlockSpec((e, 1), lambda i: (0, 0))],
        scratch_shapes=[pltpu.VMEM((e, 1), F32)],
        compiler_params=_cparams(("arbitrary",)),
        name="route",
    )(logits_t, bias_col)


def _slots_kernel(e_ref, r_ref, ps_ref, s_ref):
    tm = e_ref.shape[1]
    ioe = lax.broadcasted_iota(I32, (N_EXPERTS, tm), 0)
    ps = ps_ref[...]
    for k in range(TOP_K):
        start = jnp.sum(jnp.where(ioe == e_ref[k:k + 1, :], ps, 0.0), axis=0, keepdims=True)
        s_ref[k:k + 1, :] = start.astype(I32) + r_ref[k:k + 1, :]


def _slots(eidx, rank, pstart_col):
    s = eidx.shape[1]
    tm = min(ROUTE_TM, s)
    tok = pl.BlockSpec((TOP_K, tm), lambda i: (0, i))
    return pl.pallas_call(
        _slots_kernel,
        out_shape=jax.ShapeDtypeStruct((TOP_K, s), I32),
        grid=(s // tm,),
        in_specs=[tok, tok, pl.BlockSpec((N_EXPERTS, 1), lambda i: (0, 0))],
        out_specs=tok,
        compiler_params=_cparams(("arbitrary",)),
        name="slots",
    )(eidx, rank, pstart_col)


def _scatter_kernel(slot_ref, h_ref, xs_ref, sem):
    tm = h_ref.shape[0]

    def issue(r, carry):
        for k in range(TOP_K):
            dst = slot_ref[r * TOP_K + k]
            pltpu.make_async_copy(h_ref.at[pl.ds(r, 1)], xs_ref.at[pl.ds(dst, 1)], sem).start()
        return carry

    lax.fori_loop(0, tm, issue, 0)
    for k in range(TOP_K):
        pltpu.make_async_copy(h_ref, xs_ref.at[pl.ds(0, tm)], sem).wait()


def _scatter_rows(slots_flat, h, n_rows):
    s, d = h.shape
    tm = min(ROW_TM, s)
    return pl.pallas_call(
        _scatter_kernel,
        out_shape=jax.ShapeDtypeStruct((n_rows, d), F32),
        grid=(s // tm,),
        in_specs=[pl.BlockSpec((tm * TOP_K,), lambda i: (i,), memory_space=pltpu.SMEM),
                  pl.BlockSpec((tm, d), lambda i: (i, 0))],
        out_specs=pl.BlockSpec(memory_space=pl.ANY),
        scratch_shapes=[pltpu.SemaphoreType.DMA],
        compiler_params=_cparams(("arbitrary",)),
        name="scatter",
    )(slots_flat, h)


def _experts_kernel(pstart_ref, cnt_ref, valid_ref, xs_ref, wg_ref, wu_ref, wd_ref, ys_ref,
                    wgu_sc, wd_sc, xbuf, ybuf, in_sem, out_sem):
    e = pl.program_id(0)
    last = pl.num_programs(0) - 1
    ff = wg_ref.shape[2]
    chunks_of = lambda ex: (cnt_ref[ex] + (EXP_CHUNK - 1)) // EXP_CHUNK
    g0 = pstart_ref[e] // EXP_CHUNK
    nch = chunks_of(e)
    total = pstart_ref[last] // EXP_CHUNK + chunks_of(last)

    def fetch(g, slot, do):
        valid = valid_ref[g]
        row0 = pl.multiple_of(g * EXP_CHUNK, EXP_CHUNK)

        @pl.when(valid == EXP_CHUNK)
        def _():
            do(pltpu.make_async_copy(xs_ref.at[pl.ds(row0, EXP_CHUNK)], xbuf.at[slot],
                                     in_sem.at[slot]))

        @pl.when(valid < EXP_CHUNK)
        def _():
            off = jnp.int32(0)
            size = EXP_CHUNK // 2
            while size >= SUBLANES:
                @pl.when((valid & size) != 0)
                def _(off=off, size=size):
                    o = pl.multiple_of(off, size)
                    do(pltpu.make_async_copy(xs_ref.at[pl.ds(row0 + o, size)],
                                             xbuf.at[slot, pl.ds(o, size)], in_sem.at[slot]))
                off = off + (valid & size)
                size //= 2
            for j in range(SUBLANES - 1):
                @pl.when(j < (valid & (SUBLANES - 1)))
                def _(off=off, j=j):
                    do(pltpu.make_async_copy(xs_ref.at[pl.ds(row0 + off + j, 1)],
                                             xbuf.at[slot, pl.ds(off + j, 1)], in_sem.at[slot]))

    def out_copy(g, slot):
        rows = pl.ds(pl.multiple_of(g * EXP_CHUNK, EXP_CHUNK), EXP_CHUNK)
        return pltpu.make_async_copy(ybuf.at[slot], ys_ref.at[rows], out_sem.at[slot])

    @pl.when(e == 0)
    def _():
        xbuf[...] = jnp.zeros(xbuf.shape, F32)

        @pl.when(total > 0)
        def _():
            fetch(0, 0, lambda c: c.start())

    @pl.when(nch > 0)
    def _():
        wgu_sc[:, :ff] = wg_ref[0].astype(BF16)
        wgu_sc[:, ff:] = wu_ref[0].astype(BF16)
        wd_sc[...] = wd_ref[0].astype(BF16)

        def chunk(c, carry):
            g = g0 + c
            slot = g % 2
            fetch(g, slot, lambda cp: cp.wait())

            @pl.when(g + 1 < total)
            def _():
                fetch(g + 1, 1 - slot, lambda cp: cp.start())

            gu = jnp.dot(xbuf[slot].astype(BF16), wgu_sc[...], preferred_element_type=F32)
            hmid = jax.nn.silu(gu[:, :ff]) * gu[:, ff:]
            y = jnp.dot(hmid.astype(BF16), wd_sc[...], preferred_element_type=F32)

            @pl.when(g >= 2)
            def _():
                out_copy(g - 2, slot).wait()

            ybuf[slot] = y
            out_copy(g, slot).start()
            return carry

        lax.fori_loop(0, nch, chunk, 0)

    @pl.when(e == last)
    def _():
        @pl.when(total >= 2)
        def _():
            out_copy(total - 2, total % 2).wait()

        @pl.when(total >= 1)
        def _():
            out_copy(total - 1, (total - 1) % 2).wait()

        n_all = ys_ref.shape[0] // EXP_CHUNK
        ybuf[0] = jnp.zeros(ybuf.shape[1:], F32)
        lax.fori_loop(total, n_all, lambda g, c: (out_copy(g, 0).start(), c)[1], 0)
        lax.fori_loop(total, n_all, lambda g, c: (out_copy(g, 0).wait(), c)[1], 0)


def _experts(pstart, counts, chunk_valid, xs, wg, wu, wd):
    n_rows, d = xs.shape
    ne, _, ff = wg.shape
    grid_spec = pltpu.PrefetchScalarGridSpec(
        num_scalar_prefetch=3,
        grid=(ne,),
        in_specs=[
            pl.BlockSpec(memory_space=pl.ANY),
            pl.BlockSpec((1, d, ff), lambda e, ps, cn, va: (e, 0, 0)),
            pl.BlockSpec((1, d, ff), lambda e, ps, cn, va: (e, 0, 0)),
            pl.BlockSpec((1, ff, d), lambda e, ps, cn, va: (e, 0, 0)),
        ],
        out_specs=pl.BlockSpec(memory_space=pl.ANY),
        scratch_shapes=[
            pltpu.VMEM((d, 2 * ff), BF16),
            pltpu.VMEM((ff, d), BF16),
            pltpu.VMEM((2, EXP_CHUNK, d), F32),
            pltpu.VMEM((2, EXP_CHUNK, d), F32),
            pltpu.SemaphoreType.DMA((2,)),
            pltpu.SemaphoreType.DMA((2,)),
        ],
    )
    return pl.pallas_call(
        _experts_kernel,
        out_shape=jax.ShapeDtypeStruct((n_rows, d), F32),
        grid_spec=grid_spec,
        compiler_params=_cparams(("arbitrary",), VMEM_LIMIT),
        name="experts",
    )(pstart, counts, chunk_valid, xs, wg, wu, wd)


def _combine_kernel(slot_ref, ys_ref, wt_ref, h_ref, wgu_ref, wd_ref, g_ref, b_ref, o_ref,
                    gbuf, sem, *, alpha):
    tm = h_ref.shape[0]
    ff = wd_ref.shape[0]

    def issue(r, carry):
        for k in range(TOP_K):
            src = slot_ref[r * TOP_K + k]
            pltpu.make_async_copy(ys_ref.at[pl.ds(src, 1)], gbuf.at[k, pl.ds(r, 1)], sem).start()
        return carry

    lax.fori_loop(0, tm, issue, 0)

    h = h_ref[...]
    gu = jnp.dot(h.astype(BF16), wgu_ref[...], preferred_element_type=F32)
    hmid = jax.nn.silu(gu[:, :ff]) * gu[:, ff:]
    acc = jnp.dot(hmid.astype(BF16), wd_ref[...], preferred_element_type=F32)

    for k in range(TOP_K):
        pltpu.make_async_copy(ys_ref.at[pl.ds(0, tm)], gbuf.at[k], sem).wait()
    wt = wt_ref[...]
    for k in range(TOP_K):
        acc = acc + wt[:, k:k + 1] * gbuf[k]
    o_ref[...] = _layer_norm(alpha * h + acc, g_ref[...], b_ref[...])


def _combine(slots_flat, ys, wt, h, wgu_s, wd_s, ln_g, ln_b, alpha):
    s, d = h.shape
    tm = min(ROW_TM, s)
    row = lambda c: pl.BlockSpec((tm, c), lambda i: (i, 0))
    const = lambda a: pl.BlockSpec(a.shape, lambda i: (0, 0), pipeline_mode=pl.Buffered(1))
    return pl.pallas_call(
        functools.partial(_combine_kernel, alpha=alpha),
        out_shape=jax.ShapeDtypeStruct((s, d), F32),
        grid=(s // tm,),
        in_specs=[pl.BlockSpec((tm * TOP_K,), lambda i: (i,), memory_space=pltpu.SMEM),
                  pl.BlockSpec(memory_space=pl.ANY),
                  row(TOP_K), row(d), const(wgu_s), const(wd_s), const(ln_g), const(ln_b)],
        out_specs=row(d),
        scratch_shapes=[pltpu.VMEM((TOP_K, tm, d), F32), pltpu.SemaphoreType.DMA],
        compiler_params=_cparams(("arbitrary",), VMEM_LIMIT),
        name="combine",
    )(slots_flat, ys, wt, h, wgu_s, wd_s, ln_g, ln_b)


def _rope_tables(s):
    half = HEAD_DIM // 2
    inv = ROPE_THETA ** (-jnp.arange(half, dtype=F32) / half)
    ang = jnp.arange(s).astype(F32)[:, None] * inv[None, :]
    cos, sin = jnp.cos(ang), jnp.sin(ang)
    return jnp.concatenate([cos, cos], axis=1), jnp.concatenate([-sin, sin], axis=1)


def _layer(x, l, depth, w_in, b_gate, lambda_q1, lambda_k1, lambda_q2, lambda_k2, diff_norm_g,
           w_branch_diff, w_branch_dil, w_out, ln1_g, ln1_b, router_w, router_bias,
           w_gate_e, w_up_e, w_down_e, w_gate_s, w_up_s, w_down_s, ln2_g, ln2_b):
    s, d = x.shape
    alpha = (2.0 * depth) ** 0.25
    lambda_init = 0.8 - 0.6 * math.exp(-0.3 * l)
    cos, sin = _rope_tables(s)
    x_bf = x.astype(BF16)
    w_bf = w_in[l].astype(BF16)

    proj = _inproj(x_bf, w_bf, cos, sin, b_gate[l].reshape(1, 2 * d))
    lam_p = jnp.stack([lambda_q1[l], lambda_k1[l], lambda_q2[l], lambda_k2[l]]).astype(F32)
    y_diff = _diff_attention(proj, lam_p, diff_norm_g[l].reshape(1, -1), lambda_init)

    dil = [_dilated_group(_inproj_dil(x_bf, w_bf, cos, sin, g, dilation), g)
           for g, (_, dilation) in enumerate(DIL_CONFIGS)]
    h, logits_t = _merge(
        [o for o, _ in dil], [ls for _, ls in dil], y_diff, proj, x,
        w_branch_diff[l].astype(BF16), w_branch_dil[l].astype(BF16), w_out[l].astype(BF16),
        router_w[l].T.astype(BF16), ln1_g[l].reshape(1, d), ln1_b[l].reshape(1, d), alpha)

    eidx, wts, rank, counts = _route(logits_t, router_bias[l].reshape(N_EXPERTS, 1).astype(F32))
    counts = counts.reshape(N_EXPERTS)
    n_chunks = (counts + (EXP_CHUNK - 1)) // EXP_CHUNK
    chunk0 = jnp.cumsum(n_chunks) - n_chunks
    pstart = chunk0 * EXP_CHUNK
    max_chunks = s * TOP_K // EXP_CHUNK + N_EXPERTS
    cid = jnp.arange(max_chunks)
    chunk_expert = jnp.minimum(jnp.searchsorted(chunk0 + n_chunks, cid, side="right"), N_EXPERTS - 1)
    chunk_valid = jnp.clip(counts[chunk_expert] - (cid - chunk0[chunk_expert]) * EXP_CHUNK,
                           0, EXP_CHUNK).astype(I32)

    slots = _slots(eidx, rank, pstart.astype(F32).reshape(N_EXPERTS, 1))
    slots_flat = slots.T.reshape(-1)
    xs = _scatter_rows(slots_flat, h, max_chunks * EXP_CHUNK)
    ys = _experts(pstart.astype(I32), counts, chunk_valid, xs, w_gate_e[l], w_up_e[l], w_down_e[l])
    wgu_s = jnp.concatenate([w_gate_s[l], w_up_s[l]], axis=1).astype(BF16)
    return _combine(slots_flat, ys, wts.T, h, wgu_s, w_down_s[l].astype(BF16),
                    ln2_g[l].reshape(1, d), ln2_b[l].reshape(1, d), alpha)


def kernel(x, w_in, b_gate, lambda_q1, lambda_k1, lambda_q2, lambda_k2, diff_norm_g, w_branch_diff,
           w_branch_dil, w_out, ln1_g, ln1_b, router_w, router_bias, w_gate_e, w_up_e, w_down_e,
           w_gate_s, w_up_s, w_down_s, ln2_g, ln2_b):
    b, s, d = x.shape
    depth = w_in.shape[0]
    outs = []
    for bi in range(b):
        xb = x[bi]
        for l in range(depth):
            xb = _layer(xb, l, depth, w_in, b_gate, lambda_q1, lambda_k1, lambda_q2, lambda_k2,
                        diff_norm_g, w_branch_diff, w_branch_dil, w_out, ln1_g, ln1_b, router_w,
                        router_bias, w_gate_e, w_up_e, w_down_e, w_gate_s, w_up_s, w_down_s,
                        ln2_g, ln2_b)
        outs.append(xb)
    return jnp.stack(outs)
```

```python
import functools
import math

import jax
import jax.numpy as jnp
import numpy as np
from jax import lax
from jax.experimental import pallas as pl
from jax.experimental.pallas import tpu as pltpu

F32 = jnp.float32
BF16 = jnp.bfloat16
I32 = jnp.int32

HEAD_DIM = 128
ROPE_THETA = 10000.0
LN_EPS = 1e-5
DIFF_HEADS = 4
DIL_CONFIGS = ((128, 1), (512, 4), (2048, 16))
DIL_HEADS_PER_GROUP = 4
DIL_OUT = DIL_HEADS_PER_GROUP * HEAD_DIM
N_EXPERTS = 256
TOP_K = 8
N_GROUPS = 8
TOPK_GROUPS = 4
GROUP_SIZE = N_EXPERTS // N_GROUPS
ROUTED_SCALE = 2.5
ATTN_SCALE = HEAD_DIM ** -0.5
LOG2E = math.log2(math.e)

PROJ_TILE = 512
DIL_SRC_TILE0 = 6
GATE_SRC_TILE0 = 15
N_GATE_TILES = 8
N_MAIN_TILES = N_GATE_TILES + 6
DIFF_COL0 = N_GATE_TILES * PROJ_TILE // (2 * HEAD_DIM)

INPROJ_TM = 1024
ATTN_TQ = 512
ATTN_TK = 1024
LANES = 128
DIL_T = 128
MERGE_TM = 256
ROUTE_TM = 512
ROW_TM = 128
EXP_CHUNK = 128
SUBLANES = 8
NEG = -1e30
VMEM_LIMIT = 56 * 1024 * 1024


def _cparams(sem, vmem=None):
    return pltpu.CompilerParams(dimension_semantics=sem, vmem_limit_bytes=vmem)


def _rope(a, cos, sin):
    outs = []
    for h in range(a.shape[1] // HEAD_DIM):
        ah = a[:, h * HEAD_DIM:(h + 1) * HEAD_DIM]
        outs.append(ah * cos + pltpu.roll(ah, HEAD_DIM // 2, 1) * sin)
    return jnp.concatenate(outs, axis=1)


def _inproj_kernel(x_ref, w_ref, cos_ref, sin_ref, b_ref, o_ref):
    j = pl.program_id(1)
    acc = jnp.dot(x_ref[...], w_ref[...], preferred_element_type=F32)
    is_gate = j < N_GATE_TILES
    is_q = (j >= N_GATE_TILES) & (j < N_GATE_TILES + 2)
    is_rope = (j >= N_GATE_TILES) & (j < N_GATE_TILES + 4)

    @pl.when(is_gate)
    def _():
        o_ref[...] = jax.nn.sigmoid(acc + b_ref[...]).astype(BF16)

    @pl.when(is_rope)
    def _():
        scale = jnp.where(is_q, ATTN_SCALE * LOG2E, 1.0).astype(F32)
        o_ref[...] = _rope(acc, cos_ref[...] * scale, sin_ref[...] * scale).astype(BF16)

    @pl.when(j >= N_GATE_TILES + 4)
    def _():
        o_ref[...] = acc.astype(BF16)


def _inproj(x_bf, w_bf, cos, sin, b_gate):
    s, d = x_bf.shape
    tm = min(INPROJ_TM, s)
    src = lambda j: jnp.where(j < N_GATE_TILES, j + GATE_SRC_TILE0, j - N_GATE_TILES)
    return pl.pallas_call(
        _inproj_kernel,
        out_shape=jax.ShapeDtypeStruct((s, N_MAIN_TILES * PROJ_TILE), BF16),
        grid=(s // tm, N_MAIN_TILES),
        in_specs=[
            pl.BlockSpec((tm, d), lambda i, j: (i, 0)),
            pl.BlockSpec((d, PROJ_TILE), lambda i, j: (0, src(j))),
            pl.BlockSpec((tm, HEAD_DIM), lambda i, j: (i, 0)),
            pl.BlockSpec((tm, HEAD_DIM), lambda i, j: (i, 0)),
            pl.BlockSpec((1, PROJ_TILE), lambda i, j: (0, jnp.minimum(j, N_GATE_TILES - 1))),
        ],
        out_specs=pl.BlockSpec((tm, PROJ_TILE), lambda i, j: (i, j)),
        compiler_params=_cparams(("arbitrary", "arbitrary"), VMEM_LIMIT),
        name="inproj",
    )(x_bf, w_bf, cos, sin, b_gate)


def _inproj_dil_kernel(x_ref, w_ref, cos_ref, sin_ref, o_ref, scr, *, dilation):
    part = pl.program_id(1)
    acc = jnp.dot(x_ref[...], w_ref[...], preferred_element_type=F32)

    def put(val):
        for c in range(scr.shape[0]):
            scr[c] = val[:, c * HEAD_DIM:(c + 1) * HEAD_DIM]

    @pl.when(part < 2)
    def _():
        scale = jnp.where(part == 0, ATTN_SCALE, 1.0).astype(F32)
        put(_rope(acc, cos_ref[...] * scale, sin_ref[...] * scale))

    @pl.when(part == 2)
    def _():
        put(acc)

    n = scr.shape[1] // dilation
    for r in range(dilation):
        for c in range(scr.shape[0]):
            o_ref[r, :, c * HEAD_DIM:(c + 1) * HEAD_DIM] = (
                scr[c, pl.ds(r, n, stride=dilation), :].astype(BF16))


def _inproj_dil(x_bf, w_bf, cos, sin, g, dilation):
    s, d = x_bf.shape
    tm = min(INPROJ_TM, s)
    n = tm // dilation
    return pl.pallas_call(
        functools.partial(_inproj_dil_kernel, dilation=dilation),
        out_shape=jax.ShapeDtypeStruct((dilation, s // dilation, 3 * PROJ_TILE), BF16),
        grid=(s // tm, 3),
        in_specs=[
            pl.BlockSpec((tm, d), lambda i, p: (i, 0)),
            pl.BlockSpec((d, PROJ_TILE), lambda i, p: (0, DIL_SRC_TILE0 + g + 3 * p)),
            pl.BlockSpec((tm, HEAD_DIM), lambda i, p: (i, 0)),
            pl.BlockSpec((tm, HEAD_DIM), lambda i, p: (i, 0)),
        ],
        out_specs=pl.BlockSpec((dilation, n, PROJ_TILE), lambda i, p: (0, i, p)),
        scratch_shapes=[pltpu.VMEM((PROJ_TILE // HEAD_DIM, tm, HEAD_DIM), F32)],
        compiler_params=_cparams(("arbitrary", "arbitrary"), VMEM_LIMIT),
        name=f"inproj_dil{g}",
    )(x_bf, w_bf, cos, sin)


def _diff_kernel(qi_ref, kj_ref, q_ref, k_ref, v_ref, lam_ref, g_ref, o_ref, m_sc, l_sc, acc_sc,
                 *, lambda_init):
    step = pl.program_id(1)
    qi = qi_ref[step]
    kj = kj_ref[step]
    tq, tk = q_ref.shape[0], k_ref.shape[0]
    last_kj = qi // (tk // tq)

    @pl.when(kj == 0)
    def _():
        m_sc[...] = jnp.full(m_sc.shape, NEG, F32)
        l_sc[...] = jnp.zeros(l_sc.shape, F32)
        acc_sc[...] = jnp.zeros(acc_sc.shape, F32)

    def update(masked):
        v = v_ref[...]
        if masked:
            row = qi * tq + lax.broadcasted_iota(I32, (tq, tk), 0)
            col = kj * tk + lax.broadcasted_iota(I32, (tq, tk), 1)
            keep = col <= row
        for mm in range(2):
            q = q_ref[:, mm * HEAD_DIM:(mm + 1) * HEAD_DIM]
            k = k_ref[:, mm * HEAD_DIM:(mm + 1) * HEAD_DIM]
            s = lax.dot_general(q, k, (((1,), (1,)), ((), ())), preferred_element_type=F32)
            if masked:
                s = jnp.where(keep, s, NEG)
            m_prev = m_sc[mm]
            m_new = jnp.maximum(m_prev, jnp.max(s, axis=-1, keepdims=True))
            alpha = jnp.exp2(m_prev - m_new)
            p = jnp.exp2(s - jnp.tile(m_new, (1, tk // LANES)))
            l_sc[mm] = alpha * l_sc[mm] + jnp.sum(p, axis=-1, keepdims=True)
            pv = jnp.dot(p.astype(BF16), v, preferred_element_type=F32)
            acc_sc[mm] = jnp.tile(alpha, (1, v.shape[1] // LANES)) * acc_sc[mm] + pv
            m_sc[mm] = m_new

    @pl.when(kj < last_kj)
    def _():
        update(False)

    @pl.when(kj == last_kj)
    def _():
        update(True)
        lam_p = lam_ref[...]
        lam = (jnp.exp(jnp.sum(lam_p[0:1] * lam_p[1:2], axis=-1, keepdims=True))
               - jnp.exp(jnp.sum(lam_p[2:3] * lam_p[3:4], axis=-1, keepdims=True))
               + lambda_init)
        rep = acc_sc.shape[2] // LANES
        o = (acc_sc[0] / jnp.tile(l_sc[0], (1, rep))
             - lam * (acc_sc[1] / jnp.tile(l_sc[1], (1, rep))))
        o = o * lax.rsqrt(jnp.mean(o * o, axis=-1, keepdims=True) + LN_EPS) * g_ref[...]
        o_ref[...] = (o * (1.0 - lambda_init)).astype(o_ref.dtype)


def _diff_attention(proj, lam_p, norm_g, lambda_init):
    s = proj.shape[0]
    tq = min(ATTN_TQ, s)
    tk = min(ATTN_TK, s)
    pairs = [(i, j) for i in range(s // tq) for j in range(i * tq // tk + 1)]
    qi_tab = np.asarray([p[0] for p in pairs], np.int32)
    kj_tab = np.asarray([p[1] for p in pairs], np.int32)
    vw = 2 * HEAD_DIM
    grid_spec = pltpu.PrefetchScalarGridSpec(
        num_scalar_prefetch=2,
        grid=(DIFF_HEADS, len(pairs)),
        in_specs=[
            pl.BlockSpec((tq, vw), lambda h, st, qi, kj: (qi[st], DIFF_COL0 + h)),
            pl.BlockSpec((tk, vw), lambda h, st, qi, kj: (kj[st], DIFF_COL0 + DIFF_HEADS + h)),
            pl.BlockSpec((tk, vw), lambda h, st, qi, kj: (kj[st], DIFF_COL0 + 2 * DIFF_HEADS + h)),
            pl.BlockSpec((4, HEAD_DIM), lambda h, st, qi, kj: (0, 0)),
            pl.BlockSpec((1, vw), lambda h, st, qi, kj: (0, 0)),
        ],
        out_specs=pl.BlockSpec((tq, vw), lambda h, st, qi, kj: (qi[st], h)),
        scratch_shapes=[
            pltpu.VMEM((2, tq, LANES), F32),
            pltpu.VMEM((2, tq, LANES), F32),
            pltpu.VMEM((2, tq, vw), F32),
        ],
    )
    return pl.pallas_call(
        functools.partial(_diff_kernel, lambda_init=lambda_init),
        out_shape=jax.ShapeDtypeStruct((s, DIFF_HEADS * vw), BF16),
        grid_spec=grid_spec,
        compiler_params=_cparams(("arbitrary", "arbitrary"), VMEM_LIMIT),
        name="diffattn",
    )(jnp.asarray(qi_tab), jnp.asarray(kj_tab), proj, proj, proj, lam_p, norm_g)


def _dil_kernel(q_ref, kp_ref, kc_ref, vp_ref, vc_ref, o_ref, lse_ref):
    n = pl.program_id(1)
    t = q_ref.shape[0]
    qi = lax.broadcasted_iota(I32, (t, t), 0)
    kj = lax.broadcasted_iota(I32, (t, t), 1)
    keep_prev = (kj >= qi) & (n > 0)
    keep_cur = kj <= qi
    for h in range(DIL_HEADS_PER_GROUP):
        sl = slice(h * HEAD_DIM, (h + 1) * HEAD_DIM)
        q = q_ref[:, sl]
        dn = (((1,), (1,)), ((), ()))
        sp = lax.dot_general(q, kp_ref[:, sl], dn, preferred_element_type=F32)
        sc = lax.dot_general(q, kc_ref[:, sl], dn, preferred_element_type=F32)
        sp = jnp.where(keep_prev, sp, NEG)
        sc = jnp.where(keep_cur, sc, NEG)
        m = jnp.maximum(jnp.max(sp, axis=-1, keepdims=True), jnp.max(sc, axis=-1, keepdims=True))
        ep = jnp.exp(sp - m)
        ec = jnp.exp(sc - m)
        den = jnp.sum(ep, axis=-1, keepdims=True) + jnp.sum(ec, axis=-1, keepdims=True)
        acc = (jnp.dot(ep.astype(BF16), vp_ref[:, sl], preferred_element_type=F32)
               + jnp.dot(ec.astype(BF16), vc_ref[:, sl], preferred_element_type=F32))
        o_ref[:, sl] = acc / den
        lse_ref[:, sl] = jnp.broadcast_to(m + jnp.log(den), (t, HEAD_DIM))


def _dilated_group(qkv, g):
    dilation, l, _ = qkv.shape
    t = DIL_T
    cur = lambda part: (lambda r, n: (r, n, part))
    prev = lambda part: (lambda r, n: (r, jnp.maximum(n - 1, 0), part))
    blk = (None, t, PROJ_TILE)
    return pl.pallas_call(
        _dil_kernel,
        out_shape=[jax.ShapeDtypeStruct((dilation, l, DIL_OUT), F32)] * 2,
        grid=(dilation, l // t),
        in_specs=[
            pl.BlockSpec(blk, cur(0)),
            pl.BlockSpec(blk, prev(1)),
            pl.BlockSpec(blk, cur(1)),
            pl.BlockSpec(blk, prev(2)),
            pl.BlockSpec(blk, cur(2)),
        ],
        out_specs=[pl.BlockSpec((None, t, DIL_OUT), lambda r, n: (r, n, 0))] * 2,
        compiler_params=_cparams(("arbitrary", "arbitrary")),
        name=f"dilattn{g}",
    )(qkv, qkv, qkv, qkv, qkv)


def _layer_norm(z, g, b):
    mu = jnp.mean(z, axis=-1, keepdims=True)
    zc = z - mu
    var = jnp.mean(zc * zc, axis=-1, keepdims=True)
    return zc * lax.rsqrt(var + LN_EPS) * g + b


def _merge_kernel(o0, o1, o2, l0, l1, l2, yd_ref, ga_ref, gb_ref, x_ref, wbd_ref, wbl_ref, wo_ref,
                  rwt_ref, g_ref, b_ref, h_ref, lt_ref, scr, *, alpha):
    def positions(ref, slot):
        d = ref.shape[0]
        if d == 1:
            return ref[0]
        heads = scr.shape[1]
        for r in range(d):
            for c in range(heads):
                scr[slot, c, pl.ds(r, ref.shape[1], stride=d), :] = (
                    ref[r, :, c * HEAD_DIM:(c + 1) * HEAD_DIM])
        return jnp.concatenate([scr[slot, c] for c in range(heads)], axis=1)

    la, lb, lc = positions(l0, 0), positions(l1, 0), positions(l2, 1)
    m = jnp.maximum(jnp.maximum(la, lb), lc)
    ea, eb, ec = jnp.exp(la - m), jnp.exp(lb - m), jnp.exp(lc - m)
    num = ea * positions(o0, 0) + eb * positions(o1, 2) + ec * positions(o2, 3)
    ydil = num / (ea + eb + ec)
    a = jnp.dot(yd_ref[...], wbd_ref[...], preferred_element_type=F32)
    b = jnp.dot(ydil.astype(BF16), wbl_ref[...], preferred_element_type=F32)
    merged = ga_ref[...].astype(F32) * a + gb_ref[...].astype(F32) * b
    z = alpha * x_ref[...] + jnp.dot(merged.astype(BF16), wo_ref[...], preferred_element_type=F32)
    h = _layer_norm(z, g_ref[...], b_ref[...])
    h_ref[...] = h
    lt_ref[...] = lax.dot_general(rwt_ref[...], h.astype(BF16), (((1,), (1,)), ((), ())),
                                  preferred_element_type=F32)


def _merge(dil_o, dil_l, y_diff, proj, x, wbd, wbl, wo, rwt, ln_g, ln_b, alpha):
    s, d = x.shape
    tm = min(MERGE_TM, s)
    row = lambda w: pl.BlockSpec((tm, w), lambda i: (i, 0))
    const = lambda a: pl.BlockSpec(a.shape, lambda i: (0, 0), pipeline_mode=pl.Buffered(1))
    res = lambda a: pl.BlockSpec((a.shape[0], tm // a.shape[0], DIL_OUT), lambda i: (0, i, 0))
    in_specs = ([res(a) for a in dil_o] + [res(a) for a in dil_l] + [
        row(y_diff.shape[1]),
        pl.BlockSpec((tm, d), lambda i: (i, 0)),
        pl.BlockSpec((tm, d), lambda i: (i, 1)),
        row(d), const(wbd), const(wbl), const(wo), const(rwt), const(ln_g), const(ln_b)])
    return pl.pallas_call(
        functools.partial(_merge_kernel, alpha=alpha),
        out_shape=[jax.ShapeDtypeStruct((s, d), F32),
                   jax.ShapeDtypeStruct((N_EXPERTS, s), F32)],
        grid=(s // tm,),
        in_specs=in_specs,
        out_specs=[row(d), pl.BlockSpec((N_EXPERTS, tm), lambda i: (0, i))],
        scratch_shapes=[pltpu.VMEM((4, DIL_HEADS_PER_GROUP, tm, HEAD_DIM), F32)],
        compiler_params=_cparams(("arbitrary",), VMEM_LIMIT),
        name="merge",
    )(*dil_o, *dil_l, y_diff, proj, proj, x, wbd, wbl, wo, rwt, ln_g, ln_b)


def _route_kernel(lt_ref, bias_ref, e_ref, w_ref, r_ref, cnt_ref, carry):
    i = pl.program_id(0)
    tm = lt_ref.shape[1]
    ninf = -jnp.inf

    @pl.when(i == 0)
    def _():
        carry[...] = jnp.zeros(carry.shape, F32)

    sc = jax.nn.sigmoid(lt_ref[...])
    biased = sc + bias_ref[...]
    iog = lax.broadcasted_iota(I32, (GROUP_SIZE, tm), 0)
    blocks, gscore = [], []
    for g in range(N_GROUPS):
        blk = biased[g * GROUP_SIZE:(g + 1) * GROUP_SIZE, :]
        m1 = jnp.max(blk, axis=0, keepdims=True)
        i1 = jnp.min(jnp.where(blk == m1, iog, GROUP_SIZE), axis=0, keepdims=True)
        m2 = jnp.max(jnp.where(iog == i1, ninf, blk), axis=0, keepdims=True)
        blocks.append(blk)
        gscore.append(m1 + m2)
    masked = []
    for g in range(N_GROUPS):
        ahead = jnp.zeros((1, tm), F32)
        for o in range(N_GROUPS):
            if o == g:
                continue
            wins = (gscore[o] >= gscore[g]) if o < g else (gscore[o] > gscore[g])
            ahead = ahead + jnp.where(wins, 1.0, 0.0)
        keep = jnp.broadcast_to(ahead, (GROUP_SIZE, tm)) < TOPK_GROUPS
        masked.append(jnp.where(keep, blocks[g], ninf))
    v = jnp.concatenate(masked, axis=0)
    ioe = lax.broadcasted_iota(I32, (N_EXPERTS, tm), 0)
    idxs, ws = [], []
    sel = jnp.zeros((N_EXPERTS, tm), jnp.bool_)
    for _ in range(TOP_K):
        mx = jnp.max(v, axis=0, keepdims=True)
        idx = jnp.min(jnp.where(v == mx, ioe, N_EXPERTS), axis=0, keepdims=True)
        hit = ioe == idx
        idxs.append(idx)
        ws.append(jnp.sum(jnp.where(hit, sc, 0.0), axis=0, keepdims=True))
        sel = sel | hit
        v = jnp.where(hit, ninf, v)
    wsum = ws[0]
    for wk in ws[1:]:
        wsum = wsum + wk

    self = jnp.where(sel, 1.0, 0.0)
    ta = lax.broadcasted_iota(I32, (tm, tm), 0)
    tb = lax.broadcasted_iota(I32, (tm, tm), 1)
    before = jnp.where(ta < tb, 1.0, 0.0).astype(BF16)
    pos = jnp.dot(self.astype(BF16), before, preferred_element_type=F32) + carry[...]
    for k in range(TOP_K):
        e_ref[k:k + 1, :] = idxs[k]
        w_ref[k:k + 1, :] = ws[k] / wsum * ROUTED_SCALE
        rk = jnp.sum(jnp.where(ioe == idxs[k], pos, 0.0), axis=0, keepdims=True)
        r_ref[k:k + 1, :] = rk.astype(I32)
    total = carry[...] + jnp.sum(self, axis=1, keepdims=True)
    carry[...] = total
    cnt_ref[...] = total.astype(I32)


def _route(logits_t, bias_col):
    e, s = logits_t.shape
    tm = min(ROUTE_TM, s)
    tok = pl.BlockSpec((TOP_K, tm), lambda i: (0, i))
    return pl.pallas_call(
        _route_kernel,
        out_shape=[jax.ShapeDtypeStruct((TOP_K, s), I32),
                   jax.ShapeDtypeStruct((TOP_K, s), F32),
                   jax.ShapeDtypeStruct((TOP_K, s), I32),
                   jax.ShapeDtypeStruct((e, 1), I32)],
        grid=(s // tm,),
        in_specs=[pl.BlockSpec((e, tm), lambda i: (0, i)),
                  pl.BlockSpec((e, 1), lambda i: (0, 0))],
        out_specs=[tok, tok, tok, pl.BlockSpec((e, 1), lambda i: (0, 0))],
        scratch_shapes=[pltpu.VMEM((e, 1), F32)],
        compiler_params=_cparams(("arbitrary",)),
        name="route",
    )(logits_t, bias_col)


def _slots_kernel(e_ref, r_ref, ps_ref, s_ref):
    tm = e_ref.shape[1]
    ioe = lax.broadcasted_iota(I32, (N_EXPERTS, tm), 0)
    ps = ps_ref[...]
    for k in range(TOP_K):
        start = jnp.sum(jnp.where(ioe == e_ref[k:k + 1, :], ps, 0.0), axis=0, keepdims=True)
        s_ref[k:k + 1, :] = start.astype(I32) + r_ref[k:k + 1, :]


def _slots(eidx, rank, pstart_col):
    s = eidx.shape[1]
    tm = min(ROUTE_TM, s)
    tok = pl.BlockSpec((TOP_K, tm), lambda i: (0, i))
    return pl.pallas_call(
        _slots_kernel,
        out_shape=jax.ShapeDtypeStruct((TOP_K, s), I32),
        grid=(s // tm,),
        in_specs=[tok, tok, pl.BlockSpec((N_EXPERTS, 1), lambda i: (0, 0))],
        out_specs=tok,
        compiler_params=_cparams(("arbitrary",)),
        name="slots",
    )(eidx, rank, pstart_col)


def _scatter_kernel(slot_ref, h_ref, xs_ref, sem):
    tm = h_ref.shape[0]

    def issue(r, carry):
        for k in range(TOP_K):
            dst = slot_ref[r * TOP_K + k]
            pltpu.make_async_copy(h_ref.at[pl.ds(r, 1)], xs_ref.at[pl.ds(dst, 1)],
                                  sem).start(priority=k % 2)
        return carry

    lax.fori_loop(0, tm, issue, 0)
    for k in range(TOP_K):
        pltpu.make_async_copy(h_ref, xs_ref.at[pl.ds(0, tm)], sem).wait()


def _scatter_rows(slots_flat, h, n_rows):
    s, d = h.shape
    tm = min(ROW_TM, s)
    return pl.pallas_call(
        _scatter_kernel,
        out_shape=jax.ShapeDtypeStruct((n_rows, d), F32),
        grid=(s // tm,),
        in_specs=[pl.BlockSpec((tm * TOP_K,), lambda i: (i,), memory_space=pltpu.SMEM),
                  pl.BlockSpec((tm, d), lambda i: (i, 0))],
        out_specs=pl.BlockSpec(memory_space=pl.ANY),
        scratch_shapes=[pltpu.SemaphoreType.DMA],
        compiler_params=_cparams(("arbitrary",)),
        name="scatter",
    )(slots_flat, h)


def _experts_kernel(pstart_ref, cnt_ref, valid_ref, xs_ref, wg_ref, wu_ref, wd_ref, ys_ref,
                    wgu_sc, wd_sc, xbuf, ybuf, in_sem, out_sem):
    e = pl.program_id(0)
    last = pl.num_programs(0) - 1
    ff = wg_ref.shape[2]
    chunks_of = lambda ex: (cnt_ref[ex] + (EXP_CHUNK - 1)) // EXP_CHUNK
    g0 = pstart_ref[e] // EXP_CHUNK
    nch = chunks_of(e)
    total = pstart_ref[last] // EXP_CHUNK + chunks_of(last)

    def fetch(g, slot, do):
        valid = valid_ref[g]
        row0 = pl.multiple_of(g * EXP_CHUNK, EXP_CHUNK)

        @pl.when(valid == EXP_CHUNK)
        def _():
            do(pltpu.make_async_copy(xs_ref.at[pl.ds(row0, EXP_CHUNK)], xbuf.at[slot],
                                     in_sem.at[slot]))

        @pl.when(valid < EXP_CHUNK)
        def _():
            off = jnp.int32(0)
            size = EXP_CHUNK // 2
            while size >= SUBLANES:
                @pl.when((valid & size) != 0)
                def _(off=off, size=size):
                    o = pl.multiple_of(off, size)
                    do(pltpu.make_async_copy(xs_ref.at[pl.ds(row0 + o, size)],
                                             xbuf.at[slot, pl.ds(o, size)], in_sem.at[slot]))
                off = off + (valid & size)
                size //= 2
            for j in range(SUBLANES - 1):
                @pl.when(j < (valid & (SUBLANES - 1)))
                def _(off=off, j=j):
                    do(pltpu.make_async_copy(xs_ref.at[pl.ds(row0 + off + j, 1)],
                                             xbuf.at[slot, pl.ds(off + j, 1)], in_sem.at[slot]))

    def out_copy(g, slot):
        rows = pl.ds(pl.multiple_of(g * EXP_CHUNK, EXP_CHUNK), EXP_CHUNK)
        return pltpu.make_async_copy(ybuf.at[slot], ys_ref.at[rows], out_sem.at[slot])

    def start(cp):
        cp.start(priority=1)

    @pl.when(e == 0)
    def _():
        xbuf[...] = jnp.zeros(xbuf.shape, F32)

        @pl.when(total > 0)
        def _():
            fetch(0, 0, start)

    @pl.when(nch > 0)
    def _():
        wgu_sc[:, :ff] = wg_ref[0].astype(BF16)
        wgu_sc[:, ff:] = wu_ref[0].astype(BF16)
        wd_sc[...] = wd_ref[0].astype(BF16)

        def chunk(c, carry):
            g = g0 + c
            slot = g % 2
            fetch(g, slot, lambda cp: cp.wait())

            @pl.when(g + 1 < total)
            def _():
                fetch(g + 1, 1 - slot, start)

            gu = jnp.dot(xbuf[slot].astype(BF16), wgu_sc[...], preferred_element_type=F32)
            hmid = jax.nn.silu(gu[:, :ff]) * gu[:, ff:]
            y = jnp.dot(hmid.astype(BF16), wd_sc[...], preferred_element_type=F32)

            @pl.when(g >= 2)
            def _():
                out_copy(g - 2, slot).wait()

            ybuf[slot] = y
            start(out_copy(g, slot))
            return carry

        lax.fori_loop(0, nch, chunk, 0)

    @pl.when(e == last)
    def _():
        @pl.when(total >= 2)
        def _():
            out_copy(total - 2, total % 2).wait()

        @pl.when(total >= 1)
        def _():
            out_copy(total - 1, (total - 1) % 2).wait()

        n_all = ys_ref.shape[0] // EXP_CHUNK
        ybuf[0] = jnp.zeros(ybuf.shape[1:], F32)
        lax.fori_loop(total, n_all, lambda g, c: (out_copy(g, 0).start(), c)[1], 0)
        lax.fori_loop(total, n_all, lambda g, c: (out_copy(g, 0).wait(), c)[1], 0)


def _experts(pstart, counts, chunk_valid, xs, wg, wu, wd):
    n_rows, d = xs.shape
    ne, _, ff = wg.shape
    grid_spec = pltpu.PrefetchScalarGridSpec(
        num_scalar_prefetch=3,
        grid=(ne,),
        in_specs=[
            pl.BlockSpec(memory_space=pl.ANY),
            pl.BlockSpec((1, d, ff), lambda e, ps, cn, va: (e, 0, 0)),
            pl.BlockSpec((1, d, ff), lambda e, ps, cn, va: (e, 0, 0)),
            pl.BlockSpec((1, ff, d), lambda e, ps, cn, va: (e, 0, 0)),
        ],
        out_specs=pl.BlockSpec(memory_space=pl.ANY),
        scratch_shapes=[
            pltpu.VMEM((d, 2 * ff), BF16),
            pltpu.VMEM((ff, d), BF16),
            pltpu.VMEM((2, EXP_CHUNK, d), F32),
            pltpu.VMEM((2, EXP_CHUNK, d), F32),
            pltpu.SemaphoreType.DMA((2,)),
            pltpu.SemaphoreType.DMA((2,)),
        ],
    )
    return pl.pallas_call(
        _experts_kernel,
        out_shape=jax.ShapeDtypeStruct((n_rows, d), F32),
        grid_spec=grid_spec,
        compiler_params=_cparams(("arbitrary",), VMEM_LIMIT),
        name="experts",
    )(pstart, counts, chunk_valid, xs, wg, wu, wd)


def _combine_kernel(slot_ref, ys_ref, wt_ref, h_ref, wgu_ref, wd_ref, g_ref, b_ref, o_ref,
                    gbuf, sem, *, alpha):
    tm = h_ref.shape[0]
    ff = wd_ref.shape[0]

    def issue(r, carry):
        for k in range(TOP_K):
            src = slot_ref[r * TOP_K + k]
            pltpu.make_async_copy(ys_ref.at[pl.ds(src, 1)], gbuf.at[k, pl.ds(r, 1)],
                                  sem).start(priority=k % 2)
        return carry

    lax.fori_loop(0, tm, issue, 0)

    h = h_ref[...]
    gu = jnp.dot(h.astype(BF16), wgu_ref[...], preferred_element_type=F32)
    hmid = jax.nn.silu(gu[:, :ff]) * gu[:, ff:]
    acc = jnp.dot(hmid.astype(BF16), wd_ref[...], preferred_element_type=F32)

    for k in range(TOP_K):
        pltpu.make_async_copy(ys_ref.at[pl.ds(0, tm)], gbuf.at[k], sem).wait()
    wt = wt_ref[...]
    for k in range(TOP_K):
        acc = acc + wt[:, k:k + 1] * gbuf[k]
    o_ref[...] = _layer_norm(alpha * h + acc, g_ref[...], b_ref[...])


def _combine(slots_flat, ys, wt, h, wgu_s, wd_s, ln_g, ln_b, alpha):
    s, d = h.shape
    tm = min(ROW_TM, s)
    row = lambda c: pl.BlockSpec((tm, c), lambda i: (i, 0))
    const = lambda a: pl.BlockSpec(a.shape, lambda i: (0, 0), pipeline_mode=pl.Buffered(1))
    return pl.pallas_call(
        functools.partial(_combine_kernel, alpha=alpha),
        out_shape=jax.ShapeDtypeStruct((s, d), F32),
        grid=(s // tm,),
        in_specs=[pl.BlockSpec((tm * TOP_K,), lambda i: (i,), memory_space=pltpu.SMEM),
                  pl.BlockSpec(memory_space=pl.ANY),
                  row(TOP_K), row(d), const(wgu_s), const(wd_s), const(ln_g), const(ln_b)],
        out_specs=row(d),
        scratch_shapes=[pltpu.VMEM((TOP_K, tm, d), F32), pltpu.SemaphoreType.DMA],
        compiler_params=_cparams(("arbitrary",), VMEM_LIMIT),
        name="combine",
    )(slots_flat, ys, wt, h, wgu_s, wd_s, ln_g, ln_b)


def _rope_tables(s):
    half = HEAD_DIM // 2
    inv = ROPE_THETA ** (-jnp.arange(half, dtype=F32) / half)
    ang = jnp.arange(s).astype(F32)[:, None] * inv[None, :]
    cos, sin = jnp.cos(ang), jnp.sin(ang)
    return jnp.concatenate([cos, cos], axis=1), jnp.concatenate([-sin, sin], axis=1)


def _layer(x, l, depth, w_in, b_gate, lambda_q1, lambda_k1, lambda_q2, lambda_k2, diff_norm_g,
           w_branch_diff, w_branch_dil, w_out, ln1_g, ln1_b, router_w, router_bias,
           w_gate_e, w_up_e, w_down_e, w_gate_s, w_up_s, w_down_s, ln2_g, ln2_b):
    s, d = x.shape
    alpha = (2.0 * depth) ** 0.25
    lambda_init = 0.8 - 0.6 * math.exp(-0.3 * l)
    cos, sin = _rope_tables(s)
    x_bf = x.astype(BF16)
    w_bf = w_in[l].astype(BF16)

    proj = _inproj(x_bf, w_bf, cos, sin, b_gate[l].reshape(1, 2 * d))
    lam_p = jnp.stack([lambda_q1[l], lambda_k1[l], lambda_q2[l], lambda_k2[l]]).astype(F32)
    y_diff = _diff_attention(proj, lam_p, diff_norm_g[l].reshape(1, -1), lambda_init)

    dil = [_dilated_group(_inproj_dil(x_bf, w_bf, cos, sin, g, dilation), g)
           for g, (_, dilation) in enumerate(DIL_CONFIGS)]
    h, logits_t = _merge(
        [o for o, _ in dil], [ls for _, ls in dil], y_diff, proj, x,
        w_branch_diff[l].astype(BF16), w_branch_dil[l].astype(BF16), w_out[l].astype(BF16),
        router_w[l].T.astype(BF16), ln1_g[l].reshape(1, d), ln1_b[l].reshape(1, d), alpha)

    eidx, wts, rank, counts = _route(logits_t, router_bias[l].reshape(N_EXPERTS, 1).astype(F32))
    counts = counts.reshape(N_EXPERTS)
    n_chunks = (counts + (EXP_CHUNK - 1)) // EXP_CHUNK
    chunk0 = jnp.cumsum(n_chunks) - n_chunks
    pstart = chunk0 * EXP_CHUNK
    max_chunks = s * TOP_K // EXP_CHUNK + N_EXPERTS
    cid = jnp.arange(max_chunks)
    chunk_end = chunk0 + n_chunks
    chunk_expert = jnp.minimum(jnp.sum(chunk_end[None, :] <= cid[:, None], axis=1), N_EXPERTS - 1)
    chunk_valid = jnp.clip(counts[chunk_expert] - (cid - chunk0[chunk_expert]) * EXP_CHUNK,
                           0, EXP_CHUNK).astype(I32)

    slots = _slots(eidx, rank, pstart.astype(F32).reshape(N_EXPERTS, 1))
    slots_flat = slots.T.reshape(-1)
    xs = _scatter_rows(slots_flat, h, max_chunks * EXP_CHUNK)
    ys = _experts(pstart.astype(I32), counts, chunk_valid, xs, w_gate_e[l], w_up_e[l], w_down_e[l])
    wgu_s = jnp.concatenate([w_gate_s[l], w_up_s[l]], axis=1).astype(BF16)
    return _combine(slots_flat, ys, wts.T, h, wgu_s, w_down_s[l].astype(BF16),
                    ln2_g[l].reshape(1, d), ln2_b[l].reshape(1, d), alpha)


def kernel(x, w_in, b_gate, lambda_q1, lambda_k1, lambda_q2, lambda_k2, diff_norm_g, w_branch_diff,
           w_branch_dil, w_out, ln1_g, ln1_b, router_w, router_bias, w_gate_e, w_up_e, w_down_e,
           w_gate_s, w_up_s, w_down_s, ln2_g, ln2_b):
    b, s, d = x.shape
    depth = w_in.shape[0]
    outs = []
    for bi in range(b):
        xb = x[bi]
        for l in range(depth):
            xb = _layer(xb, l, depth, w_in, b_gate, lambda_q1, lambda_k1, lambda_q2, lambda_k2,
                        diff_norm_g, w_branch_diff, w_branch_dil, w_out, ln1_g, ln1_b, router_w,
                        router_bias, w_gate_e, w_up_e, w_down_e, w_gate_s, w_up_s, w_down_s,
                        ln2_g, ln2_b)
        outs.append(xb)
    return jnp.stack(outs)
```

```python
import functools
import math

import jax
import jax.numpy as jnp
import numpy as np
from jax import lax
from jax.experimental import pallas as pl
from jax.experimental.pallas import tpu as pltpu

F32 = jnp.float32
BF16 = jnp.bfloat16
I32 = jnp.int32

HEAD_DIM = 128
ROPE_THETA = 10000.0
LN_EPS = 1e-5
DIFF_HEADS = 4
DIL_CONFIGS = ((128, 1), (512, 4), (2048, 16))
DIL_HEADS_PER_GROUP = 4
DIL_OUT = DIL_HEADS_PER_GROUP * HEAD_DIM
N_EXPERTS = 256
TOP_K = 8
N_GROUPS = 8
TOPK_GROUPS = 4
GROUP_SIZE = N_EXPERTS // N_GROUPS
ROUTED_SCALE = 2.5
ATTN_SCALE = HEAD_DIM ** -0.5
LOG2E = math.log2(math.e)

PROJ_TILE = 512
DIL_SRC_TILE0 = 6
GATE_SRC_TILE0 = 15
N_GATE_TILES = 8
N_MAIN_TILES = N_GATE_TILES + 6
DIFF_COL0 = N_GATE_TILES * PROJ_TILE // (2 * HEAD_DIM)

INPROJ_TM = 1024
ATTN_TQ = 1024
ATTN_TK = 1024
LANES = 128
DIL_T = 128
MERGE_TM = 256
ROUTE_TM = 512
ROW_TM = 128
EXP_CHUNK = 128
W_RING = 3
SUBLANES = 8
NEG = -1e30
VMEM_LIMIT = 56 * 1024 * 1024


def _cparams(sem, vmem=None):
    return pltpu.CompilerParams(dimension_semantics=sem, vmem_limit_bytes=vmem)


def _rope(a, cos, sin):
    outs = []
    for h in range(a.shape[1] // HEAD_DIM):
        ah = a[:, h * HEAD_DIM:(h + 1) * HEAD_DIM]
        outs.append(ah * cos + pltpu.roll(ah, HEAD_DIM // 2, 1) * sin)
    return jnp.concatenate(outs, axis=1)


def _inproj_kernel(x_ref, w_ref, cos_ref, sin_ref, b_ref, o_ref):
    j = pl.program_id(1)
    acc = jnp.dot(x_ref[...], w_ref[...], preferred_element_type=F32)
    is_gate = j < N_GATE_TILES
    is_q = (j >= N_GATE_TILES) & (j < N_GATE_TILES + 2)
    is_rope = (j >= N_GATE_TILES) & (j < N_GATE_TILES + 4)

    @pl.when(is_gate)
    def _():
        o_ref[...] = jax.nn.sigmoid(acc + b_ref[...]).astype(BF16)

    @pl.when(is_rope)
    def _():
        scale = jnp.where(is_q, ATTN_SCALE * LOG2E, 1.0).astype(F32)
        o_ref[...] = _rope(acc, cos_ref[...] * scale, sin_ref[...] * scale).astype(BF16)

    @pl.when(j >= N_GATE_TILES + 4)
    def _():
        o_ref[...] = acc.astype(BF16)


def _inproj(x_bf, w_bf, cos, sin, b_gate):
    s, d = x_bf.shape
    tm = min(INPROJ_TM, s)
    src = lambda j: jnp.where(j < N_GATE_TILES, j + GATE_SRC_TILE0, j - N_GATE_TILES)
    return pl.pallas_call(
        _inproj_kernel,
        out_shape=jax.ShapeDtypeStruct((s, N_MAIN_TILES * PROJ_TILE), BF16),
        grid=(s // tm, N_MAIN_TILES),
        in_specs=[
            pl.BlockSpec((tm, d), lambda i, j: (i, 0)),
            pl.BlockSpec((d, PROJ_TILE), lambda i, j: (0, src(j))),
            pl.BlockSpec((tm, HEAD_DIM), lambda i, j: (i, 0)),
            pl.BlockSpec((tm, HEAD_DIM), lambda i, j: (i, 0)),
            pl.BlockSpec((1, PROJ_TILE), lambda i, j: (0, jnp.minimum(j, N_GATE_TILES - 1))),
        ],
        out_specs=pl.BlockSpec((tm, PROJ_TILE), lambda i, j: (i, j)),
        compiler_params=_cparams(("arbitrary", "arbitrary"), VMEM_LIMIT),
        name="inproj",
    )(x_bf, w_bf, cos, sin, b_gate)


def _inproj_dil_kernel(x_ref, w_ref, cos_ref, sin_ref, o_ref, scr, *, dilation):
    part = pl.program_id(1)
    acc = jnp.dot(x_ref[...], w_ref[...], preferred_element_type=F32)

    def put(val):
        for c in range(scr.shape[0]):
            scr[c] = val[:, c * HEAD_DIM:(c + 1) * HEAD_DIM]

    @pl.when(part < 2)
    def _():
        scale = jnp.where(part == 0, ATTN_SCALE, 1.0).astype(F32)
        put(_rope(acc, cos_ref[...] * scale, sin_ref[...] * scale))

    @pl.when(part == 2)
    def _():
        put(acc)

    n = scr.shape[1] // dilation
    for r in range(dilation):
        for c in range(scr.shape[0]):
            o_ref[r, :, c * HEAD_DIM:(c + 1) * HEAD_DIM] = (
                scr[c, pl.ds(r, n, stride=dilation), :].astype(BF16))


def _inproj_dil(x_bf, w_bf, cos, sin, g, dilation):
    s, d = x_bf.shape
    tm = min(INPROJ_TM, s)
    n = tm // dilation
    return pl.pallas_call(
        functools.partial(_inproj_dil_kernel, dilation=dilation),
        out_shape=jax.ShapeDtypeStruct((dilation, s // dilation, 3 * PROJ_TILE), BF16),
        grid=(s // tm, 3),
        in_specs=[
            pl.BlockSpec((tm, d), lambda i, p: (i, 0)),
            pl.BlockSpec((d, PROJ_TILE), lambda i, p: (0, DIL_SRC_TILE0 + g + 3 * p)),
            pl.BlockSpec((tm, HEAD_DIM), lambda i, p: (i, 0)),
            pl.BlockSpec((tm, HEAD_DIM), lambda i, p: (i, 0)),
        ],
        out_specs=pl.BlockSpec((dilation, n, PROJ_TILE), lambda i, p: (0, i, p)),
        scratch_shapes=[pltpu.VMEM((PROJ_TILE // HEAD_DIM, tm, HEAD_DIM), F32)],
        compiler_params=_cparams(("arbitrary", "arbitrary"), VMEM_LIMIT),
        name=f"inproj_dil{g}",
    )(x_bf, w_bf, cos, sin)


def _diff_kernel(qi_ref, kj_ref, q_ref, k_ref, v_ref, lam_ref, g_ref, o_ref, m_sc, l_sc, acc_sc,
                 *, lambda_init):
    step = pl.program_id(1)
    qi = qi_ref[step]
    kj = kj_ref[step]
    tq, tk = q_ref.shape[0], k_ref.shape[0]
    last_kj = qi // (tk // tq)

    @pl.when(kj == 0)
    def _():
        m_sc[...] = jnp.full(m_sc.shape, NEG, F32)
        l_sc[...] = jnp.zeros(l_sc.shape, F32)
        acc_sc[...] = jnp.zeros(acc_sc.shape, F32)

    def update(masked):
        v = v_ref[...]
        if masked:
            row = qi * tq + lax.broadcasted_iota(I32, (tq, tk), 0)
            col = kj * tk + lax.broadcasted_iota(I32, (tq, tk), 1)
            keep = col <= row
        for mm in range(2):
            q = q_ref[:, mm * HEAD_DIM:(mm + 1) * HEAD_DIM]
            k = k_ref[:, mm * HEAD_DIM:(mm + 1) * HEAD_DIM]
            s = lax.dot_general(q, k, (((1,), (1,)), ((), ())), preferred_element_type=F32)
            if masked:
                s = jnp.where(keep, s, NEG)
            m_prev = m_sc[mm]
            m_new = jnp.maximum(m_prev, jnp.max(s, axis=-1, keepdims=True))
            alpha = jnp.exp2(m_prev - m_new)
            p = jnp.exp2(s - jnp.tile(m_new, (1, tk // LANES)))
            l_sc[mm] = alpha * l_sc[mm] + jnp.sum(p, axis=-1, keepdims=True)
            pv = jnp.dot(p.astype(BF16), v, preferred_element_type=F32)
            acc_sc[mm] = jnp.tile(alpha, (1, v.shape[1] // LANES)) * acc_sc[mm] + pv
            m_sc[mm] = m_new

    @pl.when(kj < last_kj)
    def _():
        update(False)

    @pl.when(kj == last_kj)
    def _():
        update(True)
        lam_p = lam_ref[...]
        lam = (jnp.exp(jnp.sum(lam_p[0:1] * lam_p[1:2], axis=-1, keepdims=True))
               - jnp.exp(jnp.sum(lam_p[2:3] * lam_p[3:4], axis=-1, keepdims=True))
               + lambda_init)
        rep = acc_sc.shape[2] // LANES
        o = (acc_sc[0] / jnp.tile(l_sc[0], (1, rep))
             - lam * (acc_sc[1] / jnp.tile(l_sc[1], (1, rep))))
        o = o * lax.rsqrt(jnp.mean(o * o, axis=-1, keepdims=True) + LN_EPS) * g_ref[...]
        o_ref[...] = (o * (1.0 - lambda_init)).astype(o_ref.dtype)


def _diff_attention(proj, lam_p, norm_g, lambda_init):
    s = proj.shape[0]
    tq = min(ATTN_TQ, s)
    tk = min(ATTN_TK, s)
    pairs = [(i, j) for i in range(s // tq) for j in range(i * tq // tk + 1)]
    qi_tab = np.asarray([p[0] for p in pairs], np.int32)
    kj_tab = np.asarray([p[1] for p in pairs], np.int32)
    vw = 2 * HEAD_DIM
    grid_spec = pltpu.PrefetchScalarGridSpec(
        num_scalar_prefetch=2,
        grid=(DIFF_HEADS, len(pairs)),
        in_specs=[
            pl.BlockSpec((tq, vw), lambda h, st, qi, kj: (qi[st], DIFF_COL0 + h)),
            pl.BlockSpec((tk, vw), lambda h, st, qi, kj: (kj[st], DIFF_COL0 + DIFF_HEADS + h)),
            pl.BlockSpec((tk, vw), lambda h, st, qi, kj: (kj[st], DIFF_COL0 + 2 * DIFF_HEADS + h)),
            pl.BlockSpec((4, HEAD_DIM), lambda h, st, qi, kj: (0, 0)),
            pl.BlockSpec((1, vw), lambda h, st, qi, kj: (0, 0)),
        ],
        out_specs=pl.BlockSpec((tq, vw), lambda h, st, qi, kj: (qi[st], h)),
        scratch_shapes=[
            pltpu.VMEM((2, tq, LANES), F32),
            pltpu.VMEM((2, tq, LANES), F32),
            pltpu.VMEM((2, tq, vw), F32),
        ],
    )
    return pl.pallas_call(
        functools.partial(_diff_kernel, lambda_init=lambda_init),
        out_shape=jax.ShapeDtypeStruct((s, DIFF_HEADS * vw), BF16),
        grid_spec=grid_spec,
        compiler_params=_cparams(("arbitrary", "arbitrary"), VMEM_LIMIT),
        name="diffattn",
    )(jnp.asarray(qi_tab), jnp.asarray(kj_tab), proj, proj, proj, lam_p, norm_g)


def _dil_kernel(q_ref, kp_ref, kc_ref, vp_ref, vc_ref, o_ref, lse_ref):
    n = pl.program_id(1)
    t = q_ref.shape[0]
    qi = lax.broadcasted_iota(I32, (t, t), 0)
    kj = lax.broadcasted_iota(I32, (t, t), 1)
    keep_prev = (kj >= qi) & (n > 0)
    keep_cur = kj <= qi
    for h in range(DIL_HEADS_PER_GROUP):
        sl = slice(h * HEAD_DIM, (h + 1) * HEAD_DIM)
        q = q_ref[:, sl]
        dn = (((1,), (1,)), ((), ()))
        sp = lax.dot_general(q, kp_ref[:, sl], dn, preferred_element_type=F32)
        sc = lax.dot_general(q, kc_ref[:, sl], dn, preferred_element_type=F32)
        sp = jnp.where(keep_prev, sp, NEG)
        sc = jnp.where(keep_cur, sc, NEG)
        m = jnp.maximum(jnp.max(sp, axis=-1, keepdims=True), jnp.max(sc, axis=-1, keepdims=True))
        ep = jnp.exp(sp - m)
        ec = jnp.exp(sc - m)
        den = jnp.sum(ep, axis=-1, keepdims=True) + jnp.sum(ec, axis=-1, keepdims=True)
        acc = (jnp.dot(ep.astype(BF16), vp_ref[:, sl], preferred_element_type=F32)
               + jnp.dot(ec.astype(BF16), vc_ref[:, sl], preferred_element_type=F32))
        o_ref[:, sl] = acc / den
        lse_ref[:, sl] = jnp.broadcast_to(m + jnp.log(den), (t, HEAD_DIM))


def _dilated_group(qkv, g):
    dilation, l, _ = qkv.shape
    t = DIL_T
    cur = lambda part: (lambda r, n: (r, n, part))
    prev = lambda part: (lambda r, n: (r, jnp.maximum(n - 1, 0), part))
    blk = (None, t, PROJ_TILE)
    return pl.pallas_call(
        _dil_kernel,
        out_shape=[jax.ShapeDtypeStruct((dilation, l, DIL_OUT), F32)] * 2,
        grid=(dilation, l // t),
        in_specs=[
            pl.BlockSpec(blk, cur(0)),
            pl.BlockSpec(blk, prev(1)),
            pl.BlockSpec(blk, cur(1)),
            pl.BlockSpec(blk, prev(2)),
            pl.BlockSpec(blk, cur(2)),
        ],
        out_specs=[pl.BlockSpec((None, t, DIL_OUT), lambda r, n: (r, n, 0))] * 2,
        compiler_params=_cparams(("arbitrary", "arbitrary")),
        name=f"dilattn{g}",
    )(qkv, qkv, qkv, qkv, qkv)


def _layer_norm(z, g, b):
    mu = jnp.mean(z, axis=-1, keepdims=True)
    zc = z - mu
    var = jnp.mean(zc * zc, axis=-1, keepdims=True)
    return zc * lax.rsqrt(var + LN_EPS) * g + b


def _merge_kernel(o0, o1, o2, l0, l1, l2, yd_ref, ga_ref, gb_ref, x_ref, wbd_ref, wbl_ref, wo_ref,
                  rwt_ref, g_ref, b_ref, h_ref, lt_ref, scr, *, alpha):
    def positions(ref, slot):
        d = ref.shape[0]
        if d == 1:
            return ref[0]
        heads = scr.shape[1]
        for r in range(d):
            for c in range(heads):
                scr[slot, c, pl.ds(r, ref.shape[1], stride=d), :] = (
                    ref[r, :, c * HEAD_DIM:(c + 1) * HEAD_DIM])
        return jnp.concatenate([scr[slot, c] for c in range(heads)], axis=1)

    la, lb, lc = positions(l0, 0), positions(l1, 0), positions(l2, 1)
    m = jnp.maximum(jnp.maximum(la, lb), lc)
    ea, eb, ec = jnp.exp(la - m), jnp.exp(lb - m), jnp.exp(lc - m)
    num = ea * positions(o0, 0) + eb * positions(o1, 2) + ec * positions(o2, 3)
    ydil = num / (ea + eb + ec)
    a = jnp.dot(yd_ref[...], wbd_ref[...], preferred_element_type=F32)
    b = jnp.dot(ydil.astype(BF16), wbl_ref[...], preferred_element_type=F32)
    merged = ga_ref[...].astype(F32) * a + gb_ref[...].astype(F32) * b
    z = alpha * x_ref[...] + jnp.dot(merged.astype(BF16), wo_ref[...], preferred_element_type=F32)
    h = _layer_norm(z, g_ref[...], b_ref[...])
    h_ref[...] = h
    lt_ref[...] = lax.dot_general(rwt_ref[...], h.astype(BF16), (((1,), (1,)), ((), ())),
                                  preferred_element_type=F32)


def _merge(dil_o, dil_l, y_diff, proj, x, wbd, wbl, wo, rwt, ln_g, ln_b, alpha):
    s, d = x.shape
    tm = min(MERGE_TM, s)
    row = lambda w: pl.BlockSpec((tm, w), lambda i: (i, 0))
    const = lambda a: pl.BlockSpec(a.shape, lambda i: (0, 0), pipeline_mode=pl.Buffered(1))
    res = lambda a: pl.BlockSpec((a.shape[0], tm // a.shape[0], DIL_OUT), lambda i: (0, i, 0))
    in_specs = ([res(a) for a in dil_o] + [res(a) for a in dil_l] + [
        row(y_diff.shape[1]),
        pl.BlockSpec((tm, d), lambda i: (i, 0)),
        pl.BlockSpec((tm, d), lambda i: (i, 1)),
        row(d), const(wbd), const(wbl), const(wo), const(rwt), const(ln_g), const(ln_b)])
    return pl.pallas_call(
        functools.partial(_merge_kernel, alpha=alpha),
        out_shape=[jax.ShapeDtypeStruct((s, d), F32),
                   jax.ShapeDtypeStruct((N_EXPERTS, s), F32)],
        grid=(s // tm,),
        in_specs=in_specs,
        out_specs=[row(d), pl.BlockSpec((N_EXPERTS, tm), lambda i: (0, i))],
        scratch_shapes=[pltpu.VMEM((4, DIL_HEADS_PER_GROUP, tm, HEAD_DIM), F32)],
        compiler_params=_cparams(("arbitrary",), VMEM_LIMIT),
        name="merge",
    )(*dil_o, *dil_l, y_diff, proj, proj, x, wbd, wbl, wo, rwt, ln_g, ln_b)


def _route_kernel(lt_ref, bias_ref, e_ref, w_ref, r_ref, cnt_ref, carry):
    i = pl.program_id(0)
    tm = lt_ref.shape[1]
    ninf = -jnp.inf

    @pl.when(i == 0)
    def _():
        carry[...] = jnp.zeros(carry.shape, F32)

    sc = jax.nn.sigmoid(lt_ref[...])
    biased = sc + bias_ref[...]
    iog = lax.broadcasted_iota(I32, (GROUP_SIZE, tm), 0)
    blocks, gscore = [], []
    for g in range(N_GROUPS):
        blk = biased[g * GROUP_SIZE:(g + 1) * GROUP_SIZE, :]
        m1 = jnp.max(blk, axis=0, keepdims=True)
        i1 = jnp.min(jnp.where(blk == m1, iog, GROUP_SIZE), axis=0, keepdims=True)
        m2 = jnp.max(jnp.where(iog == i1, ninf, blk), axis=0, keepdims=True)
        blocks.append(blk)
        gscore.append(m1 + m2)
    masked = []
    for g in range(N_GROUPS):
        ahead = jnp.zeros((1, tm), F32)
        for o in range(N_GROUPS):
            if o == g:
                continue
            wins = (gscore[o] >= gscore[g]) if o < g else (gscore[o] > gscore[g])
            ahead = ahead + jnp.where(wins, 1.0, 0.0)
        keep = jnp.broadcast_to(ahead, (GROUP_SIZE, tm)) < TOPK_GROUPS
        masked.append(jnp.where(keep, blocks[g], ninf))
    v = jnp.concatenate(masked, axis=0)
    ioe = lax.broadcasted_iota(I32, (N_EXPERTS, tm), 0)
    idxs, ws = [], []
    sel = jnp.zeros((N_EXPERTS, tm), jnp.bool_)
    for _ in range(TOP_K):
        mx = jnp.max(v, axis=0, keepdims=True)
        idx = jnp.min(jnp.where(v == mx, ioe, N_EXPERTS), axis=0, keepdims=True)
        hit = ioe == idx
        idxs.append(idx)
        ws.append(jnp.sum(jnp.where(hit, sc, 0.0), axis=0, keepdims=True))
        sel = sel | hit
        v = jnp.where(hit, ninf, v)
    wsum = ws[0]
    for wk in ws[1:]:
        wsum = wsum + wk

    self = jnp.where(sel, 1.0, 0.0)
    ta = lax.broadcasted_iota(I32, (tm, tm), 0)
    tb = lax.broadcasted_iota(I32, (tm, tm), 1)
    before = jnp.where(ta < tb, 1.0, 0.0).astype(BF16)
    pos = jnp.dot(self.astype(BF16), before, preferred_element_type=F32) + carry[...]
    for k in range(TOP_K):
        e_ref[k:k + 1, :] = idxs[k]
        w_ref[k:k + 1, :] = ws[k] / wsum * ROUTED_SCALE
        rk = jnp.sum(jnp.where(ioe == idxs[k], pos, 0.0), axis=0, keepdims=True)
        r_ref[k:k + 1, :] = rk.astype(I32)
    total = carry[...] + jnp.sum(self, axis=1, keepdims=True)
    carry[...] = total
    cnt_ref[...] = total.astype(I32)


def _route(logits_t, bias_col):
    e, s = logits_t.shape
    tm = min(ROUTE_TM, s)
    tok = pl.BlockSpec((TOP_K, tm), lambda i: (0, i))
    return pl.pallas_call(
        _route_kernel,
        out_shape=[jax.ShapeDtypeStruct((TOP_K, s), I32),
                   jax.ShapeDtypeStruct((TOP_K, s), F32),
                   jax.ShapeDtypeStruct((TOP_K, s), I32),
                   jax.ShapeDtypeStruct((e, 1), I32)],
        grid=(s // tm,),
        in_specs=[pl.BlockSpec((e, tm), lambda i: (0, i)),
                  pl.BlockSpec((e, 1), lambda i: (0, 0))],
        out_specs=[tok, tok, tok, pl.BlockSpec((e, 1), lambda i: (0, 0))],
        scratch_shapes=[pltpu.VMEM((e, 1), F32)],
        compiler_params=_cparams(("arbitrary",)),
        name="route",
    )(logits_t, bias_col)


def _slots_kernel(e_ref, r_ref, ps_ref, s_ref):
    tm = e_ref.shape[1]
    ioe = lax.broadcasted_iota(I32, (N_EXPERTS, tm), 0)
    ps = ps_ref[...]
    for k in range(TOP_K):
        start = jnp.sum(jnp.where(ioe == e_ref[k:k + 1, :], ps, 0.0), axis=0, keepdims=True)
        s_ref[k:k + 1, :] = start.astype(I32) + r_ref[k:k + 1, :]


def _slots(eidx, rank, pstart_col):
    s = eidx.shape[1]
    tm = min(ROUTE_TM, s)
    tok = pl.BlockSpec((TOP_K, tm), lambda i: (0, i))
    return pl.pallas_call(
        _slots_kernel,
        out_shape=jax.ShapeDtypeStruct((TOP_K, s), I32),
        grid=(s // tm,),
        in_specs=[tok, tok, pl.BlockSpec((N_EXPERTS, 1), lambda i: (0, 0))],
        out_specs=tok,
        compiler_params=_cparams(("arbitrary",)),
        name="slots",
    )(eidx, rank, pstart_col)


def _scatter_kernel(pstart_ref, cnt_ref, slot_ref, h_ref, xs_ref, zbuf, sem, zsem):
    i = pl.program_id(0)
    tm = h_ref.shape[0]
    n_exp = cnt_ref.shape[0]

    def zero_fill(do):
        def piece(row, size):
            do(pltpu.make_async_copy(zbuf.at[pl.ds(0, size)], xs_ref.at[pl.ds(row, size)], zsem))

        def expert(ex, carry):
            cnt = cnt_ref[ex]
            row = pstart_ref[ex] + cnt
            pad = (-cnt) & (EXP_CHUNK - 1)
            for j in range(SUBLANES - 1):
                @pl.when(j < (pad & (SUBLANES - 1)))
                def _(row=row, j=j):
                    piece(row + j, 1)
            row = row + (pad & (SUBLANES - 1))
            size = SUBLANES
            while size < EXP_CHUNK:
                @pl.when((pad & size) != 0)
                def _(row=row, size=size):
                    piece(pl.multiple_of(row, size), size)
                row = row + (pad & size)
                size *= 2
            return carry

        lax.fori_loop(0, n_exp, expert, 0)
        end = pstart_ref[n_exp - 1] + cnt_ref[n_exp - 1]
        first = (end + (EXP_CHUNK - 1)) // EXP_CHUNK
        lax.fori_loop(first, xs_ref.shape[0] // EXP_CHUNK,
                      lambda g, c: (piece(pl.multiple_of(g * EXP_CHUNK, EXP_CHUNK), EXP_CHUNK), c)[1],
                      0)

    @pl.when(i == 0)
    def _():
        zbuf[...] = jnp.zeros(zbuf.shape, F32)
        zero_fill(lambda cp: cp.start())

    def issue(r, carry):
        for k in range(TOP_K):
            dst = slot_ref[r * TOP_K + k]
            pltpu.make_async_copy(h_ref.at[pl.ds(r, 1)], xs_ref.at[pl.ds(dst, 1)],
                                  sem).start(priority=k % 2)
        return carry

    lax.fori_loop(0, tm, issue, 0)
    for k in range(TOP_K):
        pltpu.make_async_copy(h_ref, xs_ref.at[pl.ds(0, tm)], sem).wait()

    @pl.when(i == pl.num_programs(0) - 1)
    def _():
        zero_fill(lambda cp: cp.wait())


def _scatter_rows(pstart, counts, slots_flat, h, n_rows):
    s, d = h.shape
    tm = min(ROW_TM, s)
    grid_spec = pltpu.PrefetchScalarGridSpec(
        num_scalar_prefetch=2,
        grid=(s // tm,),
        in_specs=[pl.BlockSpec((tm * TOP_K,), lambda i, ps, cn: (i,), memory_space=pltpu.SMEM),
                  pl.BlockSpec((tm, d), lambda i, ps, cn: (i, 0))],
        out_specs=pl.BlockSpec(memory_space=pl.ANY),
        scratch_shapes=[pltpu.VMEM((EXP_CHUNK, d), F32), pltpu.SemaphoreType.DMA,
                        pltpu.SemaphoreType.DMA],
    )
    return pl.pallas_call(
        _scatter_kernel,
        out_shape=jax.ShapeDtypeStruct((n_rows, d), F32),
        grid_spec=grid_spec,
        compiler_params=_cparams(("arbitrary",)),
        name="scatter",
    )(pstart, counts, slots_flat, h)


def _experts_kernel(pstart_ref, cnt_ref, valid_ref, xs_ref, wg_ref, wu_ref, wd_ref, ys_ref,
                    wg_buf, wu_buf, wd_buf, wgu_sc, wd_sc, xbuf, ybuf, w_sem, in_sem, out_sem):
    e = pl.program_id(0)
    n_exp = pl.num_programs(0)
    last = n_exp - 1
    ff = wg_ref.shape[2]
    chunks_of = lambda ex: (cnt_ref[ex] + (EXP_CHUNK - 1)) // EXP_CHUNK

    def weight_copies(ex):
        ws = ex % W_RING
        return [pltpu.make_async_copy(src.at[ex], buf.at[ws], w_sem.at[i, ws])
                for i, (src, buf) in enumerate(((wg_ref, wg_buf), (wu_ref, wu_buf),
                                                (wd_ref, wd_buf)))]

    @pl.when(e == 0)
    def _():
        for ex in range(W_RING - 1):
            for cp in weight_copies(ex):
                cp.start()

    @pl.when(e + (W_RING - 1) < n_exp)
    def _():
        for cp in weight_copies(e + (W_RING - 1)):
            cp.start()

    for cp in weight_copies(e):
        cp.wait()
    wslot = e % W_RING
    g0 = pstart_ref[e] // EXP_CHUNK
    nch = chunks_of(e)
    total = pstart_ref[last] // EXP_CHUNK + chunks_of(last)

    def fetch(g, slot, do):
        valid = valid_ref[g]
        row0 = pl.multiple_of(g * EXP_CHUNK, EXP_CHUNK)

        @pl.when(valid == EXP_CHUNK)
        def _():
            do(pltpu.make_async_copy(xs_ref.at[pl.ds(row0, EXP_CHUNK)], xbuf.at[slot],
                                     in_sem.at[slot]))

        @pl.when(valid < EXP_CHUNK)
        def _():
            off = jnp.int32(0)
            size = EXP_CHUNK // 2
            while size >= SUBLANES:
                @pl.when((valid & size) != 0)
                def _(off=off, size=size):
                    o = pl.multiple_of(off, size)
                    do(pltpu.make_async_copy(xs_ref.at[pl.ds(row0 + o, size)],
                                             xbuf.at[slot, pl.ds(o, size)], in_sem.at[slot]))
                off = off + (valid & size)
                size //= 2
            for j in range(SUBLANES - 1):
                @pl.when(j < (valid & (SUBLANES - 1)))
                def _(off=off, j=j):
                    do(pltpu.make_async_copy(xs_ref.at[pl.ds(row0 + off + j, 1)],
                                             xbuf.at[slot, pl.ds(off + j, 1)], in_sem.at[slot]))

    def out_copy(g, slot):
        rows = pl.ds(pl.multiple_of(g * EXP_CHUNK, EXP_CHUNK), EXP_CHUNK)
        return pltpu.make_async_copy(ybuf.at[slot], ys_ref.at[rows], out_sem.at[slot])

    def start(cp):
        cp.start(priority=1)

    @pl.when(e == 0)
    def _():
        xbuf[...] = jnp.zeros(xbuf.shape, F32)

        @pl.when(total > 0)
        def _():
            fetch(0, 0, start)

    @pl.when(nch > 0)
    def _():
        wgu_sc[:, :ff] = wg_buf[wslot].astype(BF16)
        wgu_sc[:, ff:] = wu_buf[wslot].astype(BF16)
        wd_sc[...] = wd_buf[wslot].astype(BF16)

        def chunk(c, carry):
            g = g0 + c
            slot = g % 2
            fetch(g, slot, lambda cp: cp.wait())

            @pl.when(g + 1 < total)
            def _():
                fetch(g + 1, 1 - slot, start)

            gu = jnp.dot(xbuf[slot].astype(BF16), wgu_sc[...], preferred_element_type=F32)
            hmid = jax.nn.silu(gu[:, :ff]) * gu[:, ff:]
            y = jnp.dot(hmid.astype(BF16), wd_sc[...], preferred_element_type=F32)

            @pl.when(g >= 2)
            def _():
                out_copy(g - 2, slot).wait()

            ybuf[slot] = y
            start(out_copy(g, slot))
            return carry

        lax.fori_loop(0, nch, chunk, 0)

    @pl.when(e == last)
    def _():
        @pl.when(total >= 2)
        def _():
            out_copy(total - 2, total % 2).wait()

        @pl.when(total >= 1)
        def _():
            out_copy(total - 1, (total - 1) % 2).wait()

        n_all = ys_ref.shape[0] // EXP_CHUNK
        ybuf[0] = jnp.zeros(ybuf.shape[1:], F32)
        lax.fori_loop(total, n_all, lambda g, c: (out_copy(g, 0).start(), c)[1], 0)
        lax.fori_loop(total, n_all, lambda g, c: (out_copy(g, 0).wait(), c)[1], 0)


def _experts(pstart, counts, chunk_valid, xs, wg, wu, wd):
    n_rows, d = xs.shape
    ne, _, ff = wg.shape
    grid_spec = pltpu.PrefetchScalarGridSpec(
        num_scalar_prefetch=3,
        grid=(ne,),
        in_specs=[
            pl.BlockSpec(memory_space=pl.ANY),
            pl.BlockSpec(memory_space=pl.ANY),
            pl.BlockSpec(memory_space=pl.ANY),
            pl.BlockSpec(memory_space=pl.ANY),
        ],
        out_specs=pl.BlockSpec(memory_space=pl.ANY),
        scratch_shapes=[
            pltpu.VMEM((W_RING, d, ff), F32),
            pltpu.VMEM((W_RING, d, ff), F32),
            pltpu.VMEM((W_RING, ff, d), F32),
            pltpu.VMEM((d, 2 * ff), BF16),
            pltpu.VMEM((ff, d), BF16),
            pltpu.VMEM((2, EXP_CHUNK, d), F32),
            pltpu.VMEM((2, EXP_CHUNK, d), F32),
            pltpu.SemaphoreType.DMA((3, W_RING)),
            pltpu.SemaphoreType.DMA((2,)),
            pltpu.SemaphoreType.DMA((2,)),
        ],
    )
    return pl.pallas_call(
        _experts_kernel,
        out_shape=jax.ShapeDtypeStruct((n_rows, d), F32),
        grid_spec=grid_spec,
        compiler_params=_cparams(("arbitrary",), VMEM_LIMIT),
        name="experts",
    )(pstart, counts, chunk_valid, xs, wg, wu, wd)


def _combine_kernel(slot_ref, ys_ref, wt_ref, h_ref, wgu_ref, wd_ref, g_ref, b_ref, o_ref,
                    gbuf, sem, *, alpha):
    tm = h_ref.shape[0]
    ff = wd_ref.shape[0]

    def issue(r, carry):
        for k in range(TOP_K):
            src = slot_ref[r * TOP_K + k]
            pltpu.make_async_copy(ys_ref.at[pl.ds(src, 1)], gbuf.at[k, pl.ds(r, 1)],
                                  sem).start(priority=k % 2)
        return carry

    lax.fori_loop(0, tm, issue, 0)

    h = h_ref[...]
    gu = jnp.dot(h.astype(BF16), wgu_ref[...], preferred_element_type=F32)
    hmid = jax.nn.silu(gu[:, :ff]) * gu[:, ff:]
    acc = jnp.dot(hmid.astype(BF16), wd_ref[...], preferred_element_type=F32)

    for k in range(TOP_K):
        pltpu.make_async_copy(ys_ref.at[pl.ds(0, tm)], gbuf.at[k], sem).wait()
    wt = wt_ref[...]
    for k in range(TOP_K):
        acc = acc + wt[:, k:k + 1] * gbuf[k]
    o_ref[...] = _layer_norm(alpha * h + acc, g_ref[...], b_ref[...])


def _combine(slots_flat, ys, wt, h, wgu_s, wd_s, ln_g, ln_b, alpha):
    s, d = h.shape
    tm = min(ROW_TM, s)
    row = lambda c: pl.BlockSpec((tm, c), lambda i: (i, 0))
    const = lambda a: pl.BlockSpec(a.shape, lambda i: (0, 0), pipeline_mode=pl.Buffered(1))
    return pl.pallas_call(
        functools.partial(_combine_kernel, alpha=alpha),
        out_shape=jax.ShapeDtypeStruct((s, d), F32),
        grid=(s // tm,),
        in_specs=[pl.BlockSpec((tm * TOP_K,), lambda i: (i,), memory_space=pltpu.SMEM),
                  pl.BlockSpec(memory_space=pl.ANY),
                  row(TOP_K), row(d), const(wgu_s), const(wd_s), const(ln_g), const(ln_b)],
        out_specs=row(d),
        scratch_shapes=[pltpu.VMEM((TOP_K, tm, d), F32), pltpu.SemaphoreType.DMA],
        compiler_params=_cparams(("arbitrary",), VMEM_LIMIT),
        name="combine",
    )(slots_flat, ys, wt, h, wgu_s, wd_s, ln_g, ln_b)


def _rope_tables(s):
    half = HEAD_DIM // 2
    inv = ROPE_THETA ** (-jnp.arange(half, dtype=F32) / half)
    ang = jnp.arange(s).astype(F32)[:, None] * inv[None, :]
    cos, sin = jnp.cos(ang), jnp.sin(ang)
    return jnp.concatenate([cos, cos], axis=1), jnp.concatenate([-sin, sin], axis=1)


def _layer(x, l, depth, w_in, b_gate, lambda_q1, lambda_k1, lambda_q2, lambda_k2, diff_norm_g,
           w_branch_diff, w_branch_dil, w_out, ln1_g, ln1_b, router_w, router_bias,
           w_gate_e, w_up_e, w_down_e, w_gate_s, w_up_s, w_down_s, ln2_g, ln2_b):
    s, d = x.shape
    alpha = (2.0 * depth) ** 0.25
    lambda_init = 0.8 - 0.6 * math.exp(-0.3 * l)
    cos, sin = _rope_tables(s)
    x_bf = x.astype(BF16)
    w_bf = w_in[l].astype(BF16)

    proj = _inproj(x_bf, w_bf, cos, sin, b_gate[l].reshape(1, 2 * d))
    lam_p = jnp.stack([lambda_q1[l], lambda_k1[l], lambda_q2[l], lambda_k2[l]]).astype(F32)
    y_diff = _diff_attention(proj, lam_p, diff_norm_g[l].reshape(1, -1), lambda_init)

    dil = [_dilated_group(_inproj_dil(x_bf, w_bf, cos, sin, g, dilation), g)
           for g, (_, dilation) in enumerate(DIL_CONFIGS)]
    h, logits_t = _merge(
        [o for o, _ in dil], [ls for _, ls in dil], y_diff, proj, x,
        w_branch_diff[l].astype(BF16), w_branch_dil[l].astype(BF16), w_out[l].astype(BF16),
        router_w[l].T.astype(BF16), ln1_g[l].reshape(1, d), ln1_b[l].reshape(1, d), alpha)

    eidx, wts, rank, counts = _route(logits_t, router_bias[l].reshape(N_EXPERTS, 1).astype(F32))
    counts = counts.reshape(N_EXPERTS)
    n_chunks = (counts + (EXP_CHUNK - 1)) // EXP_CHUNK
    chunk0 = jnp.cumsum(n_chunks) - n_chunks
    pstart = chunk0 * EXP_CHUNK
    max_chunks = s * TOP_K // EXP_CHUNK + N_EXPERTS
    cid = jnp.arange(max_chunks)
    chunk_end = chunk0 + n_chunks
    chunk_expert = jnp.minimum(jnp.sum(chunk_end[None, :] <= cid[:, None], axis=1), N_EXPERTS - 1)
    chunk_valid = jnp.clip(counts[chunk_expert] - (cid - chunk0[chunk_expert]) * EXP_CHUNK,
                           0, EXP_CHUNK).astype(I32)

    slots = _slots(eidx, rank, pstart.astype(F32).reshape(N_EXPERTS, 1))
    slots_flat = slots.T.reshape(-1)
    pstart = pstart.astype(I32)
    xs = _scatter_rows(pstart, counts, slots_flat, h, max_chunks * EXP_CHUNK)
    ys = _experts(pstart, counts, chunk_valid, xs, w_gate_e[l], w_up_e[l], w_down_e[l])
    wgu_s = jnp.concatenate([w_gate_s[l], w_up_s[l]], axis=1).astype(BF16)
    return _combine(slots_flat, ys, wts.T, h, wgu_s, w_down_s[l].astype(BF16),
                    ln2_g[l].reshape(1, d), ln2_b[l].reshape(1, d), alpha)


def kernel(x, w_in, b_gate, lambda_q1, lambda_k1, lambda_q2, lambda_k2, diff_norm_g, w_branch_diff,
           w_branch_dil, w_out, ln1_g, ln1_b, router_w, router_bias, w_gate_e, w_up_e, w_down_e,
           w_gate_s, w_up_s, w_down_s, ln2_g, ln2_b):
    b, s, d = x.shape
    depth = w_in.shape[0]
    outs = []
    for bi in range(b):
        xb = x[bi]
        for l in range(depth):
            xb = _layer(xb, l, depth, w_in, b_gate, lambda_q1, lambda_k1, lambda_q2, lambda_k2,
                        diff_norm_g, w_branch_diff, w_branch_dil, w_out, ln1_g, ln1_b, router_w,
                        router_bias, w_gate_e, w_up_e, w_down_e, w_gate_s, w_up_s, w_down_s,
                        ln2_g, ln2_b)
        outs.append(xb)
    return jnp.stack(outs)
```

```python
import functools
import math

import jax
import jax.numpy as jnp
import numpy as np
from jax import lax
from jax.experimental import pallas as pl
from jax.experimental.pallas import tpu as pltpu

F32 = jnp.float32
BF16 = jnp.bfloat16
I32 = jnp.int32

HEAD_DIM = 128
ROPE_THETA = 10000.0
LN_EPS = 1e-5
DIFF_HEADS = 4
DIL_CONFIGS = ((128, 1), (512, 4), (2048, 16))
DIL_HEADS_PER_GROUP = 4
DIL_OUT = DIL_HEADS_PER_GROUP * HEAD_DIM
N_EXPERTS = 256
TOP_K = 8
N_GROUPS = 8
TOPK_GROUPS = 4
GROUP_SIZE = N_EXPERTS // N_GROUPS
ROUTED_SCALE = 2.5
ATTN_SCALE = HEAD_DIM ** -0.5
LOG2E = math.log2(math.e)

PROJ_TILE = 512
DIL_SRC_TILE0 = 6
GATE_SRC_TILE0 = 15
N_GATE_TILES = 8
N_MAIN_TILES = N_GATE_TILES + 6
DIFF_COL0 = N_GATE_TILES * PROJ_TILE // (2 * HEAD_DIM)

INPROJ_TM = 1024
ATTN_TQ = 1024
ATTN_TK = 1024
LANES = 128
DIL_T = 128
MERGE_TM = 256
ROUTE_TM = 512
ROW_TM = 128
EXP_CHUNK = 128
W_RING = 3
X_SLOTS = 5
Y_SLOTS = 4
SUBLANES = 8
NEG = -1e30
VMEM_LIMIT = 56 * 1024 * 1024


def _cparams(sem, vmem=None):
    return pltpu.CompilerParams(dimension_semantics=sem, vmem_limit_bytes=vmem)


def _rope(a, cos, sin):
    outs = []
    for h in range(a.shape[1] // HEAD_DIM):
        ah = a[:, h * HEAD_DIM:(h + 1) * HEAD_DIM]
        outs.append(ah * cos + pltpu.roll(ah, HEAD_DIM // 2, 1) * sin)
    return jnp.concatenate(outs, axis=1)


def _inproj_kernel(x_ref, w_ref, cos_ref, sin_ref, b_ref, o_ref):
    j = pl.program_id(1)
    acc = jnp.dot(x_ref[...], w_ref[...], preferred_element_type=F32)
    is_gate = j < N_GATE_TILES
    is_q = (j >= N_GATE_TILES) & (j < N_GATE_TILES + 2)
    is_rope = (j >= N_GATE_TILES) & (j < N_GATE_TILES + 4)

    @pl.when(is_gate)
    def _():
        o_ref[...] = jax.nn.sigmoid(acc + b_ref[...]).astype(BF16)

    @pl.when(is_rope)
    def _():
        scale = jnp.where(is_q, ATTN_SCALE * LOG2E, 1.0).astype(F32)
        o_ref[...] = _rope(acc, cos_ref[...] * scale, sin_ref[...] * scale).astype(BF16)

    @pl.when(j >= N_GATE_TILES + 4)
    def _():
        o_ref[...] = acc.astype(BF16)


def _inproj(x_bf, w_bf, cos, sin, b_gate):
    s, d = x_bf.shape
    tm = min(INPROJ_TM, s)
    src = lambda j: jnp.where(j < N_GATE_TILES, j + GATE_SRC_TILE0, j - N_GATE_TILES)
    return pl.pallas_call(
        _inproj_kernel,
        out_shape=jax.ShapeDtypeStruct((s, N_MAIN_TILES * PROJ_TILE), BF16),
        grid=(s // tm, N_MAIN_TILES),
        in_specs=[
            pl.BlockSpec((tm, d), lambda i, j: (i, 0)),
            pl.BlockSpec((d, PROJ_TILE), lambda i, j: (0, src(j))),
            pl.BlockSpec((tm, HEAD_DIM), lambda i, j: (i, 0)),
            pl.BlockSpec((tm, HEAD_DIM), lambda i, j: (i, 0)),
            pl.BlockSpec((1, PROJ_TILE), lambda i, j: (0, jnp.minimum(j, N_GATE_TILES - 1))),
        ],
        out_specs=pl.BlockSpec((tm, PROJ_TILE), lambda i, j: (i, j)),
        compiler_params=_cparams(("arbitrary", "arbitrary"), VMEM_LIMIT),
        name="inproj",
    )(x_bf, w_bf, cos, sin, b_gate)


def _inproj_dil_kernel(x_ref, w_ref, cos_ref, sin_ref, o_ref, scr, *, dilation):
    part = pl.program_id(1)
    acc = jnp.dot(x_ref[...], w_ref[...], preferred_element_type=F32)

    def put(val):
        for c in range(scr.shape[0]):
            scr[c] = val[:, c * HEAD_DIM:(c + 1) * HEAD_DIM]

    @pl.when(part < 2)
    def _():
        scale = jnp.where(part == 0, ATTN_SCALE, 1.0).astype(F32)
        put(_rope(acc, cos_ref[...] * scale, sin_ref[...] * scale))

    @pl.when(part == 2)
    def _():
        put(acc)

    n = scr.shape[1] // dilation
    for r in range(dilation):
        for c in range(scr.shape[0]):
            o_ref[r, :, c * HEAD_DIM:(c + 1) * HEAD_DIM] = (
                scr[c, pl.ds(r, n, stride=dilation), :].astype(BF16))


def _inproj_dil(x_bf, w_bf, cos, sin, g, dilation):
    s, d = x_bf.shape
    tm = min(INPROJ_TM, s)
    n = tm // dilation
    return pl.pallas_call(
        functools.partial(_inproj_dil_kernel, dilation=dilation),
        out_shape=jax.ShapeDtypeStruct((dilation, s // dilation, 3 * PROJ_TILE), BF16),
        grid=(s // tm, 3),
        in_specs=[
            pl.BlockSpec((tm, d), lambda i, p: (i, 0)),
            pl.BlockSpec((d, PROJ_TILE), lambda i, p: (0, DIL_SRC_TILE0 + g + 3 * p)),
            pl.BlockSpec((tm, HEAD_DIM), lambda i, p: (i, 0)),
            pl.BlockSpec((tm, HEAD_DIM), lambda i, p: (i, 0)),
        ],
        out_specs=pl.BlockSpec((dilation, n, PROJ_TILE), lambda i, p: (0, i, p)),
        scratch_shapes=[pltpu.VMEM((PROJ_TILE // HEAD_DIM, tm, HEAD_DIM), F32)],
        compiler_params=_cparams(("arbitrary", "arbitrary"), VMEM_LIMIT),
        name=f"inproj_dil{g}",
    )(x_bf, w_bf, cos, sin)


def _diff_kernel(qi_ref, kj_ref, q_ref, k_ref, v_ref, lam_ref, g_ref, o_ref, m_sc, l_sc, acc_sc,
                 *, lambda_init):
    step = pl.program_id(1)
    qi = qi_ref[step]
    kj = kj_ref[step]
    tq, tk = q_ref.shape[0], k_ref.shape[0]
    last_kj = qi // (tk // tq)

    @pl.when(kj == 0)
    def _():
        m_sc[...] = jnp.full(m_sc.shape, NEG, F32)
        l_sc[...] = jnp.zeros(l_sc.shape, F32)
        acc_sc[...] = jnp.zeros(acc_sc.shape, F32)

    def update(masked):
        v = v_ref[...]
        if masked:
            row = qi * tq + lax.broadcasted_iota(I32, (tq, tk), 0)
            col = kj * tk + lax.broadcasted_iota(I32, (tq, tk), 1)
            keep = col <= row
        for mm in range(2):
            q = q_ref[:, mm * HEAD_DIM:(mm + 1) * HEAD_DIM]
            k = k_ref[:, mm * HEAD_DIM:(mm + 1) * HEAD_DIM]
            s = lax.dot_general(q, k, (((1,), (1,)), ((), ())), preferred_element_type=F32)
            if masked:
                s = jnp.where(keep, s, NEG)
            m_prev = m_sc[mm]
            m_new = jnp.maximum(m_prev, jnp.max(s, axis=-1, keepdims=True))
            alpha = jnp.exp2(m_prev - m_new)
            p = jnp.exp2(s - jnp.tile(m_new, (1, tk // LANES)))
            l_sc[mm] = alpha * l_sc[mm] + jnp.sum(p, axis=-1, keepdims=True)
            pv = jnp.dot(p.astype(BF16), v, preferred_element_type=F32)
            acc_sc[mm] = jnp.tile(alpha, (1, v.shape[1] // LANES)) * acc_sc[mm] + pv
            m_sc[mm] = m_new

    @pl.when(kj < last_kj)
    def _():
        update(False)

    @pl.when(kj == last_kj)
    def _():
        update(True)
        lam_p = lam_ref[...]
        lam = (jnp.exp(jnp.sum(lam_p[0:1] * lam_p[1:2], axis=-1, keepdims=True))
               - jnp.exp(jnp.sum(lam_p[2:3] * lam_p[3:4], axis=-1, keepdims=True))
               + lambda_init)
        rep = acc_sc.shape[2] // LANES
        o = (acc_sc[0] / jnp.tile(l_sc[0], (1, rep))
             - lam * (acc_sc[1] / jnp.tile(l_sc[1], (1, rep))))
        o = o * lax.rsqrt(jnp.mean(o * o, axis=-1, keepdims=True) + LN_EPS) * g_ref[...]
        o_ref[...] = (o * (1.0 - lambda_init)).astype(o_ref.dtype)


def _diff_attention(proj, lam_p, norm_g, lambda_init):
    s = proj.shape[0]
    tq = min(ATTN_TQ, s)
    tk = min(ATTN_TK, s)
    pairs = [(i, j) for i in range(s // tq) for j in range(i * tq // tk + 1)]
    qi_tab = np.asarray([p[0] for p in pairs], np.int32)
    kj_tab = np.asarray([p[1] for p in pairs], np.int32)
    vw = 2 * HEAD_DIM
    grid_spec = pltpu.PrefetchScalarGridSpec(
        num_scalar_prefetch=2,
        grid=(DIFF_HEADS, len(pairs)),
        in_specs=[
            pl.BlockSpec((tq, vw), lambda h, st, qi, kj: (qi[st], DIFF_COL0 + h)),
            pl.BlockSpec((tk, vw), lambda h, st, qi, kj: (kj[st], DIFF_COL0 + DIFF_HEADS + h)),
            pl.BlockSpec((tk, vw), lambda h, st, qi, kj: (kj[st], DIFF_COL0 + 2 * DIFF_HEADS + h)),
            pl.BlockSpec((4, HEAD_DIM), lambda h, st, qi, kj: (0, 0)),
            pl.BlockSpec((1, vw), lambda h, st, qi, kj: (0, 0)),
        ],
        out_specs=pl.BlockSpec((tq, vw), lambda h, st, qi, kj: (qi[st], h)),
        scratch_shapes=[
            pltpu.VMEM((2, tq, LANES), F32),
            pltpu.VMEM((2, tq, LANES), F32),
            pltpu.VMEM((2, tq, vw), F32),
        ],
    )
    return pl.pallas_call(
        functools.partial(_diff_kernel, lambda_init=lambda_init),
        out_shape=jax.ShapeDtypeStruct((s, DIFF_HEADS * vw), BF16),
        grid_spec=grid_spec,
        compiler_params=_cparams(("arbitrary", "arbitrary"), VMEM_LIMIT),
        name="diffattn",
    )(jnp.asarray(qi_tab), jnp.asarray(kj_tab), proj, proj, proj, lam_p, norm_g)


def _dil_kernel(q_ref, kp_ref, kc_ref, vp_ref, vc_ref, o_ref, lse_ref):
    n = pl.program_id(1)
    t = q_ref.shape[0]
    qi = lax.broadcasted_iota(I32, (t, t), 0)
    kj = lax.broadcasted_iota(I32, (t, t), 1)
    keep_prev = (kj >= qi) & (n > 0)
    keep_cur = kj <= qi
    for h in range(DIL_HEADS_PER_GROUP):
        sl = slice(h * HEAD_DIM, (h + 1) * HEAD_DIM)
        q = q_ref[:, sl]
        dn = (((1,), (1,)), ((), ()))
        sp = lax.dot_general(q, kp_ref[:, sl], dn, preferred_element_type=F32)
        sc = lax.dot_general(q, kc_ref[:, sl], dn, preferred_element_type=F32)
        sp = jnp.where(keep_prev, sp, NEG)
        sc = jnp.where(keep_cur, sc, NEG)
        m = jnp.maximum(jnp.max(sp, axis=-1, keepdims=True), jnp.max(sc, axis=-1, keepdims=True))
        ep = jnp.exp(sp - m)
        ec = jnp.exp(sc - m)
        den = jnp.sum(ep, axis=-1, keepdims=True) + jnp.sum(ec, axis=-1, keepdims=True)
        acc = (jnp.dot(ep.astype(BF16), vp_ref[:, sl], preferred_element_type=F32)
               + jnp.dot(ec.astype(BF16), vc_ref[:, sl], preferred_element_type=F32))
        o_ref[:, sl] = acc / den
        lse_ref[:, sl] = jnp.broadcast_to(m + jnp.log(den), (t, HEAD_DIM))


def _dilated_group(qkv, g):
    dilation, l, _ = qkv.shape
    t = DIL_T
    cur = lambda part: (lambda r, n: (r, n, part))
    prev = lambda part: (lambda r, n: (r, jnp.maximum(n - 1, 0), part))
    blk = (None, t, PROJ_TILE)
    return pl.pallas_call(
        _dil_kernel,
        out_shape=[jax.ShapeDtypeStruct((dilation, l, DIL_OUT), F32)] * 2,
        grid=(dilation, l // t),
        in_specs=[
            pl.BlockSpec(blk, cur(0)),
            pl.BlockSpec(blk, prev(1)),
            pl.BlockSpec(blk, cur(1)),
            pl.BlockSpec(blk, prev(2)),
            pl.BlockSpec(blk, cur(2)),
        ],
        out_specs=[pl.BlockSpec((None, t, DIL_OUT), lambda r, n: (r, n, 0))] * 2,
        compiler_params=_cparams(("arbitrary", "arbitrary")),
        name=f"dilattn{g}",
    )(qkv, qkv, qkv, qkv, qkv)


def _layer_norm(z, g, b):
    mu = jnp.mean(z, axis=-1, keepdims=True)
    zc = z - mu
    var = jnp.mean(zc * zc, axis=-1, keepdims=True)
    return zc * lax.rsqrt(var + LN_EPS) * g + b


def _merge_kernel(o0, o1, o2, l0, l1, l2, yd_ref, ga_ref, gb_ref, x_ref, wbd_ref, wbl_ref, wo_ref,
                  rwt_ref, g_ref, b_ref, h_ref, lt_ref, scr, *, alpha):
    def positions(ref, slot):
        d = ref.shape[0]
        if d == 1:
            return ref[0]
        heads = scr.shape[1]
        for r in range(d):
            for c in range(heads):
                scr[slot, c, pl.ds(r, ref.shape[1], stride=d), :] = (
                    ref[r, :, c * HEAD_DIM:(c + 1) * HEAD_DIM])
        return jnp.concatenate([scr[slot, c] for c in range(heads)], axis=1)

    la, lb, lc = positions(l0, 0), positions(l1, 0), positions(l2, 1)
    m = jnp.maximum(jnp.maximum(la, lb), lc)
    ea, eb, ec = jnp.exp(la - m), jnp.exp(lb - m), jnp.exp(lc - m)
    num = ea * positions(o0, 0) + eb * positions(o1, 2) + ec * positions(o2, 3)
    ydil = num / (ea + eb + ec)
    a = jnp.dot(yd_ref[...], wbd_ref[...], preferred_element_type=F32)
    b = jnp.dot(ydil.astype(BF16), wbl_ref[...], preferred_element_type=F32)
    merged = ga_ref[...].astype(F32) * a + gb_ref[...].astype(F32) * b
    z = alpha * x_ref[...] + jnp.dot(merged.astype(BF16), wo_ref[...], preferred_element_type=F32)
    h = _layer_norm(z, g_ref[...], b_ref[...])
    h_ref[...] = h
    lt_ref[...] = lax.dot_general(rwt_ref[...], h.astype(BF16), (((1,), (1,)), ((), ())),
                                  preferred_element_type=F32)


def _merge(dil_o, dil_l, y_diff, proj, x, wbd, wbl, wo, rwt, ln_g, ln_b, alpha):
    s, d = x.shape
    tm = min(MERGE_TM, s)
    row = lambda w: pl.BlockSpec((tm, w), lambda i: (i, 0))
    const = lambda a: pl.BlockSpec(a.shape, lambda i: (0, 0), pipeline_mode=pl.Buffered(1))
    res = lambda a: pl.BlockSpec((a.shape[0], tm // a.shape[0], DIL_OUT), lambda i: (0, i, 0))
    in_specs = ([res(a) for a in dil_o] + [res(a) for a in dil_l] + [
        row(y_diff.shape[1]),
        pl.BlockSpec((tm, d), lambda i: (i, 0)),
        pl.BlockSpec((tm, d), lambda i: (i, 1)),
        row(d), const(wbd), const(wbl), const(wo), const(rwt), const(ln_g), const(ln_b)])
    return pl.pallas_call(
        functools.partial(_merge_kernel, alpha=alpha),
        out_shape=[jax.ShapeDtypeStruct((s, d), F32),
                   jax.ShapeDtypeStruct((N_EXPERTS, s), F32)],
        grid=(s // tm,),
        in_specs=in_specs,
        out_specs=[row(d), pl.BlockSpec((N_EXPERTS, tm), lambda i: (0, i))],
        scratch_shapes=[pltpu.VMEM((4, DIL_HEADS_PER_GROUP, tm, HEAD_DIM), F32)],
        compiler_params=_cparams(("arbitrary",), VMEM_LIMIT),
        name="merge",
    )(*dil_o, *dil_l, y_diff, proj, proj, x, wbd, wbl, wo, rwt, ln_g, ln_b)


def _route_kernel(lt_ref, bias_ref, e_ref, w_ref, r_ref, cnt_ref, carry):
    i = pl.program_id(0)
    tm = lt_ref.shape[1]
    ninf = -jnp.inf

    @pl.when(i == 0)
    def _():
        carry[...] = jnp.zeros(carry.shape, F32)

    sc = jax.nn.sigmoid(lt_ref[...])
    biased = sc + bias_ref[...]
    iog = lax.broadcasted_iota(I32, (GROUP_SIZE, tm), 0)
    blocks, gscore = [], []
    for g in range(N_GROUPS):
        blk = biased[g * GROUP_SIZE:(g + 1) * GROUP_SIZE, :]
        m1 = jnp.max(blk, axis=0, keepdims=True)
        i1 = jnp.min(jnp.where(blk == m1, iog, GROUP_SIZE), axis=0, keepdims=True)
        m2 = jnp.max(jnp.where(iog == i1, ninf, blk), axis=0, keepdims=True)
        blocks.append(blk)
        gscore.append(m1 + m2)
    masked = []
    for g in range(N_GROUPS):
        ahead = jnp.zeros((1, tm), F32)
        for o in range(N_GROUPS):
            if o == g:
                continue
            wins = (gscore[o] >= gscore[g]) if o < g else (gscore[o] > gscore[g])
            ahead = ahead + jnp.where(wins, 1.0, 0.0)
        keep = jnp.broadcast_to(ahead, (GROUP_SIZE, tm)) < TOPK_GROUPS
        masked.append(jnp.where(keep, blocks[g], ninf))
    v = jnp.concatenate(masked, axis=0)
    ioe = lax.broadcasted_iota(I32, (N_EXPERTS, tm), 0)
    idxs, ws = [], []
    sel = jnp.zeros((N_EXPERTS, tm), jnp.bool_)
    for _ in range(TOP_K):
        mx = jnp.max(v, axis=0, keepdims=True)
        idx = jnp.min(jnp.where(v == mx, ioe, N_EXPERTS), axis=0, keepdims=True)
        hit = ioe == idx
        idxs.append(idx)
        ws.append(jnp.sum(jnp.where(hit, sc, 0.0), axis=0, keepdims=True))
        sel = sel | hit
        v = jnp.where(hit, ninf, v)
    wsum = ws[0]
    for wk in ws[1:]:
        wsum = wsum + wk

    self = jnp.where(sel, 1.0, 0.0)
    ta = lax.broadcasted_iota(I32, (tm, tm), 0)
    tb = lax.broadcasted_iota(I32, (tm, tm), 1)
    before = jnp.where(ta < tb, 1.0, 0.0).astype(BF16)
    pos = jnp.dot(self.astype(BF16), before, preferred_element_type=F32) + carry[...]
    for k in range(TOP_K):
        e_ref[k:k + 1, :] = idxs[k]
        w_ref[k:k + 1, :] = ws[k] / wsum * ROUTED_SCALE
        rk = jnp.sum(jnp.where(ioe == idxs[k], pos, 0.0), axis=0, keepdims=True)
        r_ref[k:k + 1, :] = rk.astype(I32)
    total = carry[...] + jnp.sum(self, axis=1, keepdims=True)
    carry[...] = total
    cnt_ref[...] = total.astype(I32)


def _route(logits_t, bias_col):
    e, s = logits_t.shape
    tm = min(ROUTE_TM, s)
    tok = pl.BlockSpec((TOP_K, tm), lambda i: (0, i))
    return pl.pallas_call(
        _route_kernel,
        out_shape=[jax.ShapeDtypeStruct((TOP_K, s), I32),
                   jax.ShapeDtypeStruct((TOP_K, s), F32),
                   jax.ShapeDtypeStruct((TOP_K, s), I32),
                   jax.ShapeDtypeStruct((e, 1), I32)],
        grid=(s // tm,),
        in_specs=[pl.BlockSpec((e, tm), lambda i: (0, i)),
                  pl.BlockSpec((e, 1), lambda i: (0, 0))],
        out_specs=[tok, tok, tok, pl.BlockSpec((e, 1), lambda i: (0, 0))],
        scratch_shapes=[pltpu.VMEM((e, 1), F32)],
        compiler_params=_cparams(("arbitrary",)),
        name="route",
    )(logits_t, bias_col)


def _slots_kernel(e_ref, r_ref, ps_ref, s_ref):
    tm = e_ref.shape[1]
    ioe = lax.broadcasted_iota(I32, (N_EXPERTS, tm), 0)
    ps = ps_ref[...]
    for k in range(TOP_K):
        start = jnp.sum(jnp.where(ioe == e_ref[k:k + 1, :], ps, 0.0), axis=0, keepdims=True)
        s_ref[k:k + 1, :] = start.astype(I32) + r_ref[k:k + 1, :]


def _slots(eidx, rank, pstart_col):
    s = eidx.shape[1]
    tm = min(ROUTE_TM, s)
    tok = pl.BlockSpec((TOP_K, tm), lambda i: (0, i))
    return pl.pallas_call(
        _slots_kernel,
        out_shape=jax.ShapeDtypeStruct((TOP_K, s), I32),
        grid=(s // tm,),
        in_specs=[tok, tok, pl.BlockSpec((N_EXPERTS, 1), lambda i: (0, 0))],
        out_specs=tok,
        compiler_params=_cparams(("arbitrary",)),
        name="slots",
    )(eidx, rank, pstart_col)


def _scatter_kernel(pstart_ref, cnt_ref, slot_ref, h_ref, xs_ref, zbuf, sem, zsem, *, n_steps):
    i = pl.program_id(0)
    tm = h_ref.shape[0]
    n_exp = cnt_ref.shape[0]

    def zero_fill(do):
        def piece(row, size):
            do(pltpu.make_async_copy(zbuf.at[pl.ds(0, size)], xs_ref.at[pl.ds(row, size)], zsem))

        def expert(ex, carry):
            cnt = cnt_ref[ex]
            row = pstart_ref[ex] + cnt
            pad = (-cnt) & (EXP_CHUNK - 1)
            for j in range(SUBLANES - 1):
                @pl.when(j < (pad & (SUBLANES - 1)))
                def _(row=row, j=j):
                    piece(row + j, 1)
            row = row + (pad & (SUBLANES - 1))
            size = SUBLANES
            while size < EXP_CHUNK:
                @pl.when((pad & size) != 0)
                def _(row=row, size=size):
                    piece(pl.multiple_of(row, size), size)
                row = row + (pad & size)
                size *= 2
            return carry

        per_step = pl.cdiv(n_exp, n_steps)
        lax.fori_loop(i * per_step, jnp.minimum((i + 1) * per_step, n_exp), expert, 0)
        end = pstart_ref[n_exp - 1] + cnt_ref[n_exp - 1]
        first = (end + (EXP_CHUNK - 1)) // EXP_CHUNK
        n_all = xs_ref.shape[0] // EXP_CHUNK
        max_tail = n_all - n_steps * tm * TOP_K // EXP_CHUNK
        tail_per_step = pl.cdiv(max_tail, n_steps)
        lo = first + i * tail_per_step
        lax.fori_loop(lo, jnp.minimum(lo + tail_per_step, n_all),
                      lambda g, c: (piece(pl.multiple_of(g * EXP_CHUNK, EXP_CHUNK), EXP_CHUNK), c)[1],
                      0)

    @pl.when(i == 0)
    def _():
        zbuf[...] = jnp.zeros(zbuf.shape, F32)

    zero_fill(lambda cp: cp.start())

    def issue(r, carry):
        for k in range(TOP_K):
            dst = slot_ref[r * TOP_K + k]
            pltpu.make_async_copy(h_ref.at[pl.ds(r, 1)], xs_ref.at[pl.ds(dst, 1)],
                                  sem).start(priority=k % 2)
        return carry

    lax.fori_loop(0, tm, issue, 0)
    for k in range(TOP_K):
        pltpu.make_async_copy(h_ref, xs_ref.at[pl.ds(0, tm)], sem).wait()

    zero_fill(lambda cp: cp.wait())


def _scatter_rows(pstart, counts, slots_flat, h, n_rows):
    s, d = h.shape
    tm = min(ROW_TM, s)
    grid_spec = pltpu.PrefetchScalarGridSpec(
        num_scalar_prefetch=2,
        grid=(s // tm,),
        in_specs=[pl.BlockSpec((tm * TOP_K,), lambda i, ps, cn: (i,), memory_space=pltpu.SMEM),
                  pl.BlockSpec((tm, d), lambda i, ps, cn: (i, 0))],
        out_specs=pl.BlockSpec(memory_space=pl.ANY),
        scratch_shapes=[pltpu.VMEM((EXP_CHUNK, d), F32), pltpu.SemaphoreType.DMA,
                        pltpu.SemaphoreType.DMA],
    )
    return pl.pallas_call(
        functools.partial(_scatter_kernel, n_steps=s // tm),
        out_shape=jax.ShapeDtypeStruct((n_rows, d), F32),
        grid_spec=grid_spec,
        compiler_params=_cparams(("arbitrary",)),
        name="scatter",
    )(pstart, counts, slots_flat, h)


def _experts_kernel(pstart_ref, cnt_ref, valid_ref, xs_ref, wg_ref, wu_ref, wd_ref, ys_ref,
                    wg_buf, wu_buf, wd_buf, wgu_sc, wd_sc, xbuf, ybuf, w_sem, in_sem, out_sem):
    e = pl.program_id(0)
    n_exp = pl.num_programs(0)
    last = n_exp - 1
    ff = wg_ref.shape[2]
    chunks_of = lambda ex: (cnt_ref[ex] + (EXP_CHUNK - 1)) // EXP_CHUNK

    def weight_copies(ex):
        ws = ex % W_RING
        half = wd_ref.shape[1] // 2
        lo, hi = pl.ds(0, half), pl.ds(half, half)
        parts = ((wg_ref.at[ex], wg_buf.at[ws]), (wu_ref.at[ex], wu_buf.at[ws]),
                 (wd_ref.at[ex, lo], wd_buf.at[ws, lo]), (wd_ref.at[ex, hi], wd_buf.at[ws, hi]))
        return [(pltpu.make_async_copy(src, dst, w_sem.at[i, ws]), 0)
                for i, (src, dst) in enumerate(parts)]

    @pl.when(e == 0)
    def _():
        for ex in range(W_RING - 1):
            for cp, queue in weight_copies(ex):
                cp.start(priority=queue)

    @pl.when(e + (W_RING - 1) < n_exp)
    def _():
        for cp, queue in weight_copies(e + (W_RING - 1)):
            cp.start(priority=queue)

    for cp, _ in weight_copies(e):
        cp.wait()
    wslot = e % W_RING
    g0 = pstart_ref[e] // EXP_CHUNK
    nch = chunks_of(e)
    total = pstart_ref[last] // EXP_CHUNK + chunks_of(last)

    def fetch(g, slot, do):
        valid = valid_ref[g]
        row0 = pl.multiple_of(g * EXP_CHUNK, EXP_CHUNK)

        @pl.when(valid == EXP_CHUNK)
        def _():
            do(pltpu.make_async_copy(xs_ref.at[pl.ds(row0, EXP_CHUNK)], xbuf.at[slot],
                                     in_sem.at[slot]))

        @pl.when(valid < EXP_CHUNK)
        def _():
            off = jnp.int32(0)
            size = EXP_CHUNK // 2
            while size >= SUBLANES:
                @pl.when((valid & size) != 0)
                def _(off=off, size=size):
                    o = pl.multiple_of(off, size)
                    do(pltpu.make_async_copy(xs_ref.at[pl.ds(row0 + o, size)],
                                             xbuf.at[slot, pl.ds(o, size)], in_sem.at[slot]))
                off = off + (valid & size)
                size //= 2
            for j in range(SUBLANES - 1):
                @pl.when(j < (valid & (SUBLANES - 1)))
                def _(off=off, j=j):
                    do(pltpu.make_async_copy(xs_ref.at[pl.ds(row0 + off + j, 1)],
                                             xbuf.at[slot, pl.ds(off + j, 1)], in_sem.at[slot]))

    def out_copy(g, slot):
        rows = pl.ds(pl.multiple_of(g * EXP_CHUNK, EXP_CHUNK), EXP_CHUNK)
        return pltpu.make_async_copy(ybuf.at[slot], ys_ref.at[rows], out_sem.at[slot])

    def start(cp):
        cp.start(priority=1)

    n_x, n_y = xbuf.shape[0], ybuf.shape[0]
    ahead = n_x - 1

    @pl.when(e == 0)
    def _():
        xbuf[...] = jnp.zeros(xbuf.shape, F32)
        for j in range(ahead):
            @pl.when(j < total)
            def _(j=j):
                fetch(j, j, start)

    @pl.when(nch > 0)
    def _():
        wgu_sc[:, :ff] = wg_buf[wslot].astype(BF16)
        wgu_sc[:, ff:] = wu_buf[wslot].astype(BF16)
        wd_sc[...] = wd_buf[wslot].astype(BF16)

        def chunk(c, carry):
            g = g0 + c
            slot = g % n_x
            fetch(g, slot, lambda cp: cp.wait())

            @pl.when(g + ahead < total)
            def _():
                fetch(g + ahead, (g + ahead) % n_x, start)

            gu = jnp.dot(xbuf[slot].astype(BF16), wgu_sc[...], preferred_element_type=F32)
            hmid = jax.nn.silu(gu[:, :ff]) * gu[:, ff:]
            y = jnp.dot(hmid.astype(BF16), wd_sc[...], preferred_element_type=F32)

            yslot = g % n_y

            @pl.when(g >= n_y)
            def _():
                out_copy(g - n_y, yslot).wait()

            ybuf[yslot] = y
            start(out_copy(g, yslot))
            return carry

        lax.fori_loop(0, nch, chunk, 0)

    @pl.when(e == last)
    def _():
        for j in range(n_y):
            @pl.when(total > j)
            def _(j=j):
                out_copy(total - 1 - j, (total - 1 - j) % n_y).wait()

        n_all = ys_ref.shape[0] // EXP_CHUNK
        ybuf[0] = jnp.zeros(ybuf.shape[1:], F32)
        lax.fori_loop(total, n_all, lambda g, c: (out_copy(g, 0).start(), c)[1], 0)
        lax.fori_loop(total, n_all, lambda g, c: (out_copy(g, 0).wait(), c)[1], 0)


def _experts(pstart, counts, chunk_valid, xs, wg, wu, wd):
    n_rows, d = xs.shape
    ne, _, ff = wg.shape
    grid_spec = pltpu.PrefetchScalarGridSpec(
        num_scalar_prefetch=3,
        grid=(ne,),
        in_specs=[
            pl.BlockSpec(memory_space=pl.ANY),
            pl.BlockSpec(memory_space=pl.ANY),
            pl.BlockSpec(memory_space=pl.ANY),
            pl.BlockSpec(memory_space=pl.ANY),
        ],
        out_specs=pl.BlockSpec(memory_space=pl.ANY),
        scratch_shapes=[
            pltpu.VMEM((W_RING, d, ff), F32),
            pltpu.VMEM((W_RING, d, ff), F32),
            pltpu.VMEM((W_RING, ff, d), F32),
            pltpu.VMEM((d, 2 * ff), BF16),
            pltpu.VMEM((ff, d), BF16),
            pltpu.VMEM((X_SLOTS, EXP_CHUNK, d), F32),
            pltpu.VMEM((Y_SLOTS, EXP_CHUNK, d), F32),
            pltpu.SemaphoreType.DMA((4, W_RING)),
            pltpu.SemaphoreType.DMA((X_SLOTS,)),
            pltpu.SemaphoreType.DMA((Y_SLOTS,)),
        ],
    )
    return pl.pallas_call(
        _experts_kernel,
        out_shape=jax.ShapeDtypeStruct((n_rows, d), F32),
        grid_spec=grid_spec,
        compiler_params=_cparams(("arbitrary",), VMEM_LIMIT),
        name="experts",
    )(pstart, counts, chunk_valid, xs, wg, wu, wd)


def _combine_kernel(slot_ref, ys_ref, wt_ref, h_ref, wgu_ref, wd_ref, g_ref, b_ref, o_ref,
                    gbuf, sem, *, alpha):
    tm = h_ref.shape[0]
    ff = wd_ref.shape[0]

    def issue(r, carry):
        for k in range(TOP_K):
            src = slot_ref[r * TOP_K + k]
            pltpu.make_async_copy(ys_ref.at[pl.ds(src, 1)], gbuf.at[k, pl.ds(r, 1)],
                                  sem).start(priority=k % 2)
        return carry

    lax.fori_loop(0, tm, issue, 0)

    h = h_ref[...]
    gu = jnp.dot(h.astype(BF16), wgu_ref[...], preferred_element_type=F32)
    hmid = jax.nn.silu(gu[:, :ff]) * gu[:, ff:]
    acc = jnp.dot(hmid.astype(BF16), wd_ref[...], preferred_element_type=F32)

    for k in range(TOP_K):
        pltpu.make_async_copy(ys_ref.at[pl.ds(0, tm)], gbuf.at[k], sem).wait()
    wt = wt_ref[...]
    for k in range(TOP_K):
        acc = acc + wt[:, k:k + 1] * gbuf[k]
    o_ref[...] = _layer_norm(alpha * h + acc, g_ref[...], b_ref[...])


def _combine(slots_flat, ys, wt, h, wgu_s, wd_s, ln_g, ln_b, alpha):
    s, d = h.shape
    tm = min(ROW_TM, s)
    row = lambda c: pl.BlockSpec((tm, c), lambda i: (i, 0))
    const = lambda a: pl.BlockSpec(a.shape, lambda i: (0, 0), pipeline_mode=pl.Buffered(1))
    return pl.pallas_call(
        functools.partial(_combine_kernel, alpha=alpha),
        out_shape=jax.ShapeDtypeStruct((s, d), F32),
        grid=(s // tm,),
        in_specs=[pl.BlockSpec((tm * TOP_K,), lambda i: (i,), memory_space=pltpu.SMEM),
                  pl.BlockSpec(memory_space=pl.ANY),
                  row(TOP_K), row(d), const(wgu_s), const(wd_s), const(ln_g), const(ln_b)],
        out_specs=row(d),
        scratch_shapes=[pltpu.VMEM((TOP_K, tm, d), F32), pltpu.SemaphoreType.DMA],
        compiler_params=_cparams(("arbitrary",), VMEM_LIMIT),
        name="combine",
    )(slots_flat, ys, wt, h, wgu_s, wd_s, ln_g, ln_b)


def _rope_tables(s):
    half = HEAD_DIM // 2
    inv = ROPE_THETA ** (-jnp.arange(half, dtype=F32) / half)
    ang = jnp.arange(s).astype(F32)[:, None] * inv[None, :]
    cos, sin = jnp.cos(ang), jnp.sin(ang)
    return jnp.concatenate([cos, cos], axis=1), jnp.concatenate([-sin, sin], axis=1)


def _layer(x, l, depth, w_in, b_gate, lambda_q1, lambda_k1, lambda_q2, lambda_k2, diff_norm_g,
           w_branch_diff, w_branch_dil, w_out, ln1_g, ln1_b, router_w, router_bias,
           w_gate_e, w_up_e, w_down_e, w_gate_s, w_up_s, w_down_s, ln2_g, ln2_b):
    s, d = x.shape
    alpha = (2.0 * depth) ** 0.25
    lambda_init = 0.8 - 0.6 * math.exp(-0.3 * l)
    cos, sin = _rope_tables(s)
    x_bf = x.astype(BF16)
    w_bf = w_in[l].astype(BF16)

    proj = _inproj(x_bf, w_bf, cos, sin, b_gate[l].reshape(1, 2 * d))
    lam_p = jnp.stack([lambda_q1[l], lambda_k1[l], lambda_q2[l], lambda_k2[l]]).astype(F32)
    y_diff = _diff_attention(proj, lam_p, diff_norm_g[l].reshape(1, -1), lambda_init)

    dil = [_dilated_group(_inproj_dil(x_bf, w_bf, cos, sin, g, dilation), g)
           for g, (_, dilation) in enumerate(DIL_CONFIGS)]
    h, logits_t = _merge(
        [o for o, _ in dil], [ls for _, ls in dil], y_diff, proj, x,
        w_branch_diff[l].astype(BF16), w_branch_dil[l].astype(BF16), w_out[l].astype(BF16),
        router_w[l].T.astype(BF16), ln1_g[l].reshape(1, d), ln1_b[l].reshape(1, d), alpha)

    eidx, wts, rank, counts = _route(logits_t, router_bias[l].reshape(N_EXPERTS, 1).astype(F32))
    counts = counts.reshape(N_EXPERTS)
    n_chunks = (counts + (EXP_CHUNK - 1)) // EXP_CHUNK
    chunk0 = jnp.cumsum(n_chunks) - n_chunks
    pstart = chunk0 * EXP_CHUNK
    max_chunks = s * TOP_K // EXP_CHUNK + N_EXPERTS
    cid = jnp.arange(max_chunks)
    chunk_end = chunk0 + n_chunks
    chunk_expert = jnp.minimum(jnp.sum(chunk_end[None, :] <= cid[:, None], axis=1), N_EXPERTS - 1)
    chunk_valid = jnp.clip(counts[chunk_expert] - (cid - chunk0[chunk_expert]) * EXP_CHUNK,
                           0, EXP_CHUNK).astype(I32)

    slots = _slots(eidx, rank, pstart.astype(F32).reshape(N_EXPERTS, 1))
    slots_flat = slots.T.reshape(-1)
    pstart = pstart.astype(I32)
    xs = _scatter_rows(pstart, counts, slots_flat, h, max_chunks * EXP_CHUNK)
    ys = _experts(pstart, counts, chunk_valid, xs, w_gate_e[l], w_up_e[l], w_down_e[l])
    wgu_s = jnp.concatenate([w_gate_s[l], w_up_s[l]], axis=1).astype(BF16)
    return _combine(slots_flat, ys, wts.T, h, wgu_s, w_down_s[l].astype(BF16),
                    ln2_g[l].reshape(1, d), ln2_b[l].reshape(1, d), alpha)


def kernel(x, w_in, b_gate, lambda_q1, lambda_k1, lambda_q2, lambda_k2, diff_norm_g, w_branch_diff,
           w_branch_dil, w_out, ln1_g, ln1_b, router_w, router_bias, w_gate_e, w_up_e, w_down_e,
           w_gate_s, w_up_s, w_down_s, ln2_g, ln2_b):
    b, s, d = x.shape
    depth = w_in.shape[0]
    outs = []
    for bi in range(b):
        xb = x[bi]
        for l in range(depth):
            xb = _layer(xb, l, depth, w_in, b_gate, lambda_q1, lambda_k1, lambda_q2, lambda_k2,
                        diff_norm_g, w_branch_diff, w_branch_dil, w_out, ln1_g, ln1_b, router_w,
                        router_bias, w_gate_e, w_up_e, w_down_e, w_gate_s, w_up_s, w_down_s,
                        ln2_g, ln2_b)
        outs.append(xb)
    return jnp.stack(outs)
```

```python
import functools
import math

import jax
import jax.numpy as jnp
import numpy as np
from jax import lax
from jax.experimental import pallas as pl
from jax.experimental.pallas import tpu as pltpu

F32 = jnp.float32
BF16 = jnp.bfloat16
I32 = jnp.int32

HEAD_DIM = 128
ROPE_THETA = 10000.0
LN_EPS = 1e-5
DIFF_HEADS = 4
DIL_CONFIGS = ((128, 1), (512, 4), (2048, 16))
DIL_HEADS_PER_GROUP = 4
DIL_OUT = DIL_HEADS_PER_GROUP * HEAD_DIM
N_EXPERTS = 256
TOP_K = 8
N_GROUPS = 8
TOPK_GROUPS = 4
GROUP_SIZE = N_EXPERTS // N_GROUPS
ROUTED_SCALE = 2.5
ATTN_SCALE = HEAD_DIM ** -0.5
LOG2E = math.log2(math.e)

PROJ_TILE = 512
N_QKV_TILES = 6
DIL_SRC_TILE0 = 6
GATE_SRC_TILE0 = 15
N_GATE_TILES = 8

INPROJ_TM = 1024
ATTN_TQ = 1024
ATTN_TK = 1024
LANES = 128
DIL_T = 128
MERGE_TM = 256
ROUTE_TM = 512
ROW_TM = 128
EXP_CHUNK = 128
W_RING = 3
X_SLOTS = 5
Y_SLOTS = 4
SUBLANES = 8
NEG = -1e30
VMEM_LIMIT = 56 * 1024 * 1024


def _cparams(sem, vmem=None):
    return pltpu.CompilerParams(dimension_semantics=sem, vmem_limit_bytes=vmem)


def _rope(a, cos, sin):
    outs = []
    for h in range(a.shape[1] // HEAD_DIM):
        ah = a[:, h * HEAD_DIM:(h + 1) * HEAD_DIM]
        outs.append(ah * cos + pltpu.roll(ah, HEAD_DIM // 2, 1) * sin)
    return jnp.concatenate(outs, axis=1)


def _gates_kernel(x_ref, w_ref, b_ref, o_ref):
    acc = jnp.dot(x_ref[...], w_ref[...], preferred_element_type=F32)
    o_ref[...] = jax.nn.sigmoid(acc + b_ref[...]).astype(BF16)


def _inproj_gates(x_bf, w_bf, b_gate):
    s, d = x_bf.shape
    tm = min(INPROJ_TM, s)
    return pl.pallas_call(
        _gates_kernel,
        out_shape=jax.ShapeDtypeStruct((s, N_GATE_TILES * PROJ_TILE), BF16),
        grid=(s // tm, N_GATE_TILES),
        in_specs=[
            pl.BlockSpec((tm, d), lambda i, j: (i, 0)),
            pl.BlockSpec((d, PROJ_TILE), lambda i, j: (0, GATE_SRC_TILE0 + j)),
            pl.BlockSpec((1, PROJ_TILE), lambda i, j: (0, j)),
        ],
        out_specs=pl.BlockSpec((tm, PROJ_TILE), lambda i, j: (i, j)),
        compiler_params=_cparams(("arbitrary", "arbitrary"), VMEM_LIMIT),
        name="inproj_gates",
    )(x_bf, w_bf, b_gate)


def _qkv_kernel(x_ref, w_ref, cos_ref, sin_ref, o_ref):
    j = pl.program_id(1)
    acc = jnp.dot(x_ref[...], w_ref[...], preferred_element_type=F32)
    scale = jnp.where(j < 2, ATTN_SCALE * LOG2E, 1.0).astype(F32)
    o_ref[...] = _rope(acc, cos_ref[...] * scale, sin_ref[...] * scale).astype(BF16)


def _inproj_qkv(x_bf, w_bf, cos_tab, sin_tab):
    s, d = x_bf.shape
    tm = min(INPROJ_TM, s)
    tab = pl.BlockSpec((None, tm, HEAD_DIM), lambda i, j: ((j >= 4).astype(I32), i, 0))
    return pl.pallas_call(
        _qkv_kernel,
        out_shape=jax.ShapeDtypeStruct((s, N_QKV_TILES * PROJ_TILE), BF16),
        grid=(s // tm, N_QKV_TILES),
        in_specs=[
            pl.BlockSpec((tm, d), lambda i, j: (i, 0)),
            pl.BlockSpec((d, PROJ_TILE), lambda i, j: (0, j)),
            tab, tab,
        ],
        out_specs=pl.BlockSpec((tm, PROJ_TILE), lambda i, j: (i, j)),
        compiler_params=_cparams(("arbitrary", "arbitrary"), VMEM_LIMIT),
        name="inproj_qkv",
    )(x_bf, w_bf, cos_tab, sin_tab)


def _inproj_dil_kernel(x_ref, w_ref, cos_ref, sin_ref, o_ref, scr, *, dilation):
    part = pl.program_id(1)
    acc = jnp.dot(x_ref[...], w_ref[...], preferred_element_type=F32)
    scale = jnp.where(part == 0, ATTN_SCALE, 1.0).astype(F32)
    val = _rope(acc, cos_ref[...] * scale, sin_ref[...] * scale)
    if dilation == 1:
        o_ref[0] = val.astype(BF16)
        return
    for c in range(scr.shape[0]):
        scr[c] = val[:, c * HEAD_DIM:(c + 1) * HEAD_DIM]
    n = scr.shape[1] // dilation
    for r in range(dilation):
        for c in range(scr.shape[0]):
            o_ref[r, :, c * HEAD_DIM:(c + 1) * HEAD_DIM] = (
                scr[c, pl.ds(r, n, stride=dilation), :].astype(BF16))


def _inproj_dil(x_bf, w_bf, cos_tab, sin_tab, g, dilation):
    s, d = x_bf.shape
    tm = min(INPROJ_TM, s)
    n = tm // dilation
    tab = pl.BlockSpec((None, tm, HEAD_DIM), lambda i, p: ((p == 2).astype(I32), i, 0))
    return pl.pallas_call(
        functools.partial(_inproj_dil_kernel, dilation=dilation),
        out_shape=jax.ShapeDtypeStruct((dilation, s // dilation, 3 * PROJ_TILE), BF16),
        grid=(s // tm, 3),
        in_specs=[
            pl.BlockSpec((tm, d), lambda i, p: (i, 0)),
            pl.BlockSpec((d, PROJ_TILE), lambda i, p: (0, DIL_SRC_TILE0 + g + 3 * p)),
            tab, tab,
        ],
        out_specs=pl.BlockSpec((dilation, n, PROJ_TILE), lambda i, p: (0, i, p)),
        scratch_shapes=[pltpu.VMEM((PROJ_TILE // HEAD_DIM, tm, HEAD_DIM), F32)],
        compiler_params=_cparams(("arbitrary", "arbitrary"), VMEM_LIMIT),
        name=f"inproj_dil{g}",
    )(x_bf, w_bf, cos_tab, sin_tab)


def _diff_kernel(qi_ref, kj_ref, q_ref, k_ref, v_ref, lam_ref, g_ref, o_ref, m_sc, l_sc, acc_sc,
                 *, lambda_init):
    step = pl.program_id(1)
    qi = qi_ref[step]
    kj = kj_ref[step]
    tq, tk = q_ref.shape[0], k_ref.shape[0]
    last_kj = qi // (tk // tq)

    @pl.when(kj == 0)
    def _():
        m_sc[...] = jnp.full(m_sc.shape, NEG, F32)
        l_sc[...] = jnp.zeros(l_sc.shape, F32)
        acc_sc[...] = jnp.zeros(acc_sc.shape, F32)

    def update(masked):
        v = v_ref[...]
        if masked:
            row = qi * tq + lax.broadcasted_iota(I32, (tq, tk), 0)
            col = kj * tk + lax.broadcasted_iota(I32, (tq, tk), 1)
            keep = col <= row
        for mm in range(2):
            q = q_ref[:, mm * HEAD_DIM:(mm + 1) * HEAD_DIM]
            k = k_ref[:, mm * HEAD_DIM:(mm + 1) * HEAD_DIM]
            s = lax.dot_general(q, k, (((1,), (1,)), ((), ())), preferred_element_type=F32)
            if masked:
                s = jnp.where(keep, s, NEG)
            m_prev = m_sc[mm]
            m_new = jnp.maximum(m_prev, jnp.max(s, axis=-1, keepdims=True))
            alpha = jnp.exp2(m_prev - m_new)
            p = jnp.exp2(s - jnp.tile(m_new, (1, tk // LANES)))
            l_sc[mm] = alpha * l_sc[mm] + jnp.sum(p, axis=-1, keepdims=True)
            pv = jnp.dot(p.astype(BF16), v, preferred_element_type=F32)
            acc_sc[mm] = jnp.tile(alpha, (1, v.shape[1] // LANES)) * acc_sc[mm] + pv
            m_sc[mm] = m_new

    @pl.when(kj < last_kj)
    def _():
        update(False)

    @pl.when(kj == last_kj)
    def _():
        update(True)
        lam_p = lam_ref[...]
        lam = (jnp.exp(jnp.sum(lam_p[0:1] * lam_p[1:2], axis=-1, keepdims=True))
               - jnp.exp(jnp.sum(lam_p[2:3] * lam_p[3:4], axis=-1, keepdims=True))
               + lambda_init)
        rep = acc_sc.shape[2] // LANES
        o = (acc_sc[0] / jnp.tile(l_sc[0], (1, rep))
             - lam * (acc_sc[1] / jnp.tile(l_sc[1], (1, rep))))
        o = o * lax.rsqrt(jnp.mean(o * o, axis=-1, keepdims=True) + LN_EPS) * g_ref[...]
        o_ref[...] = (o * (1.0 - lambda_init)).astype(o_ref.dtype)


def _diff_attention(proj, lam_p, norm_g, lambda_init):
    s = proj.shape[0]
    tq = min(ATTN_TQ, s)
    tk = min(ATTN_TK, s)
    pairs = [(i, j) for i in range(s // tq) for j in range(i * tq // tk + 1)]
    qi_tab = np.asarray([p[0] for p in pairs], np.int32)
    kj_tab = np.asarray([p[1] for p in pairs], np.int32)
    vw = 2 * HEAD_DIM
    grid_spec = pltpu.PrefetchScalarGridSpec(
        num_scalar_prefetch=2,
        grid=(DIFF_HEADS, len(pairs)),
        in_specs=[
            pl.BlockSpec((tq, vw), lambda h, st, qi, kj: (qi[st], h)),
            pl.BlockSpec((tk, vw), lambda h, st, qi, kj: (kj[st], DIFF_HEADS + h)),
            pl.BlockSpec((tk, vw), lambda h, st, qi, kj: (kj[st], 2 * DIFF_HEADS + h)),
            pl.BlockSpec((4, HEAD_DIM), lambda h, st, qi, kj: (0, 0)),
            pl.BlockSpec((1, vw), lambda h, st, qi, kj: (0, 0)),
        ],
        out_specs=pl.BlockSpec((tq, vw), lambda h, st, qi, kj: (qi[st], h)),
        scratch_shapes=[
            pltpu.VMEM((2, tq, LANES), F32),
            pltpu.VMEM((2, tq, LANES), F32),
            pltpu.VMEM((2, tq, vw), F32),
        ],
    )
    return pl.pallas_call(
        functools.partial(_diff_kernel, lambda_init=lambda_init),
        out_shape=jax.ShapeDtypeStruct((s, DIFF_HEADS * vw), BF16),
        grid_spec=grid_spec,
        compiler_params=_cparams(("arbitrary", "arbitrary"), VMEM_LIMIT),
        name="diffattn",
    )(jnp.asarray(qi_tab), jnp.asarray(kj_tab), proj, proj, proj, lam_p, norm_g)


def _dil_kernel(q_ref, kp_ref, kc_ref, vp_ref, vc_ref, o_ref, lse_ref):
    n = pl.program_id(1)
    t = q_ref.shape[0]
    qi = lax.broadcasted_iota(I32, (t, t), 0)
    kj = lax.broadcasted_iota(I32, (t, t), 1)
    keep_prev = (kj >= qi) & (n > 0)
    keep_cur = kj <= qi
    for h in range(DIL_HEADS_PER_GROUP):
        sl = slice(h * HEAD_DIM, (h + 1) * HEAD_DIM)
        q = q_ref[:, sl]
        dn = (((1,), (1,)), ((), ()))
        sp = lax.dot_general(q, kp_ref[:, sl], dn, preferred_element_type=F32)
        sc = lax.dot_general(q, kc_ref[:, sl], dn, preferred_element_type=F32)
        sp = jnp.where(keep_prev, sp, NEG)
        sc = jnp.where(keep_cur, sc, NEG)
        m = jnp.maximum(jnp.max(sp, axis=-1, keepdims=True), jnp.max(sc, axis=-1, keepdims=True))
        ep = jnp.exp(sp - m)
        ec = jnp.exp(sc - m)
        den = jnp.sum(ep, axis=-1, keepdims=True) + jnp.sum(ec, axis=-1, keepdims=True)
        acc = (jnp.dot(ep.astype(BF16), vp_ref[:, sl], preferred_element_type=F32)
               + jnp.dot(ec.astype(BF16), vc_ref[:, sl], preferred_element_type=F32))
        o_ref[:, sl] = acc / den
        lse_ref[:, sl] = jnp.broadcast_to(m + jnp.log(den), (t, HEAD_DIM))


def _dilated_group(qkv, g):
    dilation, l, _ = qkv.shape
    t = DIL_T
    cur = lambda part: (lambda r, n: (r, n, part))
    prev = lambda part: (lambda r, n: (r, jnp.maximum(n - 1, 0), part))
    blk = (None, t, PROJ_TILE)
    return pl.pallas_call(
        _dil_kernel,
        out_shape=[jax.ShapeDtypeStruct((dilation, l, DIL_OUT), F32)] * 2,
        grid=(dilation, l // t),
        in_specs=[
            pl.BlockSpec(blk, cur(0)),
            pl.BlockSpec(blk, prev(1)),
            pl.BlockSpec(blk, cur(1)),
            pl.BlockSpec(blk, prev(2)),
            pl.BlockSpec(blk, cur(2)),
        ],
        out_specs=[pl.BlockSpec((None, t, DIL_OUT), lambda r, n: (r, n, 0))] * 2,
        compiler_params=_cparams(("arbitrary", "arbitrary")),
        name=f"dilattn{g}",
    )(qkv, qkv, qkv, qkv, qkv)


def _layer_norm(z, g, b):
    mu = jnp.mean(z, axis=-1, keepdims=True)
    zc = z - mu
    var = jnp.mean(zc * zc, axis=-1, keepdims=True)
    return zc * lax.rsqrt(var + LN_EPS) * g + b


def _merge_kernel(o0, o1, o2, l0, l1, l2, yd_ref, ga_ref, gb_ref, x_ref, wbd_ref, wbl_ref, wo_ref,
                  rwt_ref, g_ref, b_ref, h_ref, lt_ref, scr, *, alpha):
    def positions(ref, slot):
        d = ref.shape[0]
        if d == 1:
            return ref[0]
        heads = scr.shape[1]
        for r in range(d):
            for c in range(heads):
                scr[slot, c, pl.ds(r, ref.shape[1], stride=d), :] = (
                    ref[r, :, c * HEAD_DIM:(c + 1) * HEAD_DIM])
        return jnp.concatenate([scr[slot, c] for c in range(heads)], axis=1)

    la, lb, lc = positions(l0, 0), positions(l1, 0), positions(l2, 1)
    m = jnp.maximum(jnp.maximum(la, lb), lc)
    ea, eb, ec = jnp.exp(la - m), jnp.exp(lb - m), jnp.exp(lc - m)
    num = ea * positions(o0, 0) + eb * positions(o1, 2) + ec * positions(o2, 3)
    ydil = num / (ea + eb + ec)
    a = jnp.dot(yd_ref[...], wbd_ref[...], preferred_element_type=F32)
    b = jnp.dot(ydil.astype(BF16), wbl_ref[...], preferred_element_type=F32)
    merged = ga_ref[...].astype(F32) * a + gb_ref[...].astype(F32) * b
    z = alpha * x_ref[...] + jnp.dot(merged.astype(BF16), wo_ref[...], preferred_element_type=F32)
    h = _layer_norm(z, g_ref[...], b_ref[...])
    h_ref[...] = h
    lt_ref[...] = lax.dot_general(rwt_ref[...], h.astype(BF16), (((1,), (1,)), ((), ())),
                                  preferred_element_type=F32)


def _merge(dil_o, dil_l, y_diff, gates, x, wbd, wbl, wo, rwt, ln_g, ln_b, alpha):
    s, d = x.shape
    tm = min(MERGE_TM, s)
    row = lambda w: pl.BlockSpec((tm, w), lambda i: (i, 0))
    const = lambda a: pl.BlockSpec(a.shape, lambda i: (0, 0), pipeline_mode=pl.Buffered(1))
    res = lambda a: pl.BlockSpec((a.shape[0], tm // a.shape[0], DIL_OUT), lambda i: (0, i, 0))
    in_specs = ([res(a) for a in dil_o] + [res(a) for a in dil_l] + [
        row(y_diff.shape[1]),
        pl.BlockSpec((tm, d), lambda i: (i, 0)),
        pl.BlockSpec((tm, d), lambda i: (i, 1)),
        row(d), const(wbd), const(wbl), const(wo), const(rwt), const(ln_g), const(ln_b)])
    return pl.pallas_call(
        functools.partial(_merge_kernel, alpha=alpha),
        out_shape=[jax.ShapeDtypeStruct((s, d), F32),
                   jax.ShapeDtypeStruct((N_EXPERTS, s), F32)],
        grid=(s // tm,),
        in_specs=in_specs,
        out_specs=[row(d), pl.BlockSpec((N_EXPERTS, tm), lambda i: (0, i))],
        scratch_shapes=[pltpu.VMEM((4, DIL_HEADS_PER_GROUP, tm, HEAD_DIM), F32)],
        compiler_params=_cparams(("arbitrary",), VMEM_LIMIT),
        name="merge",
    )(*dil_o, *dil_l, y_diff, gates, gates, x, wbd, wbl, wo, rwt, ln_g, ln_b)


def _route_kernel(lt_ref, bias_ref, e_ref, w_ref, r_ref, cnt_ref, carry):
    i = pl.program_id(0)
    tm = lt_ref.shape[1]
    ninf = -jnp.inf

    @pl.when(i == 0)
    def _():
        carry[...] = jnp.zeros(carry.shape, F32)

    sc = jax.nn.sigmoid(lt_ref[...])
    biased = sc + bias_ref[...]
    iog = lax.broadcasted_iota(I32, (GROUP_SIZE, tm), 0)
    blocks, gscore = [], []
    for g in range(N_GROUPS):
        blk = biased[g * GROUP_SIZE:(g + 1) * GROUP_SIZE, :]
        m1 = jnp.max(blk, axis=0, keepdims=True)
        i1 = jnp.min(jnp.where(blk == m1, iog, GROUP_SIZE), axis=0, keepdims=True)
        m2 = jnp.max(jnp.where(iog == i1, ninf, blk), axis=0, keepdims=True)
        blocks.append(blk)
        gscore.append(m1 + m2)
    masked = []
    for g in range(N_GROUPS):
        ahead = jnp.zeros((1, tm), F32)
        for o in range(N_GROUPS):
            if o == g:
                continue
            wins = (gscore[o] >= gscore[g]) if o < g else (gscore[o] > gscore[g])
            ahead = ahead + jnp.where(wins, 1.0, 0.0)
        keep = jnp.broadcast_to(ahead, (GROUP_SIZE, tm)) < TOPK_GROUPS
        masked.append(jnp.where(keep, blocks[g], ninf))
    v = jnp.concatenate(masked, axis=0)
    ioe = lax.broadcasted_iota(I32, (N_EXPERTS, tm), 0)
    idxs, ws = [], []
    sel = jnp.zeros((N_EXPERTS, tm), jnp.bool_)
    for _ in range(TOP_K):
        mx = jnp.max(v, axis=0, keepdims=True)
        idx = jnp.min(jnp.where(v == mx, ioe, N_EXPERTS), axis=0, keepdims=True)
        hit = ioe == idx
        idxs.append(idx)
        ws.append(jnp.sum(jnp.where(hit, sc, 0.0), axis=0, keepdims=True))
        sel = sel | hit
        v = jnp.where(hit, ninf, v)
    wsum = ws[0]
    for wk in ws[1:]:
        wsum = wsum + wk

    self = jnp.where(sel, 1.0, 0.0)
    ta = lax.broadcasted_iota(I32, (tm, tm), 0)
    tb = lax.broadcasted_iota(I32, (tm, tm), 1)
    before = jnp.where(ta < tb, 1.0, 0.0).astype(BF16)
    pos = jnp.dot(self.astype(BF16), before, preferred_element_type=F32) + carry[...]
    for k in range(TOP_K):
        e_ref[k:k + 1, :] = idxs[k]
        w_ref[k:k + 1, :] = ws[k] / wsum * ROUTED_SCALE
        rk = jnp.sum(jnp.where(ioe == idxs[k], pos, 0.0), axis=0, keepdims=True)
        r_ref[k:k + 1, :] = rk.astype(I32)
    total = carry[...] + jnp.sum(self, axis=1, keepdims=True)
    carry[...] = total
    cnt_ref[...] = total.astype(I32)


def _route(logits_t, bias_col):
    e, s = logits_t.shape
    tm = min(ROUTE_TM, s)
    tok = pl.BlockSpec((TOP_K, tm), lambda i: (0, i))
    return pl.pallas_call(
        _route_kernel,
        out_shape=[jax.ShapeDtypeStruct((TOP_K, s), I32),
                   jax.ShapeDtypeStruct((TOP_K, s), F32),
                   jax.ShapeDtypeStruct((TOP_K, s), I32),
                   jax.ShapeDtypeStruct((e, 1), I32)],
        grid=(s // tm,),
        in_specs=[pl.BlockSpec((e, tm), lambda i: (0, i)),
                  pl.BlockSpec((e, 1), lambda i: (0, 0))],
        out_specs=[tok, tok, tok, pl.BlockSpec((e, 1), lambda i: (0, 0))],
        scratch_shapes=[pltpu.VMEM((e, 1), F32)],
        compiler_params=_cparams(("arbitrary",)),
        name="route",
    )(logits_t, bias_col)


def _slots_kernel(e_ref, r_ref, ps_ref, s_ref):
    tm = e_ref.shape[1]
    ioe = lax.broadcasted_iota(I32, (N_EXPERTS, tm), 0)
    ps = ps_ref[...]
    for k in range(TOP_K):
        start = jnp.sum(jnp.where(ioe == e_ref[k:k + 1, :], ps, 0.0), axis=0, keepdims=True)
        s_ref[k:k + 1, :] = start.astype(I32) + r_ref[k:k + 1, :]


def _slots(eidx, rank, pstart_col):
    s = eidx.shape[1]
    tm = min(ROUTE_TM, s)
    tok = pl.BlockSpec((TOP_K, tm), lambda i: (0, i))
    return pl.pallas_call(
        _slots_kernel,
        out_shape=jax.ShapeDtypeStruct((TOP_K, s), I32),
        grid=(s // tm,),
        in_specs=[tok, tok, pl.BlockSpec((N_EXPERTS, 1), lambda i: (0, 0))],
        out_specs=tok,
        compiler_params=_cparams(("arbitrary",)),
        name="slots",
    )(eidx, rank, pstart_col)


def _scatter_kernel(pstart_ref, cnt_ref, slot_ref, h_ref, xs_ref, zbuf, sem, zsem, *, n_steps):
    i = pl.program_id(0)
    tm = h_ref.shape[0]
    n_exp = cnt_ref.shape[0]

    def zero_fill(do):
        def piece(row, size):
            do(pltpu.make_async_copy(zbuf.at[pl.ds(0, size)], xs_ref.at[pl.ds(row, size)], zsem))

        def expert(ex, carry):
            cnt = cnt_ref[ex]
            row = pstart_ref[ex] + cnt
            pad = (-cnt) & (EXP_CHUNK - 1)
            for j in range(SUBLANES - 1):
                @pl.when(j < (pad & (SUBLANES - 1)))
                def _(row=row, j=j):
                    piece(row + j, 1)
            row = row + (pad & (SUBLANES - 1))
            size = SUBLANES
            while size < EXP_CHUNK:
                @pl.when((pad & size) != 0)
                def _(row=row, size=size):
                    piece(pl.multiple_of(row, size), size)
                row = row + (pad & size)
                size *= 2
            return carry

        per_step = pl.cdiv(n_exp, n_steps)
        lax.fori_loop(i * per_step, jnp.minimum((i + 1) * per_step, n_exp), expert, 0)
        end = pstart_ref[n_exp - 1] + cnt_ref[n_exp - 1]
        first = (end + (EXP_CHUNK - 1)) // EXP_CHUNK
        n_all = xs_ref.shape[0] // EXP_CHUNK
        max_tail = n_all - n_steps * tm * TOP_K // EXP_CHUNK
        tail_per_step = pl.cdiv(max_tail, n_steps)
        lo = first + i * tail_per_step
        lax.fori_loop(lo, jnp.minimum(lo + tail_per_step, n_all),
                      lambda g, c: (piece(pl.multiple_of(g * EXP_CHUNK, EXP_CHUNK), EXP_CHUNK), c)[1],
                      0)

    @pl.when(i == 0)
    def _():
        zbuf[...] = jnp.zeros(zbuf.shape, F32)

    zero_fill(lambda cp: cp.start())

    def issue(r, carry):
        for k in range(TOP_K):
            dst = slot_ref[r * TOP_K + k]
            pltpu.make_async_copy(h_ref.at[pl.ds(r, 1)], xs_ref.at[pl.ds(dst, 1)],
                                  sem).start(priority=k % 2)
        return carry

    lax.fori_loop(0, tm, issue, 0)
    for k in range(TOP_K):
        pltpu.make_async_copy(h_ref, xs_ref.at[pl.ds(0, tm)], sem).wait()

    zero_fill(lambda cp: cp.wait())


def _scatter_rows(pstart, counts, slots_flat, h, n_rows):
    s, d = h.shape
    tm = min(ROW_TM, s)
    grid_spec = pltpu.PrefetchScalarGridSpec(
        num_scalar_prefetch=2,
        grid=(s // tm,),
        in_specs=[pl.BlockSpec((tm * TOP_K,), lambda i, ps, cn: (i,), memory_space=pltpu.SMEM),
                  pl.BlockSpec((tm, d), lambda i, ps, cn: (i, 0))],
        out_specs=pl.BlockSpec(memory_space=pl.ANY),
        scratch_shapes=[pltpu.VMEM((EXP_CHUNK, d), F32), pltpu.SemaphoreType.DMA,
                        pltpu.SemaphoreType.DMA],
    )
    return pl.pallas_call(
        functools.partial(_scatter_kernel, n_steps=s // tm),
        out_shape=jax.ShapeDtypeStruct((n_rows, d), F32),
        grid_spec=grid_spec,
        compiler_params=_cparams(("arbitrary",)),
        name="scatter",
    )(pstart, counts, slots_flat, h)


def _experts_kernel(pstart_ref, cnt_ref, valid_ref, xs_ref, wg_ref, wu_ref, wd_ref, ys_ref,
                    wg_buf, wu_buf, wd_buf, wgu_sc, wd_sc, xbuf, ybuf, w_sem, in_sem, out_sem):
    e = pl.program_id(0)
    n_exp = pl.num_programs(0)
    last = n_exp - 1
    ff = wg_ref.shape[2]
    chunks_of = lambda ex: (cnt_ref[ex] + (EXP_CHUNK - 1)) // EXP_CHUNK

    def weight_copies(ex):
        ws = ex % W_RING
        half = wd_ref.shape[1] // 2
        lo, hi = pl.ds(0, half), pl.ds(half, half)
        parts = ((wg_ref.at[ex], wg_buf.at[ws]), (wu_ref.at[ex], wu_buf.at[ws]),
                 (wd_ref.at[ex, lo], wd_buf.at[ws, lo]), (wd_ref.at[ex, hi], wd_buf.at[ws, hi]))
        return [(pltpu.make_async_copy(src, dst, w_sem.at[i, ws]), 0)
                for i, (src, dst) in enumerate(parts)]

    @pl.when(e == 0)
    def _():
        for ex in range(W_RING - 1):
            for cp, queue in weight_copies(ex):
                cp.start(priority=queue)

    @pl.when(e + (W_RING - 1) < n_exp)
    def _():
        for cp, queue in weight_copies(e + (W_RING - 1)):
            cp.start(priority=queue)

    for cp, _ in weight_copies(e):
        cp.wait()
    wslot = e % W_RING
    g0 = pstart_ref[e] // EXP_CHUNK
    nch = chunks_of(e)
    total = pstart_ref[last] // EXP_CHUNK + chunks_of(last)

    def fetch(g, slot, do):
        valid = valid_ref[g]
        row0 = pl.multiple_of(g * EXP_CHUNK, EXP_CHUNK)

        @pl.when(valid == EXP_CHUNK)
        def _():
            do(pltpu.make_async_copy(xs_ref.at[pl.ds(row0, EXP_CHUNK)], xbuf.at[slot],
                                     in_sem.at[slot]))

        @pl.when(valid < EXP_CHUNK)
        def _():
            off = jnp.int32(0)
            size = EXP_CHUNK // 2
            while size >= SUBLANES:
                @pl.when((valid & size) != 0)
                def _(off=off, size=size):
                    o = pl.multiple_of(off, size)
                    do(pltpu.make_async_copy(xs_ref.at[pl.ds(row0 + o, size)],
                                             xbuf.at[slot, pl.ds(o, size)], in_sem.at[slot]))
                off = off + (valid & size)
                size //= 2
            for j in range(SUBLANES - 1):
                @pl.when(j < (valid & (SUBLANES - 1)))
                def _(off=off, j=j):
                    do(pltpu.make_async_copy(xs_ref.at[pl.ds(row0 + off + j, 1)],
                                             xbuf.at[slot, pl.ds(off + j, 1)], in_sem.at[slot]))

    def out_copy(g, slot):
        rows = pl.ds(pl.multiple_of(g * EXP_CHUNK, EXP_CHUNK), EXP_CHUNK)
        return pltpu.make_async_copy(ybuf.at[slot], ys_ref.at[rows], out_sem.at[slot])

    def start(cp):
        cp.start(priority=1)

    n_x, n_y = xbuf.shape[0], ybuf.shape[0]
    ahead = n_x - 1

    @pl.when(e == 0)
    def _():
        xbuf[...] = jnp.zeros(xbuf.shape, F32)
        for j in range(ahead):
            @pl.when(j < total)
            def _(j=j):
                fetch(j, j, start)

    @pl.when(nch > 0)
    def _():
        wgu_sc[:, :ff] = wg_buf[wslot].astype(BF16)
        wgu_sc[:, ff:] = wu_buf[wslot].astype(BF16)
        wd_sc[...] = wd_buf[wslot].astype(BF16)

        def chunk(c, carry):
            g = g0 + c
            slot = g % n_x
            fetch(g, slot, lambda cp: cp.wait())

            @pl.when(g + ahead < total)
            def _():
                fetch(g + ahead, (g + ahead) % n_x, start)

            gu = jnp.dot(xbuf[slot].astype(BF16), wgu_sc[...], preferred_element_type=F32)
            hmid = jax.nn.silu(gu[:, :ff]) * gu[:, ff:]
            y = jnp.dot(hmid.astype(BF16), wd_sc[...], preferred_element_type=F32)

            yslot = g % n_y

            @pl.when(g >= n_y)
            def _():
                out_copy(g - n_y, yslot).wait()

            ybuf[yslot] = y
            start(out_copy(g, yslot))
            return carry

        lax.fori_loop(0, nch, chunk, 0)

    @pl.when(e == last)
    def _():
        for j in range(n_y):
            @pl.when(total > j)
            def _(j=j):
                out_copy(total - 1 - j, (total - 1 - j) % n_y).wait()

        n_all = ys_ref.shape[0] // EXP_CHUNK
        ybuf[0] = jnp.zeros(ybuf.shape[1:], F32)
        lax.fori_loop(total, n_all, lambda g, c: (out_copy(g, 0).start(), c)[1], 0)
        lax.fori_loop(total, n_all, lambda g, c: (out_copy(g, 0).wait(), c)[1], 0)


def _experts(pstart, counts, chunk_valid, xs, wg, wu, wd):
    n_rows, d = xs.shape
    ne, _, ff = wg.shape
    grid_spec = pltpu.PrefetchScalarGridSpec(
        num_scalar_prefetch=3,
        grid=(ne,),
        in_specs=[
            pl.BlockSpec(memory_space=pl.ANY),
            pl.BlockSpec(memory_space=pl.ANY),
            pl.BlockSpec(memory_space=pl.ANY),
            pl.BlockSpec(memory_space=pl.ANY),
        ],
        out_specs=pl.BlockSpec(memory_space=pl.ANY),
        scratch_shapes=[
            pltpu.VMEM((W_RING, d, ff), F32),
            pltpu.VMEM((W_RING, d, ff), F32),
            pltpu.VMEM((W_RING, ff, d), F32),
            pltpu.VMEM((d, 2 * ff), BF16),
            pltpu.VMEM((ff, d), BF16),
            pltpu.VMEM((X_SLOTS, EXP_CHUNK, d), F32),
            pltpu.VMEM((Y_SLOTS, EXP_CHUNK, d), F32),
            pltpu.SemaphoreType.DMA((4, W_RING)),
            pltpu.SemaphoreType.DMA((X_SLOTS,)),
            pltpu.SemaphoreType.DMA((Y_SLOTS,)),
        ],
    )
    return pl.pallas_call(
        _experts_kernel,
        out_shape=jax.ShapeDtypeStruct((n_rows, d), F32),
        grid_spec=grid_spec,
        compiler_params=_cparams(("arbitrary",), VMEM_LIMIT),
        name="experts",
    )(pstart, counts, chunk_valid, xs, wg, wu, wd)


def _combine_kernel(slot_ref, ys_ref, wt_ref, h_ref, wgu_ref, wd_ref, g_ref, b_ref, o_ref,
                    gbuf, sem, *, alpha):
    tm = h_ref.shape[0]
    ff = wd_ref.shape[0]

    def issue(r, carry):
        for k in range(TOP_K):
            src = slot_ref[r * TOP_K + k]
            pltpu.make_async_copy(ys_ref.at[pl.ds(src, 1)], gbuf.at[k, pl.ds(r, 1)],
                                  sem).start(priority=k % 2)
        return carry

    lax.fori_loop(0, tm, issue, 0)

    h = h_ref[...]
    gu = jnp.dot(h.astype(BF16), wgu_ref[...], preferred_element_type=F32)
    hmid = jax.nn.silu(gu[:, :ff]) * gu[:, ff:]
    acc = jnp.dot(hmid.astype(BF16), wd_ref[...], preferred_element_type=F32)

    for k in range(TOP_K):
        pltpu.make_async_copy(ys_ref.at[pl.ds(0, tm)], gbuf.at[k], sem).wait()
    wt = wt_ref[...]
    for k in range(TOP_K):
        acc = acc + wt[:, k:k + 1] * gbuf[k]
    o_ref[...] = _layer_norm(alpha * h + acc, g_ref[...], b_ref[...])


def _combine(slots_flat, ys, wt, h, wgu_s, wd_s, ln_g, ln_b, alpha):
    s, d = h.shape
    tm = min(ROW_TM, s)
    row = lambda c: pl.BlockSpec((tm, c), lambda i: (i, 0))
    const = lambda a: pl.BlockSpec(a.shape, lambda i: (0, 0), pipeline_mode=pl.Buffered(1))
    return pl.pallas_call(
        functools.partial(_combine_kernel, alpha=alpha),
        out_shape=jax.ShapeDtypeStruct((s, d), F32),
        grid=(s // tm,),
        in_specs=[pl.BlockSpec((tm * TOP_K,), lambda i: (i,), memory_space=pltpu.SMEM),
                  pl.BlockSpec(memory_space=pl.ANY),
                  row(TOP_K), row(d), const(wgu_s), const(wd_s), const(ln_g), const(ln_b)],
        out_specs=row(d),
        scratch_shapes=[pltpu.VMEM((TOP_K, tm, d), F32), pltpu.SemaphoreType.DMA],
        compiler_params=_cparams(("arbitrary",), VMEM_LIMIT),
        name="combine",
    )(slots_flat, ys, wt, h, wgu_s, wd_s, ln_g, ln_b)


def _rope_tables(s):
    half = HEAD_DIM // 2
    inv = ROPE_THETA ** (-jnp.arange(half, dtype=F32) / half)
    ang = jnp.arange(s).astype(F32)[:, None] * inv[None, :]
    cos, sin = jnp.cos(ang), jnp.sin(ang)
    cos2, sin2 = jnp.concatenate([cos, cos], axis=1), jnp.concatenate([-sin, sin], axis=1)
    return jnp.stack([cos2, jnp.ones_like(cos2)]), jnp.stack([sin2, jnp.zeros_like(sin2)])


def _layer(x, l, depth, w_in, b_gate, lambda_q1, lambda_k1, lambda_q2, lambda_k2, diff_norm_g,
           w_branch_diff, w_branch_dil, w_out, ln1_g, ln1_b, router_w, router_bias,
           w_gate_e, w_up_e, w_down_e, w_gate_s, w_up_s, w_down_s, ln2_g, ln2_b):
    s, d = x.shape
    alpha = (2.0 * depth) ** 0.25
    lambda_init = 0.8 - 0.6 * math.exp(-0.3 * l)
    cos, sin = _rope_tables(s)
    x_bf = x.astype(BF16)
    w_bf = w_in[l].astype(BF16)

    gates = _inproj_gates(x_bf, w_bf, b_gate[l].reshape(1, 2 * d))
    qkv = _inproj_qkv(x_bf, w_bf, cos, sin)
    lam_p = jnp.stack([lambda_q1[l], lambda_k1[l], lambda_q2[l], lambda_k2[l]]).astype(F32)
    y_diff = _diff_attention(qkv, lam_p, diff_norm_g[l].reshape(1, -1), lambda_init)

    dil = [_dilated_group(_inproj_dil(x_bf, w_bf, cos, sin, g, dilation), g)
           for g, (_, dilation) in enumerate(DIL_CONFIGS)]
    h, logits_t = _merge(
        [o for o, _ in dil], [ls for _, ls in dil], y_diff, gates, x,
        w_branch_diff[l].astype(BF16), w_branch_dil[l].astype(BF16), w_out[l].astype(BF16),
        router_w[l].T.astype(BF16), ln1_g[l].reshape(1, d), ln1_b[l].reshape(1, d), alpha)

    eidx, wts, rank, counts = _route(logits_t, router_bias[l].reshape(N_EXPERTS, 1).astype(F32))
    counts = counts.reshape(N_EXPERTS)
    n_chunks = (counts + (EXP_CHUNK - 1)) // EXP_CHUNK
    chunk0 = jnp.cumsum(n_chunks) - n_chunks
    pstart = chunk0 * EXP_CHUNK
    max_chunks = s * TOP_K // EXP_CHUNK + N_EXPERTS
    cid = jnp.arange(max_chunks)
    chunk_end = chunk0 + n_chunks
    chunk_expert = jnp.minimum(jnp.sum(chunk_end[None, :] <= cid[:, None], axis=1), N_EXPERTS - 1)
    chunk_valid = jnp.clip(counts[chunk_expert] - (cid - chunk0[chunk_expert]) * EXP_CHUNK,
                           0, EXP_CHUNK).astype(I32)

    slots = _slots(eidx, rank, pstart.astype(F32).reshape(N_EXPERTS, 1))
    slots_flat = slots.T.reshape(-1)
    pstart = pstart.astype(I32)
    xs = _scatter_rows(pstart, counts, slots_flat, h, max_chunks * EXP_CHUNK)
    ys = _experts(pstart, counts, chunk_valid, xs, w_gate_e[l], w_up_e[l], w_down_e[l])
    wgu_s = jnp.concatenate([w_gate_s[l], w_up_s[l]], axis=1).astype(BF16)
    return _combine(slots_flat, ys, wts.T, h, wgu_s, w_down_s[l].astype(BF16),
                    ln2_g[l].reshape(1, d), ln2_b[l].reshape(1, d), alpha)


def kernel(x, w_in, b_gate, lambda_q1, lambda_k1, lambda_q2, lambda_k2, diff_norm_g, w_branch_diff,
           w_branch_dil, w_out, ln1_g, ln1_b, router_w, router_bias, w_gate_e, w_up_e, w_down_e,
           w_gate_s, w_up_s, w_down_s, ln2_g, ln2_b):
    b, s, d = x.shape
    depth = w_in.shape[0]
    outs = []
    for bi in range(b):
        xb = x[bi]
        for l in range(depth):
            xb = _layer(xb, l, depth, w_in, b_gate, lambda_q1, lambda_k1, lambda_q2, lambda_k2,
                        diff_norm_g, w_branch_diff, w_branch_dil, w_out, ln1_g, ln1_b, router_w,
                        router_bias, w_gate_e, w_up_e, w_down_e, w_gate_s, w_up_s, w_down_s,
                        ln2_g, ln2_b)
        outs.append(xb)
    return jnp.stack(outs)
```

```python
import functools
import math

import jax
import jax.numpy as jnp
import numpy as np
from jax import lax
from jax.experimental import pallas as pl
from jax.experimental.pallas import tpu as pltpu

F32 = jnp.float32
BF16 = jnp.bfloat16
I32 = jnp.int32

HEAD_DIM = 128
ROPE_THETA = 10000.0
LN_EPS = 1e-5
DIFF_HEADS = 4
DIL_CONFIGS = ((128, 1), (512, 4), (2048, 16))
DIL_HEADS_PER_GROUP = 4
DIL_OUT = DIL_HEADS_PER_GROUP * HEAD_DIM
N_EXPERTS = 256
TOP_K = 8
N_GROUPS = 8
TOPK_GROUPS = 4
GROUP_SIZE = N_EXPERTS // N_GROUPS
ROUTED_SCALE = 2.5
ATTN_SCALE = HEAD_DIM ** -0.5
LOG2E = math.log2(math.e)

PROJ_TILE = 512
N_QKV_TILES = 6
DIL_SRC_TILE0 = 6
GATE_SRC_TILE0 = 15
N_GATE_TILES = 8

INPROJ_TM = 1024
ATTN_TQ = 1024
ATTN_TK = 1024
LANES = 128
DIL_T = 128
MERGE_TM = 256
ROUTE_TM = 512
ROW_TM = 128
EXP_CHUNK = 128
W_RING = 3
X_SLOTS = 5
Y_SLOTS = 4
SUBLANES = 8
NEG = -1e30
VMEM_LIMIT = 56 * 1024 * 1024


def _cparams(sem, vmem=None):
    return pltpu.CompilerParams(dimension_semantics=sem, vmem_limit_bytes=vmem)


def _rope(a, cos, sin):
    outs = []
    for h in range(a.shape[1] // HEAD_DIM):
        ah = a[:, h * HEAD_DIM:(h + 1) * HEAD_DIM]
        outs.append(ah * cos + pltpu.roll(ah, HEAD_DIM // 2, 1) * sin)
    return jnp.concatenate(outs, axis=1)


def _gates_kernel(x_ref, w_ref, b_ref, o_ref):
    acc = jnp.dot(x_ref[...], w_ref[...], preferred_element_type=F32)
    o_ref[...] = jax.nn.sigmoid(acc + b_ref[...]).astype(BF16)


def _inproj_gates(x_bf, w_bf, b_gate):
    s, d = x_bf.shape
    tm = min(INPROJ_TM, s)
    return pl.pallas_call(
        _gates_kernel,
        out_shape=jax.ShapeDtypeStruct((s, N_GATE_TILES * PROJ_TILE), BF16),
        grid=(s // tm, N_GATE_TILES),
        in_specs=[
            pl.BlockSpec((tm, d), lambda i, j: (i, 0)),
            pl.BlockSpec((d, PROJ_TILE), lambda i, j: (0, GATE_SRC_TILE0 + j)),
            pl.BlockSpec((1, PROJ_TILE), lambda i, j: (0, j)),
        ],
        out_specs=pl.BlockSpec((tm, PROJ_TILE), lambda i, j: (i, j)),
        compiler_params=_cparams(("arbitrary", "arbitrary"), VMEM_LIMIT),
        name="inproj_gates",
    )(x_bf, w_bf, b_gate)


def _qkv_kernel(x_ref, w_ref, cos_ref, sin_ref, o_ref):
    j = pl.program_id(1)
    acc = jnp.dot(x_ref[...], w_ref[...], preferred_element_type=F32)
    scale = jnp.where(j < 2, ATTN_SCALE * LOG2E, 1.0).astype(F32)
    o_ref[...] = _rope(acc, cos_ref[...] * scale, sin_ref[...] * scale).astype(BF16)


def _inproj_qkv(x_bf, w_bf, cos_tab, sin_tab):
    s, d = x_bf.shape
    tm = min(INPROJ_TM, s)
    tab = pl.BlockSpec((None, tm, HEAD_DIM), lambda i, j: ((j >= 4).astype(I32), i, 0))
    return pl.pallas_call(
        _qkv_kernel,
        out_shape=jax.ShapeDtypeStruct((s, N_QKV_TILES * PROJ_TILE), BF16),
        grid=(s // tm, N_QKV_TILES),
        in_specs=[
            pl.BlockSpec((tm, d), lambda i, j: (i, 0)),
            pl.BlockSpec((d, PROJ_TILE), lambda i, j: (0, j)),
            tab, tab,
        ],
        out_specs=pl.BlockSpec((tm, PROJ_TILE), lambda i, j: (i, j)),
        compiler_params=_cparams(("arbitrary", "arbitrary"), VMEM_LIMIT),
        name="inproj_qkv",
    )(x_bf, w_bf, cos_tab, sin_tab)


def _inproj_dil_kernel(x_ref, w_ref, cos_ref, sin_ref, o_ref, scr, *, dilation):
    part = pl.program_id(1)
    acc = jnp.dot(x_ref[...], w_ref[...], preferred_element_type=F32)
    scale = jnp.where(part == 0, ATTN_SCALE, 1.0).astype(F32)
    val = _rope(acc, cos_ref[...] * scale, sin_ref[...] * scale)
    if dilation == 1:
        o_ref[0] = val.astype(BF16)
        return
    for c in range(scr.shape[0]):
        scr[c] = val[:, c * HEAD_DIM:(c + 1) * HEAD_DIM]
    n = scr.shape[1] // dilation
    for r in range(dilation):
        for c in range(scr.shape[0]):
            o_ref[r, :, c * HEAD_DIM:(c + 1) * HEAD_DIM] = (
                scr[c, pl.ds(r, n, stride=dilation), :].astype(BF16))


def _inproj_dil(x_bf, w_bf, cos_tab, sin_tab, g, dilation):
    s, d = x_bf.shape
    tm = min(INPROJ_TM, s)
    n = tm // dilation
    tab = pl.BlockSpec((None, tm, HEAD_DIM), lambda i, p: ((p == 2).astype(I32), i, 0))
    return pl.pallas_call(
        functools.partial(_inproj_dil_kernel, dilation=dilation),
        out_shape=jax.ShapeDtypeStruct((dilation, s // dilation, 3 * PROJ_TILE), BF16),
        grid=(s // tm, 3),
        in_specs=[
            pl.BlockSpec((tm, d), lambda i, p: (i, 0)),
            pl.BlockSpec((d, PROJ_TILE), lambda i, p: (0, DIL_SRC_TILE0 + g + 3 * p)),
            tab, tab,
        ],
        out_specs=pl.BlockSpec((dilation, n, PROJ_TILE), lambda i, p: (0, i, p)),
        scratch_shapes=[pltpu.VMEM((PROJ_TILE // HEAD_DIM, tm, HEAD_DIM), F32)],
        compiler_params=_cparams(("arbitrary", "arbitrary"), VMEM_LIMIT),
        name=f"inproj_dil{g}",
    )(x_bf, w_bf, cos_tab, sin_tab)


def _diff_kernel(qi_ref, kj_ref, q_ref, k_ref, v_ref, lam_ref, g_ref, o_ref, m_sc, l_sc, acc_sc,
                 *, lambda_init):
    step = pl.program_id(1)
    qi = qi_ref[step]
    kj = kj_ref[step]
    tq, tk = q_ref.shape[0], k_ref.shape[0]
    last_kj = qi // (tk // tq)

    @pl.when(kj == 0)
    def _():
        m_sc[...] = jnp.full(m_sc.shape, NEG, F32)
        l_sc[...] = jnp.zeros(l_sc.shape, F32)
        acc_sc[...] = jnp.zeros(acc_sc.shape, F32)

    def update(masked):
        v = v_ref[...]
        if masked:
            row = qi * tq + lax.broadcasted_iota(I32, (tq, tk), 0)
            col = kj * tk + lax.broadcasted_iota(I32, (tq, tk), 1)
            keep = col <= row
        for mm in range(2):
            q = q_ref[:, mm * HEAD_DIM:(mm + 1) * HEAD_DIM]
            k = k_ref[:, mm * HEAD_DIM:(mm + 1) * HEAD_DIM]
            s = lax.dot_general(q, k, (((1,), (1,)), ((), ())), preferred_element_type=F32)
            if masked:
                s = jnp.where(keep, s, NEG)
            m_prev = m_sc[mm]
            m_new = jnp.maximum(m_prev, jnp.max(s, axis=-1, keepdims=True))
            alpha = jnp.exp2(m_prev - m_new)
            p = jnp.exp2(s - jnp.tile(m_new, (1, tk // LANES)))
            l_sc[mm] = alpha * l_sc[mm] + jnp.sum(p, axis=-1, keepdims=True)
            pv = jnp.dot(p.astype(BF16), v, preferred_element_type=F32)
            acc_sc[mm] = jnp.tile(alpha, (1, v.shape[1] // LANES)) * acc_sc[mm] + pv
            m_sc[mm] = m_new

    @pl.when(kj < last_kj)
    def _():
        update(False)

    @pl.when(kj == last_kj)
    def _():
        update(True)
        lam_p = lam_ref[...]
        lam = (jnp.exp(jnp.sum(lam_p[0:1] * lam_p[1:2], axis=-1, keepdims=True))
               - jnp.exp(jnp.sum(lam_p[2:3] * lam_p[3:4], axis=-1, keepdims=True))
               + lambda_init)
        rep = acc_sc.shape[2] // LANES
        o = (acc_sc[0] / jnp.tile(l_sc[0], (1, rep))
             - lam * (acc_sc[1] / jnp.tile(l_sc[1], (1, rep))))
        o = o * lax.rsqrt(jnp.mean(o * o, axis=-1, keepdims=True) + LN_EPS) * g_ref[...]
        o_ref[...] = (o * (1.0 - lambda_init)).astype(o_ref.dtype)


def _diff_attention(proj, lam_p, norm_g, lambda_init):
    s = proj.shape[0]
    tq = min(ATTN_TQ, s)
    tk = min(ATTN_TK, s)
    pairs = [(i, j) for i in range(s // tq) for j in range(i * tq // tk + 1)]
    qi_tab = np.asarray([p[0] for p in pairs], np.int32)
    kj_tab = np.asarray([p[1] for p in pairs], np.int32)
    vw = 2 * HEAD_DIM
    grid_spec = pltpu.PrefetchScalarGridSpec(
        num_scalar_prefetch=2,
        grid=(DIFF_HEADS, len(pairs)),
        in_specs=[
            pl.BlockSpec((tq, vw), lambda h, st, qi, kj: (qi[st], h)),
            pl.BlockSpec((tk, vw), lambda h, st, qi, kj: (kj[st], DIFF_HEADS + h)),
            pl.BlockSpec((tk, vw), lambda h, st, qi, kj: (kj[st], 2 * DIFF_HEADS + h)),
            pl.BlockSpec((4, HEAD_DIM), lambda h, st, qi, kj: (0, 0)),
            pl.BlockSpec((1, vw), lambda h, st, qi, kj: (0, 0)),
        ],
        out_specs=pl.BlockSpec((tq, vw), lambda h, st, qi, kj: (qi[st], h)),
        scratch_shapes=[
            pltpu.VMEM((2, tq, LANES), F32),
            pltpu.VMEM((2, tq, LANES), F32),
            pltpu.VMEM((2, tq, vw), F32),
        ],
    )
    return pl.pallas_call(
        functools.partial(_diff_kernel, lambda_init=lambda_init),
        out_shape=jax.ShapeDtypeStruct((s, DIFF_HEADS * vw), BF16),
        grid_spec=grid_spec,
        compiler_params=_cparams(("arbitrary", "arbitrary"), VMEM_LIMIT),
        name="diffattn",
    )(jnp.asarray(qi_tab), jnp.asarray(kj_tab), proj, proj, proj, lam_p, norm_g)


def _dil_kernel(q_ref, kp_ref, kc_ref, vp_ref, vc_ref, o_ref, lse_ref):
    n = pl.program_id(1)
    t = q_ref.shape[0]
    qi = lax.broadcasted_iota(I32, (t, t), 0)
    kj = lax.broadcasted_iota(I32, (t, t), 1)
    keep_prev = (kj >= qi) & (n > 0)
    keep_cur = kj <= qi
    for h in range(DIL_HEADS_PER_GROUP):
        sl = slice(h * HEAD_DIM, (h + 1) * HEAD_DIM)
        q = q_ref[:, sl]
        dn = (((1,), (1,)), ((), ()))
        sp = lax.dot_general(q, kp_ref[:, sl], dn, preferred_element_type=F32)
        sc = lax.dot_general(q, kc_ref[:, sl], dn, preferred_element_type=F32)
        sp = jnp.where(keep_prev, sp, NEG)
        sc = jnp.where(keep_cur, sc, NEG)
        m = jnp.maximum(jnp.max(sp, axis=-1, keepdims=True), jnp.max(sc, axis=-1, keepdims=True))
        ep = jnp.exp(sp - m)
        ec = jnp.exp(sc - m)
        den = jnp.sum(ep, axis=-1, keepdims=True) + jnp.sum(ec, axis=-1, keepdims=True)
        acc = (jnp.dot(ep.astype(BF16), vp_ref[:, sl], preferred_element_type=F32)
               + jnp.dot(ec.astype(BF16), vc_ref[:, sl], preferred_element_type=F32))
        o_ref[:, sl] = acc / den
        lse_ref[:, sl] = jnp.broadcast_to(m + jnp.log(den), (t, HEAD_DIM))


def _dilated_group(qkv, g):
    dilation, l, _ = qkv.shape
    t = DIL_T
    cur = lambda part: (lambda r, n: (r, n, part))
    prev = lambda part: (lambda r, n: (r, jnp.maximum(n - 1, 0), part))
    blk = (None, t, PROJ_TILE)
    return pl.pallas_call(
        _dil_kernel,
        out_shape=[jax.ShapeDtypeStruct((dilation, l, DIL_OUT), F32)] * 2,
        grid=(dilation, l // t),
        in_specs=[
            pl.BlockSpec(blk, cur(0)),
            pl.BlockSpec(blk, prev(1)),
            pl.BlockSpec(blk, cur(1)),
            pl.BlockSpec(blk, prev(2)),
            pl.BlockSpec(blk, cur(2)),
        ],
        out_specs=[pl.BlockSpec((None, t, DIL_OUT), lambda r, n: (r, n, 0))] * 2,
        compiler_params=_cparams(("arbitrary", "arbitrary")),
        name=f"dilattn{g}",
    )(qkv, qkv, qkv, qkv, qkv)


def _layer_norm(z, g, b):
    mu = jnp.mean(z, axis=-1, keepdims=True)
    zc = z - mu
    var = jnp.mean(zc * zc, axis=-1, keepdims=True)
    return zc * lax.rsqrt(var + LN_EPS) * g + b


def _merge_kernel(o0, o1, o2, l0, l1, l2, yd_ref, ga_ref, gb_ref, x_ref, wbd_ref, wbl_ref, wo_ref,
                  rwt_ref, g_ref, b_ref, h_ref, lt_ref, scr, *, alpha):
    def positions(ref, slot):
        d = ref.shape[0]
        if d == 1:
            return ref[0]
        heads = scr.shape[1]
        for r in range(d):
            for c in range(heads):
                scr[slot, c, pl.ds(r, ref.shape[1], stride=d), :] = (
                    ref[r, :, c * HEAD_DIM:(c + 1) * HEAD_DIM])
        return jnp.concatenate([scr[slot, c] for c in range(heads)], axis=1)

    la, lb, lc = positions(l0, 0), positions(l1, 0), positions(l2, 1)
    m = jnp.maximum(jnp.maximum(la, lb), lc)
    ea, eb, ec = jnp.exp(la - m), jnp.exp(lb - m), jnp.exp(lc - m)
    num = ea * positions(o0, 0) + eb * positions(o1, 2) + ec * positions(o2, 3)
    ydil = num / (ea + eb + ec)
    a = jnp.dot(yd_ref[...], wbd_ref[...], preferred_element_type=F32)
    b = jnp.dot(ydil.astype(BF16), wbl_ref[...], preferred_element_type=F32)
    merged = ga_ref[...].astype(F32) * a + gb_ref[...].astype(F32) * b
    z = alpha * x_ref[...] + jnp.dot(merged.astype(BF16), wo_ref[...], preferred_element_type=F32)
    h = _layer_norm(z, g_ref[...], b_ref[...])
    h_ref[...] = h
    lt_ref[...] = lax.dot_general(rwt_ref[...], h.astype(BF16), (((1,), (1,)), ((), ())),
                                  preferred_element_type=F32)


def _merge(dil_o, dil_l, y_diff, gates, x, wbd, wbl, wo, rwt, ln_g, ln_b, alpha):
    s, d = x.shape
    tm = min(MERGE_TM, s)
    row = lambda w: pl.BlockSpec((tm, w), lambda i: (i, 0))
    const = lambda a: pl.BlockSpec(a.shape, lambda i: (0, 0), pipeline_mode=pl.Buffered(1))
    res = lambda a: pl.BlockSpec((a.shape[0], tm // a.shape[0], DIL_OUT), lambda i: (0, i, 0))
    in_specs = ([res(a) for a in dil_o] + [res(a) for a in dil_l] + [
        row(y_diff.shape[1]),
        pl.BlockSpec((tm, d), lambda i: (i, 0)),
        pl.BlockSpec((tm, d), lambda i: (i, 1)),
        row(d), const(wbd), const(wbl), const(wo), const(rwt), const(ln_g), const(ln_b)])
    return pl.pallas_call(
        functools.partial(_merge_kernel, alpha=alpha),
        out_shape=[jax.ShapeDtypeStruct((s, d), F32),
                   jax.ShapeDtypeStruct((N_EXPERTS, s), F32)],
        grid=(s // tm,),
        in_specs=in_specs,
        out_specs=[row(d), pl.BlockSpec((N_EXPERTS, tm), lambda i: (0, i))],
        scratch_shapes=[pltpu.VMEM((4, DIL_HEADS_PER_GROUP, tm, HEAD_DIM), F32)],
        compiler_params=_cparams(("arbitrary",), VMEM_LIMIT),
        name="merge",
    )(*dil_o, *dil_l, y_diff, gates, gates, x, wbd, wbl, wo, rwt, ln_g, ln_b)


def _route_kernel(lt_ref, bias_ref, e_ref, w_ref, r_ref, cnt_ref, carry):
    i = pl.program_id(0)
    tm = lt_ref.shape[1]
    ninf = -jnp.inf

    @pl.when(i == 0)
    def _():
        carry[...] = jnp.zeros(carry.shape, F32)

    sc = jax.nn.sigmoid(lt_ref[...])
    biased = sc + bias_ref[...]
    iog = lax.broadcasted_iota(I32, (GROUP_SIZE, tm), 0)
    blocks, gscore = [], []
    for g in range(N_GROUPS):
        blk = biased[g * GROUP_SIZE:(g + 1) * GROUP_SIZE, :]
        m1 = jnp.max(blk, axis=0, keepdims=True)
        i1 = jnp.min(jnp.where(blk == m1, iog, GROUP_SIZE), axis=0, keepdims=True)
        m2 = jnp.max(jnp.where(iog == i1, ninf, blk), axis=0, keepdims=True)
        blocks.append(blk)
        gscore.append(m1 + m2)
    masked = []
    for g in range(N_GROUPS):
        ahead = jnp.zeros((1, tm), F32)
        for o in range(N_GROUPS):
            if o == g:
                continue
            wins = (gscore[o] >= gscore[g]) if o < g else (gscore[o] > gscore[g])
            ahead = ahead + jnp.where(wins, 1.0, 0.0)
        keep = jnp.broadcast_to(ahead, (GROUP_SIZE, tm)) < TOPK_GROUPS
        masked.append(jnp.where(keep, blocks[g], ninf))
    v = jnp.concatenate(masked, axis=0)
    ioe = lax.broadcasted_iota(I32, (N_EXPERTS, tm), 0)
    idxs, ws = [], []
    sel = jnp.zeros((N_EXPERTS, tm), jnp.bool_)
    for _ in range(TOP_K):
        mx = jnp.max(v, axis=0, keepdims=True)
        idx = jnp.min(jnp.where(v == mx, ioe, N_EXPERTS), axis=0, keepdims=True)
        hit = ioe == idx
        idxs.append(idx)
        ws.append(jnp.sum(jnp.where(hit, sc, 0.0), axis=0, keepdims=True))
        sel = sel | hit
        v = jnp.where(hit, ninf, v)
    wsum = ws[0]
    for wk in ws[1:]:
        wsum = wsum + wk

    self = jnp.where(sel, 1.0, 0.0)
    ta = lax.broadcasted_iota(I32, (tm, tm), 0)
    tb = lax.broadcasted_iota(I32, (tm, tm), 1)
    before = jnp.where(ta < tb, 1.0, 0.0).astype(BF16)
    pos = jnp.dot(self.astype(BF16), before, preferred_element_type=F32) + carry[...]
    for k in range(TOP_K):
        e_ref[k:k + 1, :] = idxs[k]
        w_ref[k:k + 1, :] = ws[k] / wsum * ROUTED_SCALE
        rk = jnp.sum(jnp.where(ioe == idxs[k], pos, 0.0), axis=0, keepdims=True)
        r_ref[k:k + 1, :] = rk.astype(I32)
    total = carry[...] + jnp.sum(self, axis=1, keepdims=True)
    carry[...] = total
    cnt_ref[...] = total.astype(I32)


def _route(logits_t, bias_col):
    e, s = logits_t.shape
    tm = min(ROUTE_TM, s)
    tok = pl.BlockSpec((TOP_K, tm), lambda i: (0, i))
    return pl.pallas_call(
        _route_kernel,
        out_shape=[jax.ShapeDtypeStruct((TOP_K, s), I32),
                   jax.ShapeDtypeStruct((TOP_K, s), F32),
                   jax.ShapeDtypeStruct((TOP_K, s), I32),
                   jax.ShapeDtypeStruct((e, 1), I32)],
        grid=(s // tm,),
        in_specs=[pl.BlockSpec((e, tm), lambda i: (0, i)),
                  pl.BlockSpec((e, 1), lambda i: (0, 0))],
        out_specs=[tok, tok, tok, pl.BlockSpec((e, 1), lambda i: (0, 0))],
        scratch_shapes=[pltpu.VMEM((e, 1), F32)],
        compiler_params=_cparams(("arbitrary",)),
        name="route",
    )(logits_t, bias_col)


def _slots_kernel(e_ref, r_ref, ps_ref, s_ref):
    tm = e_ref.shape[1]
    ioe = lax.broadcasted_iota(I32, (N_EXPERTS, tm), 0)
    ps = ps_ref[...]
    for k in range(TOP_K):
        start = jnp.sum(jnp.where(ioe == e_ref[k:k + 1, :], ps, 0.0), axis=0, keepdims=True)
        s_ref[k:k + 1, :] = start.astype(I32) + r_ref[k:k + 1, :]


def _slots(eidx, rank, pstart_col):
    s = eidx.shape[1]
    tm = min(ROUTE_TM, s)
    tok = pl.BlockSpec((TOP_K, tm), lambda i: (0, i))
    return pl.pallas_call(
        _slots_kernel,
        out_shape=jax.ShapeDtypeStruct((TOP_K, s), I32),
        grid=(s // tm,),
        in_specs=[tok, tok, pl.BlockSpec((N_EXPERTS, 1), lambda i: (0, 0))],
        out_specs=tok,
        compiler_params=_cparams(("arbitrary",)),
        name="slots",
    )(eidx, rank, pstart_col)


def _row_pieces(length, fn):
    off = jnp.int32(0)
    size = EXP_CHUNK // 2
    while size >= SUBLANES:
        @pl.when((length & size) != 0)
        def _(off=off, size=size):
            fn(off, size)
        off = off + (length & size)
        size //= 2


def _fill_tail(end, n_rows, piece):
    tail = n_rows - end
    n_full = tail // EXP_CHUNK

    def full(t, carry):
        piece(pl.multiple_of(end + t * EXP_CHUNK, SUBLANES), EXP_CHUNK)
        return carry

    lax.fori_loop(0, n_full, full, 0)
    base = end + n_full * EXP_CHUNK
    _row_pieces(tail & (EXP_CHUNK - 1),
                lambda off, size: piece(pl.multiple_of(base + off, SUBLANES), size))


def _round_up(n, m):
    return (n + (m - 1)) // m * m


def _scatter_kernel(rstart_ref, cnt_ref, slot_ref, h_ref, xs_ref, zbuf, sem, zsem):
    i = pl.program_id(0)
    tm = h_ref.shape[0]
    n_exp = cnt_ref.shape[0]

    def zero_fill(do):
        def piece(row, size):
            do(pltpu.make_async_copy(zbuf.at[pl.ds(0, size)], xs_ref.at[pl.ds(row, size)], zsem))

        def expert(ex, carry):
            cnt = cnt_ref[ex]
            row = rstart_ref[ex] + cnt
            for j in range(SUBLANES - 1):
                @pl.when(j < ((-cnt) & (SUBLANES - 1)))
                def _(row=row, j=j):
                    piece(row + j, 1)
            return carry

        lax.fori_loop(0, n_exp, expert, 0)
        end = rstart_ref[n_exp - 1] + _round_up(cnt_ref[n_exp - 1], SUBLANES)
        _fill_tail(end, xs_ref.shape[0], piece)

    def issue(r, carry):
        for k in range(TOP_K):
            dst = slot_ref[r * TOP_K + k]
            pltpu.make_async_copy(h_ref.at[pl.ds(r, 1)], xs_ref.at[pl.ds(dst, 1)],
                                  sem).start(priority=k % 2)
        return carry

    lax.fori_loop(0, tm, issue, 0)
    for k in range(TOP_K):
        pltpu.make_async_copy(h_ref, xs_ref.at[pl.ds(0, tm)], sem).wait()

    @pl.when(i == pl.num_programs(0) - 1)
    def _():
        zbuf[...] = jnp.zeros(zbuf.shape, F32)
        zero_fill(lambda cp: cp.start())
        zero_fill(lambda cp: cp.wait())


def _scatter_rows(rstart, counts, slots_flat, h, n_rows):
    s, d = h.shape
    tm = min(ROW_TM, s)
    grid_spec = pltpu.PrefetchScalarGridSpec(
        num_scalar_prefetch=2,
        grid=(s // tm,),
        in_specs=[pl.BlockSpec((tm * TOP_K,), lambda i, ps, cn: (i,), memory_space=pltpu.SMEM),
                  pl.BlockSpec((tm, d), lambda i, ps, cn: (i, 0))],
        out_specs=pl.BlockSpec(memory_space=pl.ANY),
        scratch_shapes=[pltpu.VMEM((EXP_CHUNK, d), F32), pltpu.SemaphoreType.DMA,
                        pltpu.SemaphoreType.DMA],
    )
    return pl.pallas_call(
        _scatter_kernel,
        out_shape=jax.ShapeDtypeStruct((n_rows, d), F32),
        grid_spec=grid_spec,
        compiler_params=_cparams(("arbitrary",)),
        name="scatter",
    )(rstart, counts, slots_flat, h)


def _experts_kernel(chunk0_ref, cnt_ref, row0_ref, valid_ref, xs_ref, wg_ref, wu_ref, wd_ref,
                    ys_ref, wg_buf, wu_buf, wd_buf, wgu_sc, wd_sc, xbuf, ybuf, w_sem, in_sem,
                    out_sem):
    e = pl.program_id(0)
    n_exp = pl.num_programs(0)
    last = n_exp - 1
    ff = wg_ref.shape[2]
    chunks_of = lambda ex: (cnt_ref[ex] + (EXP_CHUNK - 1)) // EXP_CHUNK

    def weight_copies(ex):
        ws = ex % W_RING
        half = wd_ref.shape[1] // 2
        lo, hi = pl.ds(0, half), pl.ds(half, half)
        parts = ((wg_ref.at[ex], wg_buf.at[ws]), (wu_ref.at[ex], wu_buf.at[ws]),
                 (wd_ref.at[ex, lo], wd_buf.at[ws, lo]), (wd_ref.at[ex, hi], wd_buf.at[ws, hi]))
        return [(pltpu.make_async_copy(src, dst, w_sem.at[i, ws]), 0)
                for i, (src, dst) in enumerate(parts)]

    @pl.when(e == 0)
    def _():
        for ex in range(W_RING - 1):
            for cp, queue in weight_copies(ex):
                cp.start(priority=queue)

    @pl.when(e + (W_RING - 1) < n_exp)
    def _():
        for cp, queue in weight_copies(e + (W_RING - 1)):
            cp.start(priority=queue)

    for cp, _ in weight_copies(e):
        cp.wait()
    wslot = e % W_RING
    g0 = chunk0_ref[e]
    nch = chunks_of(e)
    total = chunk0_ref[last] + chunks_of(last)

    def move(g, n_rows, hbm, buf, sem, do):
        row0 = pl.multiple_of(row0_ref[g], SUBLANES)

        def piece(off, size):
            src = hbm.at[pl.ds(row0 + off, size)]
            dst = buf.at[pl.ds(off, size)]
            do(pltpu.make_async_copy(src, dst, sem) if hbm is xs_ref
               else pltpu.make_async_copy(dst, src, sem))

        @pl.when(n_rows == EXP_CHUNK)
        def _():
            piece(0, EXP_CHUNK)

        @pl.when(n_rows < EXP_CHUNK)
        def _():
            _row_pieces(n_rows, lambda off, size: piece(pl.multiple_of(off, SUBLANES), size))
            whole = n_rows & ~(SUBLANES - 1)
            for j in range(SUBLANES - 1):
                @pl.when(j < (n_rows & (SUBLANES - 1)))
                def _(j=j):
                    piece(whole + j, 1)

    def fetch(g, slot, do):
        move(g, valid_ref[g], xs_ref, xbuf.at[slot], in_sem.at[slot], do)

    def write_back(g, slot, do):
        move(g, _round_up(valid_ref[g], SUBLANES), ys_ref, ybuf.at[slot], out_sem.at[slot], do)

    def start(cp):
        cp.start(priority=1)

    def wait(cp):
        cp.wait()

    n_x, n_y = xbuf.shape[0], ybuf.shape[0]
    ahead = n_x - 1

    @pl.when(e == 0)
    def _():
        xbuf[...] = jnp.zeros(xbuf.shape, F32)
        for j in range(ahead):
            @pl.when(j < total)
            def _(j=j):
                fetch(j, j, start)

    @pl.when(nch > 0)
    def _():
        wgu_sc[:, :ff] = wg_buf[wslot].astype(BF16)
        wgu_sc[:, ff:] = wu_buf[wslot].astype(BF16)
        wd_sc[...] = wd_buf[wslot].astype(BF16)

        def chunk(c, carry):
            g = g0 + c
            slot = g % n_x
            fetch(g, slot, wait)

            @pl.when(g + ahead < total)
            def _():
                fetch(g + ahead, (g + ahead) % n_x, start)

            gu = jnp.dot(xbuf[slot].astype(BF16), wgu_sc[...], preferred_element_type=F32)
            hmid = jax.nn.silu(gu[:, :ff]) * gu[:, ff:]
            y = jnp.dot(hmid.astype(BF16), wd_sc[...], preferred_element_type=F32)

            yslot = g % n_y

            @pl.when(g >= n_y)
            def _():
                write_back(g - n_y, yslot, wait)

            ybuf[yslot] = y
            write_back(g, yslot, start)
            return carry

        lax.fori_loop(0, nch, chunk, 0)

    @pl.when(e == last)
    def _():
        for j in range(n_y):
            @pl.when(total > j)
            def _(j=j):
                write_back(total - 1 - j, (total - 1 - j) % n_y, wait)

        tail_g = jnp.maximum(total - 1, 0)
        end = jnp.where(total > 0, row0_ref[tail_g] + _round_up(valid_ref[tail_g], SUBLANES), 0)
        ybuf[0] = jnp.zeros(ybuf.shape[1:], F32)

        def zero_fill(do):
            def piece(row, size):
                do(pltpu.make_async_copy(ybuf.at[0, pl.ds(0, size)], ys_ref.at[pl.ds(row, size)],
                                         out_sem.at[0]))
            _fill_tail(end, ys_ref.shape[0], piece)

        zero_fill(start)
        zero_fill(wait)


def _experts(chunk0, counts, chunk_row0, chunk_valid, xs, wg, wu, wd):
    n_rows, d = xs.shape
    ne, _, ff = wg.shape
    grid_spec = pltpu.PrefetchScalarGridSpec(
        num_scalar_prefetch=4,
        grid=(ne,),
        in_specs=[
            pl.BlockSpec(memory_space=pl.ANY),
            pl.BlockSpec(memory_space=pl.ANY),
            pl.BlockSpec(memory_space=pl.ANY),
            pl.BlockSpec(memory_space=pl.ANY),
        ],
        out_specs=pl.BlockSpec(memory_space=pl.ANY),
        scratch_shapes=[
            pltpu.VMEM((W_RING, d, ff), F32),
            pltpu.VMEM((W_RING, d, ff), F32),
            pltpu.VMEM((W_RING, ff, d), F32),
            pltpu.VMEM((d, 2 * ff), BF16),
            pltpu.VMEM((ff, d), BF16),
            pltpu.VMEM((X_SLOTS, EXP_CHUNK, d), F32),
            pltpu.VMEM((Y_SLOTS, EXP_CHUNK, d), F32),
            pltpu.SemaphoreType.DMA((4, W_RING)),
            pltpu.SemaphoreType.DMA((X_SLOTS,)),
            pltpu.SemaphoreType.DMA((Y_SLOTS,)),
        ],
    )
    return pl.pallas_call(
        _experts_kernel,
        out_shape=jax.ShapeDtypeStruct((n_rows, d), F32),
        grid_spec=grid_spec,
        compiler_params=_cparams(("arbitrary",), VMEM_LIMIT),
        name="experts",
    )(chunk0, counts, chunk_row0, chunk_valid, xs, wg, wu, wd)


def _combine_kernel(slot_ref, ys_ref, wt_ref, h_ref, wgu_ref, wd_ref, g_ref, b_ref, o_ref,
                    gbuf, sem, *, alpha):
    tm = h_ref.shape[0]
    ff = wd_ref.shape[0]

    def issue(r, carry):
        for k in range(TOP_K):
            src = slot_ref[r * TOP_K + k]
            pltpu.make_async_copy(ys_ref.at[pl.ds(src, 1)], gbuf.at[k, pl.ds(r, 1)],
                                  sem).start(priority=k % 2)
        return carry

    lax.fori_loop(0, tm, issue, 0)

    h = h_ref[...]
    gu = jnp.dot(h.astype(BF16), wgu_ref[...], preferred_element_type=F32)
    hmid = jax.nn.silu(gu[:, :ff]) * gu[:, ff:]
    acc = jnp.dot(hmid.astype(BF16), wd_ref[...], preferred_element_type=F32)

    for k in range(TOP_K):
        pltpu.make_async_copy(ys_ref.at[pl.ds(0, tm)], gbuf.at[k], sem).wait()
    wt = wt_ref[...]
    for k in range(TOP_K):
        acc = acc + wt[:, k:k + 1] * gbuf[k]
    o_ref[...] = _layer_norm(alpha * h + acc, g_ref[...], b_ref[...])


def _combine(slots_flat, ys, wt, h, wgu_s, wd_s, ln_g, ln_b, alpha):
    s, d = h.shape
    tm = min(ROW_TM, s)
    row = lambda c: pl.BlockSpec((tm, c), lambda i: (i, 0))
    const = lambda a: pl.BlockSpec(a.shape, lambda i: (0, 0), pipeline_mode=pl.Buffered(1))
    return pl.pallas_call(
        functools.partial(_combine_kernel, alpha=alpha),
        out_shape=jax.ShapeDtypeStruct((s, d), F32),
        grid=(s // tm,),
        in_specs=[pl.BlockSpec((tm * TOP_K,), lambda i: (i,), memory_space=pltpu.SMEM),
                  pl.BlockSpec(memory_space=pl.ANY),
                  row(TOP_K), row(d), const(wgu_s), const(wd_s), const(ln_g), const(ln_b)],
        out_specs=row(d),
        scratch_shapes=[pltpu.VMEM((TOP_K, tm, d), F32), pltpu.SemaphoreType.DMA],
        compiler_params=_cparams(("arbitrary",), VMEM_LIMIT),
        name="combine",
    )(slots_flat, ys, wt, h, wgu_s, wd_s, ln_g, ln_b)


def _rope_tables(s):
    half = HEAD_DIM // 2
    inv = ROPE_THETA ** (-jnp.arange(half, dtype=F32) / half)
    ang = jnp.arange(s).astype(F32)[:, None] * inv[None, :]
    cos, sin = jnp.cos(ang), jnp.sin(ang)
    cos2, sin2 = jnp.concatenate([cos, cos], axis=1), jnp.concatenate([-sin, sin], axis=1)
    return jnp.stack([cos2, jnp.ones_like(cos2)]), jnp.stack([sin2, jnp.zeros_like(sin2)])


def _layer(x, l, depth, w_in, b_gate, lambda_q1, lambda_k1, lambda_q2, lambda_k2, diff_norm_g,
           w_branch_diff, w_branch_dil, w_out, ln1_g, ln1_b, router_w, router_bias,
           w_gate_e, w_up_e, w_down_e, w_gate_s, w_up_s, w_down_s, ln2_g, ln2_b):
    s, d = x.shape
    alpha = (2.0 * depth) ** 0.25
    lambda_init = 0.8 - 0.6 * math.exp(-0.3 * l)
    cos, sin = _rope_tables(s)
    x_bf = x.astype(BF16)
    w_bf = w_in[l].astype(BF16)

    gates = _inproj_gates(x_bf, w_bf, b_gate[l].reshape(1, 2 * d))
    qkv = _inproj_qkv(x_bf, w_bf, cos, sin)
    lam_p = jnp.stack([lambda_q1[l], lambda_k1[l], lambda_q2[l], lambda_k2[l]]).astype(F32)
    y_diff = _diff_attention(qkv, lam_p, diff_norm_g[l].reshape(1, -1), lambda_init)

    dil = [_dilated_group(_inproj_dil(x_bf, w_bf, cos, sin, g, dilation), g)
           for g, (_, dilation) in enumerate(DIL_CONFIGS)]
    h, logits_t = _merge(
        [o for o, _ in dil], [ls for _, ls in dil], y_diff, gates, x,
        w_branch_diff[l].astype(BF16), w_branch_dil[l].astype(BF16), w_out[l].astype(BF16),
        router_w[l].T.astype(BF16), ln1_g[l].reshape(1, d), ln1_b[l].reshape(1, d), alpha)

    eidx, wts, rank, counts = _route(logits_t, router_bias[l].reshape(N_EXPERTS, 1).astype(F32))
    counts = counts.reshape(N_EXPERTS)
    padded = _round_up(counts, SUBLANES)
    rstart = (jnp.cumsum(padded) - padded).astype(I32)
    n_chunks = (counts + (EXP_CHUNK - 1)) // EXP_CHUNK
    chunk0 = (jnp.cumsum(n_chunks) - n_chunks).astype(I32)
    max_chunks = s * TOP_K // EXP_CHUNK + N_EXPERTS
    cid = jnp.arange(max_chunks, dtype=I32)
    chunk_end = chunk0 + n_chunks
    chunk_expert = jnp.minimum(jnp.sum(chunk_end[None, :] <= cid[:, None], axis=1), N_EXPERTS - 1)
    local = cid - chunk0[chunk_expert]
    chunk_valid = jnp.clip(counts[chunk_expert] - local * EXP_CHUNK, 0, EXP_CHUNK).astype(I32)
    chunk_row0 = (rstart[chunk_expert] + local * EXP_CHUNK).astype(I32)
    n_rows = s * TOP_K + N_EXPERTS * SUBLANES

    slots = _slots(eidx, rank, rstart.astype(F32).reshape(N_EXPERTS, 1))
    slots_flat = slots.T.reshape(-1)
    xs = _scatter_rows(rstart, counts, slots_flat, h, n_rows)
    ys = _experts(chunk0, counts, chunk_row0, chunk_valid, xs,
                  w_gate_e[l], w_up_e[l], w_down_e[l])
    wgu_s = jnp.concatenate([w_gate_s[l], w_up_s[l]], axis=1).astype(BF16)
    return _combine(slots_flat, ys, wts.T, h, wgu_s, w_down_s[l].astype(BF16),
                    ln2_g[l].reshape(1, d), ln2_b[l].reshape(1, d), alpha)


def kernel(x, w_in, b_gate, lambda_q1, lambda_k1, lambda_q2, lambda_k2, diff_norm_g, w_branch_diff,
           w_branch_dil, w_out, ln1_g, ln1_b, router_w, router_bias, w_gate_e, w_up_e, w_down_e,
           w_gate_s, w_up_s, w_down_s, ln2_g, ln2_b):
    b, s, d = x.shape
    depth = w_in.shape[0]
    outs = []
    for bi in range(b):
        xb = x[bi]
        for l in range(depth):
            xb = _layer(xb, l, depth, w_in, b_gate, lambda_q1, lambda_k1, lambda_q2, lambda_k2,
                        diff_norm_g, w_branch_diff, w_branch_dil, w_out, ln1_g, ln1_b, router_w,
                        router_bias, w_gate_e, w_up_e, w_down_e, w_gate_s, w_up_s, w_down_s,
                        ln2_g, ln2_b)
        outs.append(xb)
    return jnp.stack(outs)
```

```python
import functools
import math

import jax
import jax.numpy as jnp
import numpy as np
from jax import lax
from jax.experimental import pallas as pl
from jax.experimental.pallas import tpu as pltpu

F32 = jnp.float32
BF16 = jnp.bfloat16
I32 = jnp.int32

HEAD_DIM = 128
ROPE_THETA = 10000.0
LN_EPS = 1e-5
DIFF_HEADS = 4
DIL_CONFIGS = ((128, 1), (512, 4), (2048, 16))
DIL_HEADS_PER_GROUP = 4
DIL_OUT = DIL_HEADS_PER_GROUP * HEAD_DIM
N_EXPERTS = 256
TOP_K = 8
N_GROUPS = 8
TOPK_GROUPS = 4
GROUP_SIZE = N_EXPERTS // N_GROUPS
ROUTED_SCALE = 2.5
ATTN_SCALE = HEAD_DIM ** -0.5
LOG2E = math.log2(math.e)

PROJ_TILE = 512
N_QKV_TILES = 6
DIL_SRC_TILE0 = 6
GATE_SRC_TILE0 = 15
N_GATE_TILES = 8

INPROJ_TM = 1024
ATTN_TQ = 1024
ATTN_TK = 1024
LANES = 128
DIL_T = 128
DIL_BLOCKS_PER_STEP = 4
MERGE_TM = 256
ROUTE_TM = 512
ROW_TM = 128
EXP_CHUNK = 128
W_RING = 2
X_SLOTS = 8
Y_SLOTS = 6
SUBLANES = 8
NEG = -1e30
VMEM_LIMIT = 56 * 1024 * 1024


def _cparams(sem, vmem=None):
    return pltpu.CompilerParams(dimension_semantics=sem, vmem_limit_bytes=vmem)


def _rope(a, cos, sin):
    outs = []
    for h in range(a.shape[1] // HEAD_DIM):
        ah = a[:, h * HEAD_DIM:(h + 1) * HEAD_DIM]
        outs.append(ah * cos + pltpu.roll(ah, HEAD_DIM // 2, 1) * sin)
    return jnp.concatenate(outs, axis=1)


def _gates_kernel(x_ref, w_ref, b_ref, o_ref):
    acc = jnp.dot(x_ref[...], w_ref[...], preferred_element_type=F32)
    o_ref[...] = jax.nn.sigmoid(acc + b_ref[...]).astype(BF16)


def _inproj_gates(x_bf, w_bf, b_gate):
    s, d = x_bf.shape
    tm = min(INPROJ_TM, s)
    return pl.pallas_call(
        _gates_kernel,
        out_shape=jax.ShapeDtypeStruct((s, N_GATE_TILES * PROJ_TILE), BF16),
        grid=(s // tm, N_GATE_TILES),
        in_specs=[
            pl.BlockSpec((tm, d), lambda i, j: (i, 0)),
            pl.BlockSpec((d, PROJ_TILE), lambda i, j: (0, GATE_SRC_TILE0 + j)),
            pl.BlockSpec((1, PROJ_TILE), lambda i, j: (0, j)),
        ],
        out_specs=pl.BlockSpec((tm, PROJ_TILE), lambda i, j: (i, j)),
        compiler_params=_cparams(("arbitrary", "arbitrary"), VMEM_LIMIT),
        name="inproj_gates",
    )(x_bf, w_bf, b_gate)


def _qkv_kernel(x_ref, w_ref, cos_ref, sin_ref, o_ref):
    j = pl.program_id(1)
    acc = jnp.dot(x_ref[...], w_ref[...], preferred_element_type=F32)
    scale = jnp.where(j < 2, ATTN_SCALE * LOG2E, 1.0).astype(F32)
    o_ref[...] = _rope(acc, cos_ref[...] * scale, sin_ref[...] * scale).astype(BF16)


def _inproj_qkv(x_bf, w_bf, cos_tab, sin_tab):
    s, d = x_bf.shape
    tm = min(INPROJ_TM, s)
    tab = pl.BlockSpec((None, tm, HEAD_DIM), lambda i, j: ((j >= 4).astype(I32), i, 0))
    return pl.pallas_call(
        _qkv_kernel,
        out_shape=jax.ShapeDtypeStruct((s, N_QKV_TILES * PROJ_TILE), BF16),
        grid=(s // tm, N_QKV_TILES),
        in_specs=[
            pl.BlockSpec((tm, d), lambda i, j: (i, 0)),
            pl.BlockSpec((d, PROJ_TILE), lambda i, j: (0, j)),
            tab, tab,
        ],
        out_specs=pl.BlockSpec((tm, PROJ_TILE), lambda i, j: (i, j)),
        compiler_params=_cparams(("arbitrary", "arbitrary"), VMEM_LIMIT),
        name="inproj_qkv",
    )(x_bf, w_bf, cos_tab, sin_tab)


def _inproj_dil_kernel(x_ref, w_ref, cos_ref, sin_ref, o_ref, scr, *, dilation):
    part = pl.program_id(1)
    acc = jnp.dot(x_ref[...], w_ref[...], preferred_element_type=F32)
    scale = jnp.where(part == 0, ATTN_SCALE, 1.0).astype(F32)
    val = _rope(acc, cos_ref[...] * scale, sin_ref[...] * scale)
    if dilation == 1:
        o_ref[0] = val.astype(BF16)
        return
    for c in range(scr.shape[0]):
        scr[c] = val[:, c * HEAD_DIM:(c + 1) * HEAD_DIM]
    n = scr.shape[1] // dilation
    for r in range(dilation):
        for c in range(scr.shape[0]):
            o_ref[r, :, c * HEAD_DIM:(c + 1) * HEAD_DIM] = (
                scr[c, pl.ds(r, n, stride=dilation), :].astype(BF16))


def _inproj_dil(x_bf, w_bf, cos_tab, sin_tab, g, dilation):
    s, d = x_bf.shape
    tm = min(INPROJ_TM, s)
    n = tm // dilation
    tab = pl.BlockSpec((None, tm, HEAD_DIM), lambda i, p: ((p == 2).astype(I32), i, 0))
    return pl.pallas_call(
        functools.partial(_inproj_dil_kernel, dilation=dilation),
        out_shape=jax.ShapeDtypeStruct((dilation, s // dilation, 3 * PROJ_TILE), BF16),
        grid=(s // tm, 3),
        in_specs=[
            pl.BlockSpec((tm, d), lambda i, p: (i, 0)),
            pl.BlockSpec((d, PROJ_TILE), lambda i, p: (0, DIL_SRC_TILE0 + g + 3 * p)),
            tab, tab,
        ],
        out_specs=pl.BlockSpec((dilation, n, PROJ_TILE), lambda i, p: (0, i, p)),
        scratch_shapes=[pltpu.VMEM((PROJ_TILE // HEAD_DIM, tm, HEAD_DIM), F32)],
        compiler_params=_cparams(("arbitrary", "arbitrary"), VMEM_LIMIT),
        name=f"inproj_dil{g}",
    )(x_bf, w_bf, cos_tab, sin_tab)


def _diff_kernel(qi_ref, kj_ref, q_ref, k_ref, v_ref, lam_ref, g_ref, o_ref, m_sc, l_sc, acc_sc,
                 *, lambda_init):
    step = pl.program_id(1)
    qi = qi_ref[step]
    kj = kj_ref[step]
    tq, tk = q_ref.shape[0], k_ref.shape[0]
    last_kj = qi // (tk // tq)

    @pl.when(kj == 0)
    def _():
        m_sc[...] = jnp.full(m_sc.shape, NEG, F32)
        l_sc[...] = jnp.zeros(l_sc.shape, F32)
        acc_sc[...] = jnp.zeros(acc_sc.shape, F32)

    def update(rows, keys, masked):
        nr, nk = rows.stop - rows.start, keys.stop - keys.start
        v = v_ref[keys, :]
        if masked:
            row = qi * tq + rows.start + lax.broadcasted_iota(I32, (nr, nk), 0)
            col = kj * tk + keys.start + lax.broadcasted_iota(I32, (nr, nk), 1)
            keep = col <= row
        for mm in range(2):
            q = q_ref[rows, mm * HEAD_DIM:(mm + 1) * HEAD_DIM]
            k = k_ref[keys, mm * HEAD_DIM:(mm + 1) * HEAD_DIM]
            s = lax.dot_general(q, k, (((1,), (1,)), ((), ())), preferred_element_type=F32)
            if masked:
                s = jnp.where(keep, s, NEG)
            m_prev = m_sc[mm, rows]
            m_new = jnp.maximum(m_prev, jnp.max(s, axis=-1, keepdims=True))
            alpha = jnp.exp2(m_prev - m_new)
            p = jnp.exp2(s - jnp.tile(m_new, (1, nk // LANES)))
            l_sc[mm, rows] = alpha * l_sc[mm, rows] + jnp.sum(p, axis=-1, keepdims=True)
            pv = jnp.dot(p.astype(BF16), v, preferred_element_type=F32)
            acc_sc[mm, rows] = jnp.tile(alpha, (1, v.shape[1] // LANES)) * acc_sc[mm, rows] + pv
            m_sc[mm, rows] = m_new

    @pl.when(kj < last_kj)
    def _():
        update(slice(0, tq), slice(0, tk), False)

    @pl.when(kj == last_kj)
    def _():
        if tq == tk:
            half = tq // 2
            update(slice(0, half), slice(0, half), True)
            update(slice(half, tq), slice(0, tk), True)
        else:
            update(slice(0, tq), slice(0, tk), True)
        lam_p = lam_ref[...]
        lam = (jnp.exp(jnp.sum(lam_p[0:1] * lam_p[1:2], axis=-1, keepdims=True))
               - jnp.exp(jnp.sum(lam_p[2:3] * lam_p[3:4], axis=-1, keepdims=True))
               + lambda_init)
        rep = acc_sc.shape[2] // LANES
        o = (acc_sc[0] / jnp.tile(l_sc[0], (1, rep))
             - lam * (acc_sc[1] / jnp.tile(l_sc[1], (1, rep))))
        o = o * lax.rsqrt(jnp.mean(o * o, axis=-1, keepdims=True) + LN_EPS) * g_ref[...]
        o_ref[...] = (o * (1.0 - lambda_init)).astype(o_ref.dtype)


def _diff_attention(proj, lam_p, norm_g, lambda_init):
    s = proj.shape[0]
    tq = min(ATTN_TQ, s)
    tk = min(ATTN_TK, s)
    pairs = [(i, j) for i in range(s // tq) for j in range(i * tq // tk + 1)]
    qi_tab = np.asarray([p[0] for p in pairs], np.int32)
    kj_tab = np.asarray([p[1] for p in pairs], np.int32)
    vw = 2 * HEAD_DIM
    grid_spec = pltpu.PrefetchScalarGridSpec(
        num_scalar_prefetch=2,
        grid=(DIFF_HEADS, len(pairs)),
        in_specs=[
            pl.BlockSpec((tq, vw), lambda h, st, qi, kj: (qi[st], h)),
            pl.BlockSpec((tk, vw), lambda h, st, qi, kj: (kj[st], DIFF_HEADS + h)),
            pl.BlockSpec((tk, vw), lambda h, st, qi, kj: (kj[st], 2 * DIFF_HEADS + h)),
            pl.BlockSpec((4, HEAD_DIM), lambda h, st, qi, kj: (0, 0)),
            pl.BlockSpec((1, vw), lambda h, st, qi, kj: (0, 0)),
        ],
        out_specs=pl.BlockSpec((tq, vw), lambda h, st, qi, kj: (qi[st], h)),
        scratch_shapes=[
            pltpu.VMEM((2, tq, LANES), F32),
            pltpu.VMEM((2, tq, LANES), F32),
            pltpu.VMEM((2, tq, vw), F32),
        ],
    )
    return pl.pallas_call(
        functools.partial(_diff_kernel, lambda_init=lambda_init),
        out_shape=jax.ShapeDtypeStruct((s, DIFF_HEADS * vw), BF16),
        grid_spec=grid_spec,
        compiler_params=_cparams(("arbitrary", "arbitrary"), VMEM_LIMIT),
        name="diffattn",
    )(jnp.asarray(qi_tab), jnp.asarray(kj_tab), proj, proj, proj, lam_p, norm_g)


def _dil_kernel(q_ref, kp_ref, kc_ref, vp_ref, vc_ref, o_ref, lse_ref):
    n = pl.program_id(1)
    t = kp_ref.shape[0]
    qi = lax.broadcasted_iota(I32, (t, t), 0)
    kj = lax.broadcasted_iota(I32, (t, t), 1)
    in_span = kj >= qi
    keep_cur = kj <= qi
    dn = (((1,), (1,)), ((), ()))
    for b in range(q_ref.shape[0] // t):
        rows = slice(b * t, (b + 1) * t)
        before = slice((b - 1) * t, b * t)
        for h in range(DIL_HEADS_PER_GROUP):
            sl = slice(h * HEAD_DIM, (h + 1) * HEAD_DIM)
            q = q_ref[rows, sl]
            k_prev, v_prev = (kp_ref[:, sl], vp_ref[:, sl]) if b == 0 else (kc_ref[before, sl],
                                                                              vc_ref[before, sl])
            keep_prev = (in_span & (n > 0)) if b == 0 else in_span
            sp = lax.dot_general(q, k_prev, dn, preferred_element_type=F32)
            sc = lax.dot_general(q, kc_ref[rows, sl], dn, preferred_element_type=F32)
            sp = jnp.where(keep_prev, sp, NEG)
            sc = jnp.where(keep_cur, sc, NEG)
            m = jnp.maximum(jnp.max(sp, axis=-1, keepdims=True),
                            jnp.max(sc, axis=-1, keepdims=True))
            ep = jnp.exp(sp - m)
            ec = jnp.exp(sc - m)
            den = jnp.sum(ep, axis=-1, keepdims=True) + jnp.sum(ec, axis=-1, keepdims=True)
            acc = (jnp.dot(ep.astype(BF16), v_prev, preferred_element_type=F32)
                   + jnp.dot(ec.astype(BF16), vc_ref[rows, sl], preferred_element_type=F32))
            o_ref[rows, sl] = acc / den
            lse_ref[rows, sl] = jnp.broadcast_to(m + jnp.log(den), (t, HEAD_DIM))


def _dilated_group(qkv, g):
    dilation, l, _ = qkv.shape
    t = DIL_T
    nb = l // t
    per_step = DIL_BLOCKS_PER_STEP if nb % DIL_BLOCKS_PER_STEP == 0 else 1
    cur = lambda part: (lambda r, n: (r, n, part))
    prev = lambda part: (lambda r, n: (r, jnp.maximum(n * per_step - 1, 0), part))
    wide = (None, per_step * t, PROJ_TILE)
    one = (None, t, PROJ_TILE)
    return pl.pallas_call(
        _dil_kernel,
        out_shape=[jax.ShapeDtypeStruct((dilation, l, DIL_OUT), F32)] * 2,
        grid=(dilation, nb // per_step),
        in_specs=[
            pl.BlockSpec(wide, cur(0)),
            pl.BlockSpec(one, prev(1)),
            pl.BlockSpec(wide, cur(1)),
            pl.BlockSpec(one, prev(2)),
            pl.BlockSpec(wide, cur(2)),
        ],
        out_specs=[pl.BlockSpec((None, per_step * t, DIL_OUT), lambda r, n: (r, n, 0))] * 2,
        compiler_params=_cparams(("arbitrary", "arbitrary")),
        name=f"dilattn{g}",
    )(qkv, qkv, qkv, qkv, qkv)


def _layer_norm(z, g, b):
    mu = jnp.mean(z, axis=-1, keepdims=True)
    zc = z - mu
    var = jnp.mean(zc * zc, axis=-1, keepdims=True)
    return zc * lax.rsqrt(var + LN_EPS) * g + b


def _merge_kernel(o0, o1, o2, l0, l1, l2, yd_ref, ga_ref, gb_ref, x_ref, wbd_ref, wbl_ref, wo_ref,
                  rwt_ref, g_ref, b_ref, h_ref, lt_ref, scr, *, alpha):
    def positions(ref, slot):
        d = ref.shape[0]
        if d == 1:
            return ref[0]
        heads = scr.shape[1]
        for r in range(d):
            for c in range(heads):
                scr[slot, c, pl.ds(r, ref.shape[1], stride=d), :] = (
                    ref[r, :, c * HEAD_DIM:(c + 1) * HEAD_DIM])
        return jnp.concatenate([scr[slot, c] for c in range(heads)], axis=1)

    la, lb, lc = positions(l0, 0), positions(l1, 0), positions(l2, 1)
    m = jnp.maximum(jnp.maximum(la, lb), lc)
    ea, eb, ec = jnp.exp(la - m), jnp.exp(lb - m), jnp.exp(lc - m)
    num = ea * positions(o0, 0) + eb * positions(o1, 2) + ec * positions(o2, 3)
    ydil = num / (ea + eb + ec)
    a = jnp.dot(yd_ref[...], wbd_ref[...], preferred_element_type=F32)
    b = jnp.dot(ydil.astype(BF16), wbl_ref[...], preferred_element_type=F32)
    merged = ga_ref[...].astype(F32) * a + gb_ref[...].astype(F32) * b
    z = alpha * x_ref[...] + jnp.dot(merged.astype(BF16), wo_ref[...], preferred_element_type=F32)
    h = _layer_norm(z, g_ref[...], b_ref[...])
    h_ref[...] = h
    lt_ref[...] = lax.dot_general(rwt_ref[...], h.astype(BF16), (((1,), (1,)), ((), ())),
                                  preferred_element_type=F32)


def _merge(dil_o, dil_l, y_diff, gates, x, wbd, wbl, wo, rwt, ln_g, ln_b, alpha):
    s, d = x.shape
    tm = min(MERGE_TM, s)
    row = lambda w: pl.BlockSpec((tm, w), lambda i: (i, 0))
    const = lambda a: pl.BlockSpec(a.shape, lambda i: (0, 0), pipeline_mode=pl.Buffered(1))
    res = lambda a: pl.BlockSpec((a.shape[0], tm // a.shape[0], DIL_OUT), lambda i: (0, i, 0))
    in_specs = ([res(a) for a in dil_o] + [res(a) for a in dil_l] + [
        row(y_diff.shape[1]),
        pl.BlockSpec((tm, d), lambda i: (i, 0)),
        pl.BlockSpec((tm, d), lambda i: (i, 1)),
        row(d), const(wbd), const(wbl), const(wo), const(rwt), const(ln_g), const(ln_b)])
    return pl.pallas_call(
        functools.partial(_merge_kernel, alpha=alpha),
        out_shape=[jax.ShapeDtypeStruct((s, d), F32),
                   jax.ShapeDtypeStruct((N_EXPERTS, s), F32)],
        grid=(s // tm,),
        in_specs=in_specs,
        out_specs=[row(d), pl.BlockSpec((N_EXPERTS, tm), lambda i: (0, i))],
        scratch_shapes=[pltpu.VMEM((4, DIL_HEADS_PER_GROUP, tm, HEAD_DIM), F32)],
        compiler_params=_cparams(("arbitrary",), VMEM_LIMIT),
        name="merge",
    )(*dil_o, *dil_l, y_diff, gates, gates, x, wbd, wbl, wo, rwt, ln_g, ln_b)


def _route_kernel(lt_ref, bias_ref, e_ref, w_ref, r_ref, cnt_ref, carry):
    i = pl.program_id(0)
    tm = lt_ref.shape[1]
    ninf = -jnp.inf

    @pl.when(i == 0)
    def _():
        carry[...] = jnp.zeros(carry.shape, F32)

    sc = jax.nn.sigmoid(lt_ref[...])
    biased = sc + bias_ref[...]
    iog = lax.broadcasted_iota(I32, (GROUP_SIZE, tm), 0)
    blocks, gscore = [], []
    for g in range(N_GROUPS):
        blk = biased[g * GROUP_SIZE:(g + 1) * GROUP_SIZE, :]
        m1 = jnp.max(blk, axis=0, keepdims=True)
        i1 = jnp.min(jnp.where(blk == m1, iog, GROUP_SIZE), axis=0, keepdims=True)
        m2 = jnp.max(jnp.where(iog == i1, ninf, blk), axis=0, keepdims=True)
        blocks.append(blk)
        gscore.append(m1 + m2)
    masked = []
    for g in range(N_GROUPS):
        ahead = jnp.zeros((1, tm), F32)
        for o in range(N_GROUPS):
            if o == g:
                continue
            wins = (gscore[o] >= gscore[g]) if o < g else (gscore[o] > gscore[g])
            ahead = ahead + jnp.where(wins, 1.0, 0.0)
        keep = jnp.broadcast_to(ahead, (GROUP_SIZE, tm)) < TOPK_GROUPS
        masked.append(jnp.where(keep, blocks[g], ninf))
    v = jnp.concatenate(masked, axis=0)
    ioe = lax.broadcasted_iota(I32, (N_EXPERTS, tm), 0)
    idxs, ws = [], []
    sel = jnp.zeros((N_EXPERTS, tm), jnp.bool_)
    for _ in range(TOP_K):
        mx = jnp.max(v, axis=0, keepdims=True)
        idx = jnp.min(jnp.where(v == mx, ioe, N_EXPERTS), axis=0, keepdims=True)
        hit = ioe == idx
        idxs.append(idx)
        ws.append(jnp.sum(jnp.where(hit, sc, 0.0), axis=0, keepdims=True))
        sel = sel | hit
        v = jnp.where(hit, ninf, v)
    wsum = ws[0]
    for wk in ws[1:]:
        wsum = wsum + wk

    self = jnp.where(sel, 1.0, 0.0)
    ta = lax.broadcasted_iota(I32, (tm, tm), 0)
    tb = lax.broadcasted_iota(I32, (tm, tm), 1)
    before = jnp.where(ta < tb, 1.0, 0.0).astype(BF16)
    pos = jnp.dot(self.astype(BF16), before, preferred_element_type=F32) + carry[...]
    for k in range(TOP_K):
        e_ref[k:k + 1, :] = idxs[k]
        w_ref[k:k + 1, :] = ws[k] / wsum * ROUTED_SCALE
        rk = jnp.sum(jnp.where(ioe == idxs[k], pos, 0.0), axis=0, keepdims=True)
        r_ref[k:k + 1, :] = rk.astype(I32)
    total = carry[...] + jnp.sum(self, axis=1, keepdims=True)
    carry[...] = total
    cnt_ref[...] = total.astype(I32)


def _route(logits_t, bias_col):
    e, s = logits_t.shape
    tm = min(ROUTE_TM, s)
    tok = pl.BlockSpec((TOP_K, tm), lambda i: (0, i))
    return pl.pallas_call(
        _route_kernel,
        out_shape=[jax.ShapeDtypeStruct((TOP_K, s), I32),
                   jax.ShapeDtypeStruct((TOP_K, s), F32),
                   jax.ShapeDtypeStruct((TOP_K, s), I32),
                   jax.ShapeDtypeStruct((e, 1), I32)],
        grid=(s // tm,),
        in_specs=[pl.BlockSpec((e, tm), lambda i: (0, i)),
                  pl.BlockSpec((e, 1), lambda i: (0, 0))],
        out_specs=[tok, tok, tok, pl.BlockSpec((e, 1), lambda i: (0, 0))],
        scratch_shapes=[pltpu.VMEM((e, 1), F32)],
        compiler_params=_cparams(("arbitrary",)),
        name="route",
    )(logits_t, bias_col)


def _slots_kernel(e_ref, r_ref, ps_ref, s_ref):
    tm = e_ref.shape[1]
    ioe = lax.broadcasted_iota(I32, (N_EXPERTS, tm), 0)
    ps = ps_ref[...]
    for k in range(TOP_K):
        start = jnp.sum(jnp.where(ioe == e_ref[k:k + 1, :], ps, 0.0), axis=0, keepdims=True)
        s_ref[k:k + 1, :] = start.astype(I32) + r_ref[k:k + 1, :]


def _slots(eidx, rank, pstart_col):
    s = eidx.shape[1]
    tm = min(ROUTE_TM, s)
    tok = pl.BlockSpec((TOP_K, tm), lambda i: (0, i))
    return pl.pallas_call(
        _slots_kernel,
        out_shape=jax.ShapeDtypeStruct((TOP_K, s), I32),
        grid=(s // tm,),
        in_specs=[tok, tok, pl.BlockSpec((N_EXPERTS, 1), lambda i: (0, 0))],
        out_specs=tok,
        compiler_params=_cparams(("arbitrary",)),
        name="slots",
    )(eidx, rank, pstart_col)


def _row_pieces(length, fn):
    off = jnp.int32(0)
    size = EXP_CHUNK // 2
    while size >= SUBLANES:
        @pl.when((length & size) != 0)
        def _(off=off, size=size):
            fn(off, size)
        off = off + (length & size)
        size //= 2


def _fill_tail(end, n_rows, piece):
    tail = n_rows - end
    n_full = tail // EXP_CHUNK

    def full(t, carry):
        piece(pl.multiple_of(end + t * EXP_CHUNK, SUBLANES), EXP_CHUNK)
        return carry

    lax.fori_loop(0, n_full, full, 0)
    base = end + n_full * EXP_CHUNK
    _row_pieces(tail & (EXP_CHUNK - 1),
                lambda off, size: piece(pl.multiple_of(base + off, SUBLANES), size))


def _round_up(n, m):
    return (n + (m - 1)) // m * m


def _scatter_kernel(rstart_ref, cnt_ref, slot_ref, h_ref, xs_ref, zbuf, sem, zsem):
    i = pl.program_id(0)
    tm = h_ref.shape[0]
    n_exp = cnt_ref.shape[0]

    def zero_fill(do):
        def piece(row, size):
            do(pltpu.make_async_copy(zbuf.at[pl.ds(0, size)], xs_ref.at[pl.ds(row, size)], zsem))

        def expert(ex, carry):
            cnt = cnt_ref[ex]
            row = rstart_ref[ex] + cnt
            for j in range(SUBLANES - 1):
                @pl.when(j < ((-cnt) & (SUBLANES - 1)))
                def _(row=row, j=j):
                    piece(row + j, 1)
            return carry

        lax.fori_loop(0, n_exp, expert, 0)
        end = rstart_ref[n_exp - 1] + _round_up(cnt_ref[n_exp - 1], SUBLANES)
        _fill_tail(end, xs_ref.shape[0], piece)

    def issue(r, carry):
        for k in range(TOP_K):
            dst = slot_ref[r * TOP_K + k]
            pltpu.make_async_copy(h_ref.at[pl.ds(r, 1)], xs_ref.at[pl.ds(dst, 1)],
                                  sem).start(priority=k % 2)
        return carry

    lax.fori_loop(0, tm, issue, 0)
    for k in range(TOP_K):
        pltpu.make_async_copy(h_ref, xs_ref.at[pl.ds(0, tm)], sem).wait()

    @pl.when(i == pl.num_programs(0) - 1)
    def _():
        zbuf[...] = jnp.zeros(zbuf.shape, F32)
        zero_fill(lambda cp: cp.start())
        zero_fill(lambda cp: cp.wait())


def _scatter_rows(rstart, counts, slots_flat, h, n_rows):
    s, d = h.shape
    tm = min(ROW_TM, s)
    grid_spec = pltpu.PrefetchScalarGridSpec(
        num_scalar_prefetch=2,
        grid=(s // tm,),
        in_specs=[pl.BlockSpec((tm * TOP_K,), lambda i, ps, cn: (i,), memory_space=pltpu.SMEM),
                  pl.BlockSpec((tm, d), lambda i, ps, cn: (i, 0))],
        out_specs=pl.BlockSpec(memory_space=pl.ANY),
        scratch_shapes=[pltpu.VMEM((EXP_CHUNK, d), F32), pltpu.SemaphoreType.DMA,
                        pltpu.SemaphoreType.DMA],
    )
    return pl.pallas_call(
        _scatter_kernel,
        out_shape=jax.ShapeDtypeStruct((n_rows, d), F32),
        grid_spec=grid_spec,
        compiler_params=_cparams(("arbitrary",)),
        name="scatter",
    )(rstart, counts, slots_flat, h)


def _experts_kernel(chunk0_ref, cnt_ref, row0_ref, valid_ref, xs_ref, wg_ref, wu_ref, wd_ref,
                    ys_ref, wg_buf, wu_buf, wd_buf, wgu_sc, wd_sc, xbuf, ybuf, w_sem, in_sem,
                    out_sem):
    e = pl.program_id(0)
    n_exp = pl.num_programs(0)
    last = n_exp - 1
    ff = wg_ref.shape[2]
    chunks_of = lambda ex: (cnt_ref[ex] + (EXP_CHUNK - 1)) // EXP_CHUNK

    def weight_copies(ex):
        ws = ex % W_RING
        half = wd_ref.shape[1] // 2
        lo, hi = pl.ds(0, half), pl.ds(half, half)
        parts = ((wg_ref.at[ex], wg_buf.at[ws]), (wu_ref.at[ex], wu_buf.at[ws]),
                 (wd_ref.at[ex, lo], wd_buf.at[ws, lo]), (wd_ref.at[ex, hi], wd_buf.at[ws, hi]))
        return [(pltpu.make_async_copy(src, dst, w_sem.at[i, ws]), 0)
                for i, (src, dst) in enumerate(parts)]

    @pl.when(e == 0)
    def _():
        for ex in range(W_RING - 1):
            for cp, queue in weight_copies(ex):
                cp.start(priority=queue)

    @pl.when(e + (W_RING - 1) < n_exp)
    def _():
        for cp, queue in weight_copies(e + (W_RING - 1)):
            cp.start(priority=queue)

    for cp, _ in weight_copies(e):
        cp.wait()
    wslot = e % W_RING
    g0 = chunk0_ref[e]
    nch = chunks_of(e)
    total = chunk0_ref[last] + chunks_of(last)

    def move(g, n_rows, hbm, buf, sem, do):
        row0 = pl.multiple_of(row0_ref[g], SUBLANES)

        def piece(off, size):
            src = hbm.at[pl.ds(row0 + off, size)]
            dst = buf.at[pl.ds(off, size)]
            do(pltpu.make_async_copy(src, dst, sem) if hbm is xs_ref
               else pltpu.make_async_copy(dst, src, sem))

        @pl.when(n_rows == EXP_CHUNK)
        def _():
            piece(0, EXP_CHUNK)

        @pl.when(n_rows < EXP_CHUNK)
        def _():
            _row_pieces(n_rows, lambda off, size: piece(pl.multiple_of(off, SUBLANES), size))
            whole = n_rows & ~(SUBLANES - 1)
            for j in range(SUBLANES - 1):
                @pl.when(j < (n_rows & (SUBLANES - 1)))
                def _(j=j):
                    piece(whole + j, 1)

    def fetch(g, slot, do):
        move(g, valid_ref[g], xs_ref, xbuf.at[slot], in_sem.at[slot], do)

    def write_back(g, slot, do):
        move(g, _round_up(valid_ref[g], SUBLANES), ys_ref, ybuf.at[slot], out_sem.at[slot], do)

    def start(cp):
        cp.start(priority=1)

    def wait(cp):
        cp.wait()

    n_x, n_y = xbuf.shape[0], ybuf.shape[0]
    ahead = n_x - 1

    @pl.when(e == 0)
    def _():
        xbuf[...] = jnp.zeros(xbuf.shape, F32)
        for j in range(ahead):
            @pl.when(j < total)
            def _(j=j):
                fetch(j, j, start)

    @pl.when(nch > 0)
    def _():
        wgu_sc[:, :ff] = wg_buf[wslot].astype(BF16)
        wgu_sc[:, ff:] = wu_buf[wslot].astype(BF16)
        wd_sc[...] = wd_buf[wslot].astype(BF16)

        def chunk(c, carry):
            g = g0 + c
            slot = g % n_x
            fetch(g, slot, wait)

            @pl.when(g + ahead < total)
            def _():
                fetch(g + ahead, (g + ahead) % n_x, start)

            gu = jnp.dot(xbuf[slot].astype(BF16), wgu_sc[...], preferred_element_type=F32)
            hmid = jax.nn.silu(gu[:, :ff]) * gu[:, ff:]
            y = jnp.dot(hmid.astype(BF16), wd_sc[...], preferred_element_type=F32)

            yslot = g % n_y

            @pl.when(g >= n_y)
            def _():
                write_back(g - n_y, yslot, wait)

            ybuf[yslot] = y
            write_back(g, yslot, start)
            return carry

        lax.fori_loop(0, nch, chunk, 0)

    @pl.when(e == last)
    def _():
        for j in range(n_y):
            @pl.when(total > j)
            def _(j=j):
                write_back(total - 1 - j, (total - 1 - j) % n_y, wait)

        tail_g = jnp.maximum(total - 1, 0)
        end = jnp.where(total > 0, row0_ref[tail_g] + _round_up(valid_ref[tail_g], SUBLANES), 0)
        ybuf[0] = jnp.zeros(ybuf.shape[1:], F32)

        def zero_fill(do):
            def piece(row, size):
                do(pltpu.make_async_copy(ybuf.at[0, pl.ds(0, size)], ys_ref.at[pl.ds(row, size)],
                                         out_sem.at[0]))
            _fill_tail(end, ys_ref.shape[0], piece)

        zero_fill(start)
        zero_fill(wait)


def _experts(chunk0, counts, chunk_row0, chunk_valid, xs, wg, wu, wd):
    n_rows, d = xs.shape
    ne, _, ff = wg.shape
    grid_spec = pltpu.PrefetchScalarGridSpec(
        num_scalar_prefetch=4,
        grid=(ne,),
        in_specs=[
            pl.BlockSpec(memory_space=pl.ANY),
            pl.BlockSpec(memory_space=pl.ANY),
            pl.BlockSpec(memory_space=pl.ANY),
            pl.BlockSpec(memory_space=pl.ANY),
        ],
        out_specs=pl.BlockSpec(memory_space=pl.ANY),
        scratch_shapes=[
            pltpu.VMEM((W_RING, d, ff), F32),
            pltpu.VMEM((W_RING, d, ff), F32),
            pltpu.VMEM((W_RING, ff, d), F32),
            pltpu.VMEM((d, 2 * ff), BF16),
            pltpu.VMEM((ff, d), BF16),
            pltpu.VMEM((X_SLOTS, EXP_CHUNK, d), F32),
            pltpu.VMEM((Y_SLOTS, EXP_CHUNK, d), F32),
            pltpu.SemaphoreType.DMA((4, W_RING)),
            pltpu.SemaphoreType.DMA((X_SLOTS,)),
            pltpu.SemaphoreType.DMA((Y_SLOTS,)),
        ],
    )
    return pl.pallas_call(
        _experts_kernel,
        out_shape=jax.ShapeDtypeStruct((n_rows, d), F32),
        grid_spec=grid_spec,
        compiler_params=_cparams(("arbitrary",), VMEM_LIMIT),
        name="experts",
    )(chunk0, counts, chunk_row0, chunk_valid, xs, wg, wu, wd)


def _combine_kernel(slot_ref, ys_ref, wt_ref, h_ref, wgu_ref, wd_ref, g_ref, b_ref, o_ref,
                    gbuf, sem, *, alpha):
    tm = h_ref.shape[0]
    ff = wd_ref.shape[0]

    def issue(r, carry):
        for k in range(TOP_K):
            src = slot_ref[r * TOP_K + k]
            pltpu.make_async_copy(ys_ref.at[pl.ds(src, 1)], gbuf.at[k, pl.ds(r, 1)],
                                  sem).start(priority=k % 2)
        return carry

    lax.fori_loop(0, tm, issue, 0)

    h = h_ref[...]
    gu = jnp.dot(h.astype(BF16), wgu_ref[...], preferred_element_type=F32)
    hmid = jax.nn.silu(gu[:, :ff]) * gu[:, ff:]
    acc = jnp.dot(hmid.astype(BF16), wd_ref[...], preferred_element_type=F32)

    for k in range(TOP_K):
        pltpu.make_async_copy(ys_ref.at[pl.ds(0, tm)], gbuf.at[k], sem).wait()
    wt = wt_ref[...]
    for k in range(TOP_K):
        acc = acc + wt[:, k:k + 1] * gbuf[k]
    o_ref[...] = _layer_norm(alpha * h + acc, g_ref[...], b_ref[...])


def _combine(slots_flat, ys, wt, h, wgu_s, wd_s, ln_g, ln_b, alpha):
    s, d = h.shape
    tm = min(ROW_TM, s)
    row = lambda c: pl.BlockSpec((tm, c), lambda i: (i, 0))
    const = lambda a: pl.BlockSpec(a.shape, lambda i: (0, 0), pipeline_mode=pl.Buffered(1))
    return pl.pallas_call(
        functools.partial(_combine_kernel, alpha=alpha),
        out_shape=jax.ShapeDtypeStruct((s, d), F32),
        grid=(s // tm,),
        in_specs=[pl.BlockSpec((tm * TOP_K,), lambda i: (i,), memory_space=pltpu.SMEM),
                  pl.BlockSpec(memory_space=pl.ANY),
                  row(TOP_K), row(d), const(wgu_s), const(wd_s), const(ln_g), const(ln_b)],
        out_specs=row(d),
        scratch_shapes=[pltpu.VMEM((TOP_K, tm, d), F32), pltpu.SemaphoreType.DMA],
        compiler_params=_cparams(("arbitrary",), VMEM_LIMIT),
        name="combine",
    )(slots_flat, ys, wt, h, wgu_s, wd_s, ln_g, ln_b)


def _rope_tables(s):
    half = HEAD_DIM // 2
    inv = ROPE_THETA ** (-jnp.arange(half, dtype=F32) / half)
    ang = jnp.arange(s).astype(F32)[:, None] * inv[None, :]
    cos, sin = jnp.cos(ang), jnp.sin(ang)
    cos2, sin2 = jnp.concatenate([cos, cos], axis=1), jnp.concatenate([-sin, sin], axis=1)
    return jnp.stack([cos2, jnp.ones_like(cos2)]), jnp.stack([sin2, jnp.zeros_like(sin2)])


def _layer(x, l, depth, w_in, b_gate, lambda_q1, lambda_k1, lambda_q2, lambda_k2, diff_norm_g,
           w_branch_diff, w_branch_dil, w_out, ln1_g, ln1_b, router_w, router_bias,
           w_gate_e, w_up_e, w_down_e, w_gate_s, w_up_s, w_down_s, ln2_g, ln2_b):
    s, d = x.shape
    alpha = (2.0 * depth) ** 0.25
    lambda_init = 0.8 - 0.6 * math.exp(-0.3 * l)
    cos, sin = _rope_tables(s)
    x_bf = x.astype(BF16)
    w_bf = w_in[l].astype(BF16)

    gates = _inproj_gates(x_bf, w_bf, b_gate[l].reshape(1, 2 * d))
    qkv = _inproj_qkv(x_bf, w_bf, cos, sin)
    lam_p = jnp.stack([lambda_q1[l], lambda_k1[l], lambda_q2[l], lambda_k2[l]]).astype(F32)
    y_diff = _diff_attention(qkv, lam_p, diff_norm_g[l].reshape(1, -1), lambda_init)

    dil = [_dilated_group(_inproj_dil(x_bf, w_bf, cos, sin, g, dilation), g)
           for g, (_, dilation) in enumerate(DIL_CONFIGS)]
    h, logits_t = _merge(
        [o for o, _ in dil], [ls for _, ls in dil], y_diff, gates, x,
        w_branch_diff[l].astype(BF16), w_branch_dil[l].astype(BF16), w_out[l].astype(BF16),
        router_w[l].T.astype(BF16), ln1_g[l].reshape(1, d), ln1_b[l].reshape(1, d), alpha)

    eidx, wts, rank, counts = _route(logits_t, router_bias[l].reshape(N_EXPERTS, 1).astype(F32))
    counts = counts.reshape(N_EXPERTS)
    padded = _round_up(counts, SUBLANES)
    rstart = (jnp.cumsum(padded) - padded).astype(I32)
    n_chunks = (counts + (EXP_CHUNK - 1)) // EXP_CHUNK
    chunk0 = (jnp.cumsum(n_chunks) - n_chunks).astype(I32)
    max_chunks = s * TOP_K // EXP_CHUNK + N_EXPERTS
    cid = jnp.arange(max_chunks, dtype=I32)
    chunk_end = chunk0 + n_chunks
    chunk_expert = jnp.minimum(jnp.sum(chunk_end[None, :] <= cid[:, None], axis=1), N_EXPERTS - 1)
    local = cid - chunk0[chunk_expert]
    chunk_valid = jnp.clip(counts[chunk_expert] - local * EXP_CHUNK, 0, EXP_CHUNK).astype(I32)
    chunk_row0 = (rstart[chunk_expert] + local * EXP_CHUNK).astype(I32)
    n_rows = s * TOP_K + N_EXPERTS * SUBLANES

    slots = _slots(eidx, rank, rstart.astype(F32).reshape(N_EXPERTS, 1))
    slots_flat = slots.T.reshape(-1)
    xs = _scatter_rows(rstart, counts, slots_flat, h, n_rows)
    ys = _experts(chunk0, counts, chunk_row0, chunk_valid, xs,
                  w_gate_e[l], w_up_e[l], w_down_e[l])
    wgu_s = jnp.concatenate([w_gate_s[l], w_up_s[l]], axis=1).astype(BF16)
    return _combine(slots_flat, ys, wts.T, h, wgu_s, w_down_s[l].astype(BF16),
                    ln2_g[l].reshape(1, d), ln2_b[l].reshape(1, d), alpha)


def kernel(x, w_in, b_gate, lambda_q1, lambda_k1, lambda_q2, lambda_k2, diff_norm_g, w_branch_diff,
           w_branch_dil, w_out, ln1_g, ln1_b, router_w, router_bias, w_gate_e, w_up_e, w_down_e,
           w_gate_s, w_up_s, w_down_s, ln2_g, ln2_b):
    b, s, d = x.shape
    depth = w_in.shape[0]
    outs = []
    for bi in range(b):
        xb = x[bi]
        for l in range(depth):
            xb = _layer(xb, l, depth, w_in, b_gate, lambda_q1, lambda_k1, lambda_q2, lambda_k2,
                        diff_norm_g, w_branch_diff, w_branch_dil, w_out, ln1_g, ln1_b, router_w,
                        router_bias, w_gate_e, w_up_e, w_down_e, w_gate_s, w_up_s, w_down_s,
                        ln2_g, ln2_b)
        outs.append(xb)
    return jnp.stack(outs)
```

```python
import functools
import math

import jax
import jax.numpy as jnp
import numpy as np
from jax import lax
from jax.experimental import pallas as pl
from jax.experimental.pallas import tpu as pltpu

F32 = jnp.float32
BF16 = jnp.bfloat16
I32 = jnp.int32

HEAD_DIM = 128
ROPE_THETA = 10000.0
LN_EPS = 1e-5
DIFF_HEADS = 4
DIL_CONFIGS = ((128, 1), (512, 4), (2048, 16))
DIL_HEADS_PER_GROUP = 4
DIL_OUT = DIL_HEADS_PER_GROUP * HEAD_DIM
N_EXPERTS = 256
TOP_K = 8
N_GROUPS = 8
TOPK_GROUPS = 4
GROUP_SIZE = N_EXPERTS // N_GROUPS
ROUTED_SCALE = 2.5
ATTN_SCALE = HEAD_DIM ** -0.5
LOG2E = math.log2(math.e)

PROJ_TILE = 512
N_QKV_TILES = 6
DIL_SRC_TILE0 = 6
GATE_SRC_TILE0 = 15
N_GATE_TILES = 8

INPROJ_TM = 1024
ATTN_TQ = 1024
ATTN_TK = 1024
LANES = 128
DIL_T = 128
DIL_BLOCKS_PER_STEP = 4
MERGE_TM = 256
ROUTE_TM = 512
ROW_TM = 256
EXP_CHUNK = 128
W_RING = 2
X_SLOTS = 10
Y_SLOTS = 6
SUBLANES = 8
NEG = -1e30
VMEM_LIMIT = 56 * 1024 * 1024


def _cparams(sem, vmem=None):
    return pltpu.CompilerParams(dimension_semantics=sem, vmem_limit_bytes=vmem)


def _rope(a, cos, sin):
    outs = []
    for h in range(a.shape[1] // HEAD_DIM):
        ah = a[:, h * HEAD_DIM:(h + 1) * HEAD_DIM]
        outs.append(ah * cos + pltpu.roll(ah, HEAD_DIM // 2, 1) * sin)
    return jnp.concatenate(outs, axis=1)


def _gates_kernel(x_ref, w_ref, b_ref, o_ref):
    acc = jnp.dot(x_ref[...], w_ref[...], preferred_element_type=F32)
    o_ref[...] = jax.nn.sigmoid(acc + b_ref[...]).astype(BF16)


def _inproj_gates(x_bf, w_bf, b_gate):
    s, d = x_bf.shape
    tm = min(INPROJ_TM, s)
    return pl.pallas_call(
        _gates_kernel,
        out_shape=jax.ShapeDtypeStruct((s, N_GATE_TILES * PROJ_TILE), BF16),
        grid=(s // tm, N_GATE_TILES),
        in_specs=[
            pl.BlockSpec((tm, d), lambda i, j: (i, 0)),
            pl.BlockSpec((d, PROJ_TILE), lambda i, j: (0, GATE_SRC_TILE0 + j)),
            pl.BlockSpec((1, PROJ_TILE), lambda i, j: (0, j)),
        ],
        out_specs=pl.BlockSpec((tm, PROJ_TILE), lambda i, j: (i, j)),
        compiler_params=_cparams(("arbitrary", "arbitrary"), VMEM_LIMIT),
        name="inproj_gates",
    )(x_bf, w_bf, b_gate)


def _qkv_kernel(x_ref, w_ref, cos_ref, sin_ref, o_ref):
    j = pl.program_id(1)
    acc = jnp.dot(x_ref[...], w_ref[...], preferred_element_type=F32)
    scale = jnp.where(j < 2, ATTN_SCALE * LOG2E, 1.0).astype(F32)
    o_ref[...] = _rope(acc, cos_ref[...] * scale, sin_ref[...] * scale).astype(BF16)


def _inproj_qkv(x_bf, w_bf, cos_tab, sin_tab):
    s, d = x_bf.shape
    tm = min(INPROJ_TM, s)
    tab = pl.BlockSpec((None, tm, HEAD_DIM), lambda i, j: ((j >= 4).astype(I32), i, 0))
    return pl.pallas_call(
        _qkv_kernel,
        out_shape=jax.ShapeDtypeStruct((s, N_QKV_TILES * PROJ_TILE), BF16),
        grid=(s // tm, N_QKV_TILES),
        in_specs=[
            pl.BlockSpec((tm, d), lambda i, j: (i, 0)),
            pl.BlockSpec((d, PROJ_TILE), lambda i, j: (0, j)),
            tab, tab,
        ],
        out_specs=pl.BlockSpec((tm, PROJ_TILE), lambda i, j: (i, j)),
        compiler_params=_cparams(("arbitrary", "arbitrary"), VMEM_LIMIT),
        name="inproj_qkv",
    )(x_bf, w_bf, cos_tab, sin_tab)


def _inproj_dil_kernel(x_ref, w_ref, cos_ref, sin_ref, o_ref, scr, *, dilation):
    part = pl.program_id(1)
    acc = jnp.dot(x_ref[...], w_ref[...], preferred_element_type=F32)
    scale = jnp.where(part == 0, ATTN_SCALE, 1.0).astype(F32)
    val = _rope(acc, cos_ref[...] * scale, sin_ref[...] * scale)
    if dilation == 1:
        o_ref[0] = val.astype(BF16)
        return
    for c in range(scr.shape[0]):
        scr[c] = val[:, c * HEAD_DIM:(c + 1) * HEAD_DIM]
    n = scr.shape[1] // dilation
    for r in range(dilation):
        for c in range(scr.shape[0]):
            o_ref[r, :, c * HEAD_DIM:(c + 1) * HEAD_DIM] = (
                scr[c, pl.ds(r, n, stride=dilation), :].astype(BF16))


def _inproj_dil(x_bf, w_bf, cos_tab, sin_tab, g, dilation):
    s, d = x_bf.shape
    tm = min(INPROJ_TM, s)
    n = tm // dilation
    tab = pl.BlockSpec((None, tm, HEAD_DIM), lambda i, p: ((p == 2).astype(I32), i, 0))
    return pl.pallas_call(
        functools.partial(_inproj_dil_kernel, dilation=dilation),
        out_shape=jax.ShapeDtypeStruct((dilation, s // dilation, 3 * PROJ_TILE), BF16),
        grid=(s // tm, 3),
        in_specs=[
            pl.BlockSpec((tm, d), lambda i, p: (i, 0)),
            pl.BlockSpec((d, PROJ_TILE), lambda i, p: (0, DIL_SRC_TILE0 + g + 3 * p)),
            tab, tab,
        ],
        out_specs=pl.BlockSpec((dilation, n, PROJ_TILE), lambda i, p: (0, i, p)),
        scratch_shapes=[pltpu.VMEM((PROJ_TILE // HEAD_DIM, tm, HEAD_DIM), F32)],
        compiler_params=_cparams(("arbitrary", "arbitrary"), VMEM_LIMIT),
        name=f"inproj_dil{g}",
    )(x_bf, w_bf, cos_tab, sin_tab)


def _diff_kernel(qi_ref, kj_ref, q_ref, k_ref, v_ref, lam_ref, g_ref, o_ref, m_sc, l_sc, acc_sc,
                 *, lambda_init):
    step = pl.program_id(1)
    qi = qi_ref[step]
    kj = kj_ref[step]
    tq, tk = q_ref.shape[0], k_ref.shape[0]
    last_kj = qi // (tk // tq)

    @pl.when(kj == 0)
    def _():
        m_sc[...] = jnp.full(m_sc.shape, NEG, F32)
        l_sc[...] = jnp.zeros(l_sc.shape, F32)
        acc_sc[...] = jnp.zeros(acc_sc.shape, F32)

    def update(rows, keys, masked):
        nr, nk = rows.stop - rows.start, keys.stop - keys.start
        v = v_ref[keys, :]
        if masked:
            row = qi * tq + rows.start + lax.broadcasted_iota(I32, (nr, nk), 0)
            col = kj * tk + keys.start + lax.broadcasted_iota(I32, (nr, nk), 1)
            keep = col <= row
        for mm in range(2):
            q = q_ref[rows, mm * HEAD_DIM:(mm + 1) * HEAD_DIM]
            k = k_ref[keys, mm * HEAD_DIM:(mm + 1) * HEAD_DIM]
            s = lax.dot_general(q, k, (((1,), (1,)), ((), ())), preferred_element_type=F32)
            if masked:
                s = jnp.where(keep, s, NEG)
            m_prev = m_sc[mm, rows]
            m_new = jnp.maximum(m_prev, jnp.max(s, axis=-1, keepdims=True))
            alpha = jnp.exp2(m_prev - m_new)
            p = jnp.exp2(s - jnp.tile(m_new, (1, nk // LANES)))
            l_sc[mm, rows] = alpha * l_sc[mm, rows] + jnp.sum(p, axis=-1, keepdims=True)
            pv = jnp.dot(p.astype(BF16), v, preferred_element_type=F32)
            acc_sc[mm, rows] = jnp.tile(alpha, (1, v.shape[1] // LANES)) * acc_sc[mm, rows] + pv
            m_sc[mm, rows] = m_new

    @pl.when(kj < last_kj)
    def _():
        update(slice(0, tq), slice(0, tk), False)

    @pl.when(kj == last_kj)
    def _():
        if tq == tk:
            half = tq // 2
            update(slice(0, half), slice(0, half), True)
            update(slice(half, tq), slice(0, tk), True)
        else:
            update(slice(0, tq), slice(0, tk), True)
        lam_p = lam_ref[...]
        lam = (jnp.exp(jnp.sum(lam_p[0:1] * lam_p[1:2], axis=-1, keepdims=True))
               - jnp.exp(jnp.sum(lam_p[2:3] * lam_p[3:4], axis=-1, keepdims=True))
               + lambda_init)
        rep = acc_sc.shape[2] // LANES
        o = (acc_sc[0] / jnp.tile(l_sc[0], (1, rep))
             - lam * (acc_sc[1] / jnp.tile(l_sc[1], (1, rep))))
        o = o * lax.rsqrt(jnp.mean(o * o, axis=-1, keepdims=True) + LN_EPS) * g_ref[...]
        o_ref[...] = (o * (1.0 - lambda_init)).astype(o_ref.dtype)


def _diff_attention(proj, lam_p, norm_g, lambda_init):
    s = proj.shape[0]
    tq = min(ATTN_TQ, s)
    tk = min(ATTN_TK, s)
    pairs = [(i, j) for i in range(s // tq) for j in range(i * tq // tk + 1)]
    qi_tab = np.asarray([p[0] for p in pairs], np.int32)
    kj_tab = np.asarray([p[1] for p in pairs], np.int32)
    vw = 2 * HEAD_DIM
    grid_spec = pltpu.PrefetchScalarGridSpec(
        num_scalar_prefetch=2,
        grid=(DIFF_HEADS, len(pairs)),
        in_specs=[
            pl.BlockSpec((tq, vw), lambda h, st, qi, kj: (qi[st], h)),
            pl.BlockSpec((tk, vw), lambda h, st, qi, kj: (kj[st], DIFF_HEADS + h)),
            pl.BlockSpec((tk, vw), lambda h, st, qi, kj: (kj[st], 2 * DIFF_HEADS + h)),
            pl.BlockSpec((4, HEAD_DIM), lambda h, st, qi, kj: (0, 0)),
            pl.BlockSpec((1, vw), lambda h, st, qi, kj: (0, 0)),
        ],
        out_specs=pl.BlockSpec((tq, vw), lambda h, st, qi, kj: (qi[st], h)),
        scratch_shapes=[
            pltpu.VMEM((2, tq, LANES), F32),
            pltpu.VMEM((2, tq, LANES), F32),
            pltpu.VMEM((2, tq, vw), F32),
        ],
    )
    return pl.pallas_call(
        functools.partial(_diff_kernel, lambda_init=lambda_init),
        out_shape=jax.ShapeDtypeStruct((s, DIFF_HEADS * vw), BF16),
        grid_spec=grid_spec,
        compiler_params=_cparams(("arbitrary", "arbitrary"), VMEM_LIMIT),
        name="diffattn",
    )(jnp.asarray(qi_tab), jnp.asarray(kj_tab), proj, proj, proj, lam_p, norm_g)


def _dil_kernel(q_ref, kp_ref, kc_ref, vp_ref, vc_ref, o_ref, lse_ref):
    n = pl.program_id(1)
    t = kp_ref.shape[0]
    qi = lax.broadcasted_iota(I32, (t, t), 0)
    kj = lax.broadcasted_iota(I32, (t, t), 1)
    in_span = kj >= qi
    keep_cur = kj <= qi
    dn = (((1,), (1,)), ((), ()))
    for b in range(q_ref.shape[0] // t):
        rows = slice(b * t, (b + 1) * t)
        before = slice((b - 1) * t, b * t)
        for h in range(DIL_HEADS_PER_GROUP):
            sl = slice(h * HEAD_DIM, (h + 1) * HEAD_DIM)
            q = q_ref[rows, sl]
            k_prev, v_prev = (kp_ref[:, sl], vp_ref[:, sl]) if b == 0 else (kc_ref[before, sl],
                                                                              vc_ref[before, sl])
            keep_prev = (in_span & (n > 0)) if b == 0 else in_span
            sp = lax.dot_general(q, k_prev, dn, preferred_element_type=F32)
            sc = lax.dot_general(q, kc_ref[rows, sl], dn, preferred_element_type=F32)
            sp = jnp.where(keep_prev, sp, NEG)
            sc = jnp.where(keep_cur, sc, NEG)
            m = jnp.maximum(jnp.max(sp, axis=-1, keepdims=True),
                            jnp.max(sc, axis=-1, keepdims=True))
            ep = jnp.exp(sp - m)
            ec = jnp.exp(sc - m)
            den = jnp.sum(ep, axis=-1, keepdims=True) + jnp.sum(ec, axis=-1, keepdims=True)
            acc = (jnp.dot(ep.astype(BF16), v_prev, preferred_element_type=F32)
                   + jnp.dot(ec.astype(BF16), vc_ref[rows, sl], preferred_element_type=F32))
            o_ref[rows, sl] = acc / den
            lse_ref[rows, sl] = jnp.broadcast_to(m + jnp.log(den), (t, HEAD_DIM))


def _dilated_group(qkv, g):
    dilation, l, _ = qkv.shape
    t = DIL_T
    nb = l // t
    per_step = DIL_BLOCKS_PER_STEP if nb % DIL_BLOCKS_PER_STEP == 0 else 1
    cur = lambda part: (lambda r, n: (r, n, part))
    prev = lambda part: (lambda r, n: (r, jnp.maximum(n * per_step - 1, 0), part))
    wide = (None, per_step * t, PROJ_TILE)
    one = (None, t, PROJ_TILE)
    return pl.pallas_call(
        _dil_kernel,
        out_shape=[jax.ShapeDtypeStruct((dilation, l, DIL_OUT), F32)] * 2,
        grid=(dilation, nb // per_step),
        in_specs=[
            pl.BlockSpec(wide, cur(0)),
            pl.BlockSpec(one, prev(1)),
            pl.BlockSpec(wide, cur(1)),
            pl.BlockSpec(one, prev(2)),
            pl.BlockSpec(wide, cur(2)),
        ],
        out_specs=[pl.BlockSpec((None, per_step * t, DIL_OUT), lambda r, n: (r, n, 0))] * 2,
        compiler_params=_cparams(("arbitrary", "arbitrary")),
        name=f"dilattn{g}",
    )(qkv, qkv, qkv, qkv, qkv)


def _layer_norm(z, g, b):
    mu = jnp.mean(z, axis=-1, keepdims=True)
    zc = z - mu
    var = jnp.mean(zc * zc, axis=-1, keepdims=True)
    return zc * lax.rsqrt(var + LN_EPS) * g + b


def _merge_kernel(o0, o1, o2, l0, l1, l2, yd_ref, ga_ref, gb_ref, x_ref, wbd_ref, wbl_ref, wo_ref,
                  rwt_ref, g_ref, b_ref, h_ref, lt_ref, scr, *, alpha):
    def positions(ref, slot):
        d = ref.shape[0]
        if d == 1:
            return ref[0]
        heads = scr.shape[1]
        for r in range(d):
            for c in range(heads):
                scr[slot, c, pl.ds(r, ref.shape[1], stride=d), :] = (
                    ref[r, :, c * HEAD_DIM:(c + 1) * HEAD_DIM])
        return jnp.concatenate([scr[slot, c] for c in range(heads)], axis=1)

    la, lb, lc = positions(l0, 0), positions(l1, 0), positions(l2, 1)
    m = jnp.maximum(jnp.maximum(la, lb), lc)
    ea, eb, ec = jnp.exp(la - m), jnp.exp(lb - m), jnp.exp(lc - m)
    num = ea * positions(o0, 0) + eb * positions(o1, 2) + ec * positions(o2, 3)
    ydil = num / (ea + eb + ec)
    a = jnp.dot(yd_ref[...], wbd_ref[...], preferred_element_type=F32)
    b = jnp.dot(ydil.astype(BF16), wbl_ref[...], preferred_element_type=F32)
    merged = ga_ref[...].astype(F32) * a + gb_ref[...].astype(F32) * b
    z = alpha * x_ref[...] + jnp.dot(merged.astype(BF16), wo_ref[...], preferred_element_type=F32)
    h = _layer_norm(z, g_ref[...], b_ref[...])
    h_ref[...] = h
    lt_ref[...] = lax.dot_general(rwt_ref[...], h.astype(BF16), (((1,), (1,)), ((), ())),
                                  preferred_element_type=F32)


def _merge(dil_o, dil_l, y_diff, gates, x, wbd, wbl, wo, rwt, ln_g, ln_b, alpha):
    s, d = x.shape
    tm = min(MERGE_TM, s)
    row = lambda w: pl.BlockSpec((tm, w), lambda i: (i, 0))
    const = lambda a: pl.BlockSpec(a.shape, lambda i: (0, 0), pipeline_mode=pl.Buffered(1))
    res = lambda a: pl.BlockSpec((a.shape[0], tm // a.shape[0], DIL_OUT), lambda i: (0, i, 0))
    in_specs = ([res(a) for a in dil_o] + [res(a) for a in dil_l] + [
        row(y_diff.shape[1]),
        pl.BlockSpec((tm, d), lambda i: (i, 0)),
        pl.BlockSpec((tm, d), lambda i: (i, 1)),
        row(d), const(wbd), const(wbl), const(wo), const(rwt), const(ln_g), const(ln_b)])
    return pl.pallas_call(
        functools.partial(_merge_kernel, alpha=alpha),
        out_shape=[jax.ShapeDtypeStruct((s, d), F32),
                   jax.ShapeDtypeStruct((N_EXPERTS, s), F32)],
        grid=(s // tm,),
        in_specs=in_specs,
        out_specs=[row(d), pl.BlockSpec((N_EXPERTS, tm), lambda i: (0, i))],
        scratch_shapes=[pltpu.VMEM((4, DIL_HEADS_PER_GROUP, tm, HEAD_DIM), F32)],
        compiler_params=_cparams(("arbitrary",), VMEM_LIMIT),
        name="merge",
    )(*dil_o, *dil_l, y_diff, gates, gates, x, wbd, wbl, wo, rwt, ln_g, ln_b)


def _route_kernel(lt_ref, bias_ref, e_ref, w_ref, r_ref, cnt_ref, carry):
    i = pl.program_id(0)
    tm = lt_ref.shape[1]
    ninf = -jnp.inf

    @pl.when(i == 0)
    def _():
        carry[...] = jnp.zeros(carry.shape, F32)

    sc = jax.nn.sigmoid(lt_ref[...])
    biased = sc + bias_ref[...]
    iog = lax.broadcasted_iota(I32, (GROUP_SIZE, tm), 0)
    blocks, gscore = [], []
    for g in range(N_GROUPS):
        blk = biased[g * GROUP_SIZE:(g + 1) * GROUP_SIZE, :]
        m1 = jnp.max(blk, axis=0, keepdims=True)
        i1 = jnp.min(jnp.where(blk == m1, iog, GROUP_SIZE), axis=0, keepdims=True)
        m2 = jnp.max(jnp.where(iog == i1, ninf, blk), axis=0, keepdims=True)
        blocks.append(blk)
        gscore.append(m1 + m2)
    masked = []
    for g in range(N_GROUPS):
        ahead = jnp.zeros((1, tm), F32)
        for o in range(N_GROUPS):
            if o == g:
                continue
            wins = (gscore[o] >= gscore[g]) if o < g else (gscore[o] > gscore[g])
            ahead = ahead + jnp.where(wins, 1.0, 0.0)
        keep = jnp.broadcast_to(ahead, (GROUP_SIZE, tm)) < TOPK_GROUPS
        masked.append(jnp.where(keep, blocks[g], ninf))
    v = jnp.concatenate(masked, axis=0)
    ioe = lax.broadcasted_iota(I32, (N_EXPERTS, tm), 0)
    idxs, ws = [], []
    sel = jnp.zeros((N_EXPERTS, tm), jnp.bool_)
    for _ in range(TOP_K):
        mx = jnp.max(v, axis=0, keepdims=True)
        idx = jnp.min(jnp.where(v == mx, ioe, N_EXPERTS), axis=0, keepdims=True)
        hit = ioe == idx
        idxs.append(idx)
        ws.append(jnp.sum(jnp.where(hit, sc, 0.0), axis=0, keepdims=True))
        sel = sel | hit
        v = jnp.where(hit, ninf, v)
    wsum = ws[0]
    for wk in ws[1:]:
        wsum = wsum + wk

    self = jnp.where(sel, 1.0, 0.0)
    ta = lax.broadcasted_iota(I32, (tm, tm), 0)
    tb = lax.broadcasted_iota(I32, (tm, tm), 1)
    before = jnp.where(ta < tb, 1.0, 0.0).astype(BF16)
    pos = jnp.dot(self.astype(BF16), before, preferred_element_type=F32) + carry[...]
    for k in range(TOP_K):
        e_ref[k:k + 1, :] = idxs[k]
        w_ref[k:k + 1, :] = ws[k] / wsum * ROUTED_SCALE
        rk = jnp.sum(jnp.where(ioe == idxs[k], pos, 0.0), axis=0, keepdims=True)
        r_ref[k:k + 1, :] = rk.astype(I32)
    total = carry[...] + jnp.sum(self, axis=1, keepdims=True)
    carry[...] = total
    cnt_ref[...] = total.astype(I32)


def _route(logits_t, bias_col):
    e, s = logits_t.shape
    tm = min(ROUTE_TM, s)
    tok = pl.BlockSpec((TOP_K, tm), lambda i: (0, i))
    return pl.pallas_call(
        _route_kernel,
        out_shape=[jax.ShapeDtypeStruct((TOP_K, s), I32),
                   jax.ShapeDtypeStruct((TOP_K, s), F32),
                   jax.ShapeDtypeStruct((TOP_K, s), I32),
                   jax.ShapeDtypeStruct((e, 1), I32)],
        grid=(s // tm,),
        in_specs=[pl.BlockSpec((e, tm), lambda i: (0, i)),
                  pl.BlockSpec((e, 1), lambda i: (0, 0))],
        out_specs=[tok, tok, tok, pl.BlockSpec((e, 1), lambda i: (0, 0))],
        scratch_shapes=[pltpu.VMEM((e, 1), F32)],
        compiler_params=_cparams(("arbitrary",)),
        name="route",
    )(logits_t, bias_col)


def _slots_kernel(e_ref, r_ref, ps_ref, s_ref):
    tm = e_ref.shape[1]
    ioe = lax.broadcasted_iota(I32, (N_EXPERTS, tm), 0)
    ps = ps_ref[...]
    for k in range(TOP_K):
        start = jnp.sum(jnp.where(ioe == e_ref[k:k + 1, :], ps, 0.0), axis=0, keepdims=True)
        s_ref[k:k + 1, :] = start.astype(I32) + r_ref[k:k + 1, :]


def _slots(eidx, rank, pstart_col):
    s = eidx.shape[1]
    tm = min(ROUTE_TM, s)
    tok = pl.BlockSpec((TOP_K, tm), lambda i: (0, i))
    return pl.pallas_call(
        _slots_kernel,
        out_shape=jax.ShapeDtypeStruct((TOP_K, s), I32),
        grid=(s // tm,),
        in_specs=[tok, tok, pl.BlockSpec((N_EXPERTS, 1), lambda i: (0, 0))],
        out_specs=tok,
        compiler_params=_cparams(("arbitrary",)),
        name="slots",
    )(eidx, rank, pstart_col)


def _row_pieces(length, fn):
    off = jnp.int32(0)
    size = EXP_CHUNK // 2
    while size >= SUBLANES:
        @pl.when((length & size) != 0)
        def _(off=off, size=size):
            fn(off, size)
        off = off + (length & size)
        size //= 2


def _fill_tail(end, n_rows, piece):
    tail = n_rows - end
    n_full = tail // EXP_CHUNK

    def full(t, carry):
        piece(pl.multiple_of(end + t * EXP_CHUNK, SUBLANES), EXP_CHUNK)
        return carry

    lax.fori_loop(0, n_full, full, 0)
    base = end + n_full * EXP_CHUNK
    _row_pieces(tail & (EXP_CHUNK - 1),
                lambda off, size: piece(pl.multiple_of(base + off, SUBLANES), size))


def _round_up(n, m):
    return (n + (m - 1)) // m * m


def _scatter_kernel(rstart_ref, cnt_ref, slot_ref, h_ref, xs_ref, zbuf, sem, zsem):
    i = pl.program_id(0)
    tm = h_ref.shape[0]
    n_exp = cnt_ref.shape[0]

    def zero_fill(do):
        def piece(row, size):
            do(pltpu.make_async_copy(zbuf.at[pl.ds(0, size)], xs_ref.at[pl.ds(row, size)], zsem))

        def expert(ex, carry):
            cnt = cnt_ref[ex]
            row = rstart_ref[ex] + cnt
            for j in range(SUBLANES - 1):
                @pl.when(j < ((-cnt) & (SUBLANES - 1)))
                def _(row=row, j=j):
                    piece(row + j, 1)
            return carry

        lax.fori_loop(0, n_exp, expert, 0)
        end = rstart_ref[n_exp - 1] + _round_up(cnt_ref[n_exp - 1], SUBLANES)
        _fill_tail(end, xs_ref.shape[0], piece)

    def issue(r, carry):
        for k in range(TOP_K):
            dst = slot_ref[r * TOP_K + k]
            pltpu.make_async_copy(h_ref.at[pl.ds(r, 1)], xs_ref.at[pl.ds(dst, 1)],
                                  sem).start(priority=k % 2)
        return carry

    lax.fori_loop(0, tm, issue, 0)
    for k in range(TOP_K):
        pltpu.make_async_copy(h_ref, xs_ref.at[pl.ds(0, tm)], sem).wait()

    @pl.when(i == pl.num_programs(0) - 1)
    def _():
        zbuf[...] = jnp.zeros(zbuf.shape, F32)
        zero_fill(lambda cp: cp.start())
        zero_fill(lambda cp: cp.wait())


def _scatter_rows(rstart, counts, slots_flat, h, n_rows):
    s, d = h.shape
    tm = min(ROW_TM, s)
    grid_spec = pltpu.PrefetchScalarGridSpec(
        num_scalar_prefetch=2,
        grid=(s // tm,),
        in_specs=[pl.BlockSpec((tm * TOP_K,), lambda i, ps, cn: (i,), memory_space=pltpu.SMEM),
                  pl.BlockSpec((tm, d), lambda i, ps, cn: (i, 0))],
        out_specs=pl.BlockSpec(memory_space=pl.ANY),
        scratch_shapes=[pltpu.VMEM((EXP_CHUNK, d), F32), pltpu.SemaphoreType.DMA,
                        pltpu.SemaphoreType.DMA],
    )
    return pl.pallas_call(
        _scatter_kernel,
        out_shape=jax.ShapeDtypeStruct((n_rows, d), F32),
        grid_spec=grid_spec,
        compiler_params=_cparams(("arbitrary",)),
        name="scatter",
    )(rstart, counts, slots_flat, h)


def _experts_kernel(chunk0_ref, cnt_ref, row0_ref, valid_ref, xs_ref, wg_ref, wu_ref, wd_ref,
                    ys_ref, wg_buf, wu_buf, wd_buf, wgu_sc, wd_sc, xbuf, ybuf, w_sem, in_sem,
                    out_sem):
    e = pl.program_id(0)
    n_exp = pl.num_programs(0)
    last = n_exp - 1
    ff = wg_ref.shape[2]
    chunks_of = lambda ex: (cnt_ref[ex] + (EXP_CHUNK - 1)) // EXP_CHUNK

    def weight_copies(ex):
        ws = ex % W_RING
        half = wd_ref.shape[1] // 2
        lo, hi = pl.ds(0, half), pl.ds(half, half)
        parts = ((wg_ref.at[ex], wg_buf.at[ws]), (wu_ref.at[ex], wu_buf.at[ws]),
                 (wd_ref.at[ex, lo], wd_buf.at[ws, lo]), (wd_ref.at[ex, hi], wd_buf.at[ws, hi]))
        return [(pltpu.make_async_copy(src, dst, w_sem.at[i, ws]), 0)
                for i, (src, dst) in enumerate(parts)]

    @pl.when(e == 0)
    def _():
        for ex in range(W_RING - 1):
            for cp, queue in weight_copies(ex):
                cp.start(priority=queue)

    @pl.when(e + (W_RING - 1) < n_exp)
    def _():
        for cp, queue in weight_copies(e + (W_RING - 1)):
            cp.start(priority=queue)

    for cp, _ in weight_copies(e):
        cp.wait()
    wslot = e % W_RING
    g0 = chunk0_ref[e]
    nch = chunks_of(e)
    total = chunk0_ref[last] + chunks_of(last)

    def move(g, n_rows, hbm, buf, sem, do):
        row0 = pl.multiple_of(row0_ref[g], SUBLANES)

        def piece(off, size):
            src = hbm.at[pl.ds(row0 + off, size)]
            dst = buf.at[pl.ds(off, size)]
            do(pltpu.make_async_copy(src, dst, sem) if hbm is xs_ref
               else pltpu.make_async_copy(dst, src, sem))

        @pl.when(n_rows == EXP_CHUNK)
        def _():
            piece(0, EXP_CHUNK)

        @pl.when(n_rows < EXP_CHUNK)
        def _():
            _row_pieces(n_rows, lambda off, size: piece(pl.multiple_of(off, SUBLANES), size))
            whole = n_rows & ~(SUBLANES - 1)
            for j in range(SUBLANES - 1):
                @pl.when(j < (n_rows & (SUBLANES - 1)))
                def _(j=j):
                    piece(whole + j, 1)

    def fetch(g, slot, do):
        move(g, valid_ref[g], xs_ref, xbuf.at[slot], in_sem.at[slot], do)

    def write_back(g, slot, do):
        move(g, _round_up(valid_ref[g], SUBLANES), ys_ref, ybuf.at[slot], out_sem.at[slot], do)

    def start(cp):
        cp.start(priority=1)

    def wait(cp):
        cp.wait()

    n_x, n_y = xbuf.shape[0], ybuf.shape[0]
    ahead = n_x - 1

    @pl.when(e == 0)
    def _():
        xbuf[...] = jnp.zeros(xbuf.shape, F32)
        for j in range(ahead):
            @pl.when(j < total)
            def _(j=j):
                fetch(j, j, start)

    @pl.when(nch > 0)
    def _():
        wgu_sc[:, :ff] = wg_buf[wslot].astype(BF16)
        wgu_sc[:, ff:] = wu_buf[wslot].astype(BF16)
        wd_sc[...] = wd_buf[wslot].astype(BF16)

        def chunk(c, carry):
            g = g0 + c
            slot = g % n_x
            fetch(g, slot, wait)

            @pl.when(g + ahead < total)
            def _():
                fetch(g + ahead, (g + ahead) % n_x, start)

            gu = jnp.dot(xbuf[slot].astype(BF16), wgu_sc[...], preferred_element_type=F32)
            hmid = jax.nn.silu(gu[:, :ff]) * gu[:, ff:]
            y = jnp.dot(hmid.astype(BF16), wd_sc[...], preferred_element_type=F32)

            yslot = g % n_y

            @pl.when(g >= n_y)
            def _():
                write_back(g - n_y, yslot, wait)

            ybuf[yslot] = y
            write_back(g, yslot, start)
            return carry

        lax.fori_loop(0, nch, chunk, 0)

    @pl.when(e == last)
    def _():
        for j in range(n_y):
            @pl.when(total > j)
            def _(j=j):
                write_back(total - 1 - j, (total - 1 - j) % n_y, wait)

        tail_g = jnp.maximum(total - 1, 0)
        end = jnp.where(total > 0, row0_ref[tail_g] + _round_up(valid_ref[tail_g], SUBLANES), 0)
        ybuf[0] = jnp.zeros(ybuf.shape[1:], F32)

        def zero_fill(do):
            def piece(row, size):
                do(pltpu.make_async_copy(ybuf.at[0, pl.ds(0, size)], ys_ref.at[pl.ds(row, size)],
                                         out_sem.at[0]))
            _fill_tail(end, ys_ref.shape[0], piece)

        zero_fill(start)
        zero_fill(wait)


def _experts(chunk0, counts, chunk_row0, chunk_valid, xs, wg, wu, wd):
    n_rows, d = xs.shape
    ne, _, ff = wg.shape
    grid_spec = pltpu.PrefetchScalarGridSpec(
        num_scalar_prefetch=4,
        grid=(ne,),
        in_specs=[
            pl.BlockSpec(memory_space=pl.ANY),
            pl.BlockSpec(memory_space=pl.ANY),
            pl.BlockSpec(memory_space=pl.ANY),
            pl.BlockSpec(memory_space=pl.ANY),
        ],
        out_specs=pl.BlockSpec(memory_space=pl.ANY),
        scratch_shapes=[
            pltpu.VMEM((W_RING, d, ff), F32),
            pltpu.VMEM((W_RING, d, ff), F32),
            pltpu.VMEM((W_RING, ff, d), F32),
            pltpu.VMEM((d, 2 * ff), BF16),
            pltpu.VMEM((ff, d), BF16),
            pltpu.VMEM((X_SLOTS, EXP_CHUNK, d), F32),
            pltpu.VMEM((Y_SLOTS, EXP_CHUNK, d), F32),
            pltpu.SemaphoreType.DMA((4, W_RING)),
            pltpu.SemaphoreType.DMA((X_SLOTS,)),
            pltpu.SemaphoreType.DMA((Y_SLOTS,)),
        ],
    )
    return pl.pallas_call(
        _experts_kernel,
        out_shape=jax.ShapeDtypeStruct((n_rows, d), F32),
        grid_spec=grid_spec,
        compiler_params=_cparams(("arbitrary",), VMEM_LIMIT),
        name="experts",
    )(chunk0, counts, chunk_row0, chunk_valid, xs, wg, wu, wd)


def _combine_kernel(slot_ref, ys_ref, wt_ref, h_ref, wgu_ref, wd_ref, g_ref, b_ref, o_ref,
                    gbuf, sem, *, alpha):
    tm = h_ref.shape[0]
    ff = wd_ref.shape[0]

    def issue(r, carry):
        for k in range(TOP_K):
            src = slot_ref[r * TOP_K + k]
            pltpu.make_async_copy(ys_ref.at[pl.ds(src, 1)], gbuf.at[k, pl.ds(r, 1)],
                                  sem).start(priority=k % 2)
        return carry

    lax.fori_loop(0, tm, issue, 0)

    h = h_ref[...]
    gu = jnp.dot(h.astype(BF16), wgu_ref[...], preferred_element_type=F32)
    hmid = jax.nn.silu(gu[:, :ff]) * gu[:, ff:]
    acc = jnp.dot(hmid.astype(BF16), wd_ref[...], preferred_element_type=F32)

    for k in range(TOP_K):
        pltpu.make_async_copy(ys_ref.at[pl.ds(0, tm)], gbuf.at[k], sem).wait()
    wt = wt_ref[...]
    for k in range(TOP_K):
        acc = acc + wt[:, k:k + 1] * gbuf[k]
    o_ref[...] = _layer_norm(alpha * h + acc, g_ref[...], b_ref[...])


def _combine(slots_flat, ys, wt, h, wgu_s, wd_s, ln_g, ln_b, alpha):
    s, d = h.shape
    tm = min(ROW_TM, s)
    row = lambda c: pl.BlockSpec((tm, c), lambda i: (i, 0))
    const = lambda a: pl.BlockSpec(a.shape, lambda i: (0, 0), pipeline_mode=pl.Buffered(1))
    return pl.pallas_call(
        functools.partial(_combine_kernel, alpha=alpha),
        out_shape=jax.ShapeDtypeStruct((s, d), F32),
        grid=(s // tm,),
        in_specs=[pl.BlockSpec((tm * TOP_K,), lambda i: (i,), memory_space=pltpu.SMEM),
                  pl.BlockSpec(memory_space=pl.ANY),
                  row(TOP_K), row(d), const(wgu_s), const(wd_s), const(ln_g), const(ln_b)],
        out_specs=row(d),
        scratch_shapes=[pltpu.VMEM((TOP_K, tm, d), F32), pltpu.SemaphoreType.DMA],
        compiler_params=_cparams(("arbitrary",), VMEM_LIMIT),
        name="combine",
    )(slots_flat, ys, wt, h, wgu_s, wd_s, ln_g, ln_b)


def _rope_tables(s):
    half = HEAD_DIM // 2
    inv = ROPE_THETA ** (-jnp.arange(half, dtype=F32) / half)
    ang = jnp.arange(s).astype(F32)[:, None] * inv[None, :]
    cos, sin = jnp.cos(ang), jnp.sin(ang)
    cos2, sin2 = jnp.concatenate([cos, cos], axis=1), jnp.concatenate([-sin, sin], axis=1)
    return jnp.stack([cos2, jnp.ones_like(cos2)]), jnp.stack([sin2, jnp.zeros_like(sin2)])


def _layer(x, l, depth, w_in, b_gate, lambda_q1, lambda_k1, lambda_q2, lambda_k2, diff_norm_g,
           w_branch_diff, w_branch_dil, w_out, ln1_g, ln1_b, router_w, router_bias,
           w_gate_e, w_up_e, w_down_e, w_gate_s, w_up_s, w_down_s, ln2_g, ln2_b):
    s, d = x.shape
    alpha = (2.0 * depth) ** 0.25
    lambda_init = 0.8 - 0.6 * math.exp(-0.3 * l)
    cos, sin = _rope_tables(s)
    x_bf = x.astype(BF16)
    w_bf = w_in[l].astype(BF16)

    gates = _inproj_gates(x_bf, w_bf, b_gate[l].reshape(1, 2 * d))
    qkv = _inproj_qkv(x_bf, w_bf, cos, sin)
    lam_p = jnp.stack([lambda_q1[l], lambda_k1[l], lambda_q2[l], lambda_k2[l]]).astype(F32)
    y_diff = _diff_attention(qkv, lam_p, diff_norm_g[l].reshape(1, -1), lambda_init)

    dil = [_dilated_group(_inproj_dil(x_bf, w_bf, cos, sin, g, dilation), g)
           for g, (_, dilation) in enumerate(DIL_CONFIGS)]
    h, logits_t = _merge(
        [o for o, _ in dil], [ls for _, ls in dil], y_diff, gates, x,
        w_branch_diff[l].astype(BF16), w_branch_dil[l].astype(BF16), w_out[l].astype(BF16),
        router_w[l].T.astype(BF16), ln1_g[l].reshape(1, d), ln1_b[l].reshape(1, d), alpha)

    eidx, wts, rank, counts = _route(logits_t, router_bias[l].reshape(N_EXPERTS, 1).astype(F32))
    counts = counts.reshape(N_EXPERTS)
    padded = _round_up(counts, SUBLANES)
    rstart = (jnp.cumsum(padded) - padded).astype(I32)
    n_chunks = (counts + (EXP_CHUNK - 1)) // EXP_CHUNK
    chunk0 = (jnp.cumsum(n_chunks) - n_chunks).astype(I32)
    max_chunks = s * TOP_K // EXP_CHUNK + N_EXPERTS
    cid = jnp.arange(max_chunks, dtype=I32)
    chunk_end = chunk0 + n_chunks
    chunk_expert = jnp.minimum(jnp.sum(chunk_end[None, :] <= cid[:, None], axis=1), N_EXPERTS - 1)
    local = cid - chunk0[chunk_expert]
    chunk_valid = jnp.clip(counts[chunk_expert] - local * EXP_CHUNK, 0, EXP_CHUNK).astype(I32)
    chunk_row0 = (rstart[chunk_expert] + local * EXP_CHUNK).astype(I32)
    n_rows = s * TOP_K + N_EXPERTS * SUBLANES

    slots = _slots(eidx, rank, rstart.astype(F32).reshape(N_EXPERTS, 1))
    slots_flat = slots.T.reshape(-1)
    xs = _scatter_rows(rstart, counts, slots_flat, h, n_rows)
    ys = _experts(chunk0, counts, chunk_row0, chunk_valid, xs,
                  w_gate_e[l], w_up_e[l], w_down_e[l])
    wgu_s = jnp.concatenate([w_gate_s[l], w_up_s[l]], axis=1).astype(BF16)
    return _combine(slots_flat, ys, wts.T, h, wgu_s, w_down_s[l].astype(BF16),
                    ln2_g[l].reshape(1, d), ln2_b[l].reshape(1, d), alpha)


def kernel(x, w_in, b_gate, lambda_q1, lambda_k1, lambda_q2, lambda_k2, diff_norm_g, w_branch_diff,
           w_branch_dil, w_out, ln1_g, ln1_b, router_w, router_bias, w_gate_e, w_up_e, w_down_e,
           w_gate_s, w_up_s, w_down_s, ln2_g, ln2_b):
    b, s, d = x.shape
    depth = w_in.shape[0]
    outs = []
    for bi in range(b):
        xb = x[bi]
        for l in range(depth):
            xb = _layer(xb, l, depth, w_in, b_gate, lambda_q1, lambda_k1, lambda_q2, lambda_k2,
                        diff_norm_g, w_branch_diff, w_branch_dil, w_out, ln1_g, ln1_b, router_w,
                        router_bias, w_gate_e, w_up_e, w_down_e, w_gate_s, w_up_s, w_down_s,
                        ln2_g, ln2_b)
        outs.append(xb)
    return jnp.stack(outs)
```

```python
import functools
import math

import jax
import jax.numpy as jnp
import numpy as np
from jax import lax
from jax.experimental import pallas as pl
from jax.experimental.pallas import tpu as pltpu

F32 = jnp.float32
BF16 = jnp.bfloat16
I32 = jnp.int32

HEAD_DIM = 128
ROPE_THETA = 10000.0
LN_EPS = 1e-5
DIFF_HEADS = 4
DIL_CONFIGS = ((128, 1), (512, 4), (2048, 16))
DIL_HEADS_PER_GROUP = 4
DIL_OUT = DIL_HEADS_PER_GROUP * HEAD_DIM
N_EXPERTS = 256
TOP_K = 8
N_GROUPS = 8
TOPK_GROUPS = 4
GROUP_SIZE = N_EXPERTS // N_GROUPS
ROUTED_SCALE = 2.5
ATTN_SCALE = HEAD_DIM ** -0.5
LOG2E = math.log2(math.e)

PROJ_TILE = 512
N_Q_TILES = 2
V_TILE0 = 4
N_QKV_TILES = 6
DIL_SRC_TILE0 = 6
GATE_SRC_TILE0 = 15
N_GATE_TILES = 8

INPROJ_TM = 1024
ATTN_TQ = 1024
ATTN_TK = 1024
LANES = 128
DIL_T = 128
DIL_BLOCKS_PER_STEP = 4
MERGE_TM = 256
ROUTE_TM = 512
ROW_TM = 256
EXP_CHUNK = 128
W_RING = 2
X_SLOTS = 10
Y_SLOTS = 6
SUBLANES = 8
NEG = -1e30
V7X_VMEM_BYTES = 64 * 1024 * 1024
VMEM_LIMIT = V7X_VMEM_BYTES // 8 * 7


def _cparams(sem, vmem=None):
    return pltpu.CompilerParams(dimension_semantics=sem, vmem_limit_bytes=vmem)


def _rope(a, cos, sin):
    outs = []
    for h in range(a.shape[1] // HEAD_DIM):
        ah = a[:, h * HEAD_DIM:(h + 1) * HEAD_DIM]
        outs.append(ah * cos + pltpu.roll(ah, HEAD_DIM // 2, 1) * sin)
    return jnp.concatenate(outs, axis=1)


def _gates_kernel(x_ref, w_ref, b_ref, o_ref):
    acc = jnp.dot(x_ref[...], w_ref[...], preferred_element_type=F32)
    o_ref[...] = jax.nn.sigmoid(acc + b_ref[...]).astype(BF16)


def _inproj_gates(x_bf, w_bf, b_gate):
    s, d = x_bf.shape
    tm = min(INPROJ_TM, s)
    return pl.pallas_call(
        _gates_kernel,
        out_shape=jax.ShapeDtypeStruct((s, N_GATE_TILES * PROJ_TILE), BF16),
        grid=(s // tm, N_GATE_TILES),
        in_specs=[
            pl.BlockSpec((tm, d), lambda i, j: (i, 0)),
            pl.BlockSpec((d, PROJ_TILE), lambda i, j: (0, GATE_SRC_TILE0 + j)),
            pl.BlockSpec((1, PROJ_TILE), lambda i, j: (0, j)),
        ],
        out_specs=pl.BlockSpec((tm, PROJ_TILE), lambda i, j: (i, j)),
        compiler_params=_cparams(("arbitrary", "arbitrary"), VMEM_LIMIT),
        name="inproj_gates",
    )(x_bf, w_bf, b_gate)


def _qkv_kernel(x_ref, w_ref, cos_ref, sin_ref, o_ref):
    j = pl.program_id(1)
    acc = jnp.dot(x_ref[...], w_ref[...], preferred_element_type=F32)
    scale = jnp.where(j < N_Q_TILES, ATTN_SCALE * LOG2E, 1.0).astype(F32)
    o_ref[...] = _rope(acc, cos_ref[...] * scale, sin_ref[...] * scale).astype(BF16)


def _inproj_qkv(x_bf, w_bf, cos_tab, sin_tab):
    s, d = x_bf.shape
    tm = min(INPROJ_TM, s)
    tab = pl.BlockSpec((None, tm, HEAD_DIM), lambda i, j: ((j >= V_TILE0).astype(I32), i, 0))
    return pl.pallas_call(
        _qkv_kernel,
        out_shape=jax.ShapeDtypeStruct((s, N_QKV_TILES * PROJ_TILE), BF16),
        grid=(s // tm, N_QKV_TILES),
        in_specs=[
            pl.BlockSpec((tm, d), lambda i, j: (i, 0)),
            pl.BlockSpec((d, PROJ_TILE), lambda i, j: (0, j)),
            tab, tab,
        ],
        out_specs=pl.BlockSpec((tm, PROJ_TILE), lambda i, j: (i, j)),
        compiler_params=_cparams(("arbitrary", "arbitrary"), VMEM_LIMIT),
        name="inproj_qkv",
    )(x_bf, w_bf, cos_tab, sin_tab)


def _inproj_dil_kernel(x_ref, w_ref, cos_ref, sin_ref, o_ref, scr, *, dilation):
    part = pl.program_id(1)
    acc = jnp.dot(x_ref[...], w_ref[...], preferred_element_type=F32)
    scale = jnp.where(part == 0, ATTN_SCALE, 1.0).astype(F32)
    val = _rope(acc, cos_ref[...] * scale, sin_ref[...] * scale)
    if dilation == 1:
        o_ref[0] = val.astype(BF16)
        return
    for c in range(scr.shape[0]):
        scr[c] = val[:, c * HEAD_DIM:(c + 1) * HEAD_DIM]
    n = scr.shape[1] // dilation
    for r in range(dilation):
        for c in range(scr.shape[0]):
            o_ref[r, :, c * HEAD_DIM:(c + 1) * HEAD_DIM] = (
                scr[c, pl.ds(r, n, stride=dilation), :].astype(BF16))


def _inproj_dil(x_bf, w_bf, cos_tab, sin_tab, g, dilation):
    s, d = x_bf.shape
    tm = min(INPROJ_TM, s)
    n = tm // dilation
    tab = pl.BlockSpec((None, tm, HEAD_DIM), lambda i, p: ((p == 2).astype(I32), i, 0))
    return pl.pallas_call(
        functools.partial(_inproj_dil_kernel, dilation=dilation),
        out_shape=jax.ShapeDtypeStruct((dilation, s // dilation, 3 * PROJ_TILE), BF16),
        grid=(s // tm, 3),
        in_specs=[
            pl.BlockSpec((tm, d), lambda i, p: (i, 0)),
            pl.BlockSpec((d, PROJ_TILE), lambda i, p: (0, DIL_SRC_TILE0 + g + 3 * p)),
            tab, tab,
        ],
        out_specs=pl.BlockSpec((dilation, n, PROJ_TILE), lambda i, p: (0, i, p)),
        scratch_shapes=[pltpu.VMEM((PROJ_TILE // HEAD_DIM, tm, HEAD_DIM), F32)],
        compiler_params=_cparams(("arbitrary", "arbitrary"), VMEM_LIMIT),
        name=f"inproj_dil{g}",
    )(x_bf, w_bf, cos_tab, sin_tab)


def _diff_kernel(qi_ref, kj_ref, q_ref, k_ref, v_ref, lam_ref, g_ref, o_ref, m_sc, l_sc, acc_sc,
                 *, lambda_init):
    step = pl.program_id(1)
    qi = qi_ref[step]
    kj = kj_ref[step]
    tq, tk = q_ref.shape[0], k_ref.shape[0]
    last_kj = qi // (tk // tq)

    @pl.when(kj == 0)
    def _():
        m_sc[...] = jnp.full(m_sc.shape, NEG, F32)
        l_sc[...] = jnp.zeros(l_sc.shape, F32)
        acc_sc[...] = jnp.zeros(acc_sc.shape, F32)

    def update(rows, keys, masked):
        nr, nk = rows.stop - rows.start, keys.stop - keys.start
        v = v_ref[keys, :]
        if masked:
            row = qi * tq + rows.start + lax.broadcasted_iota(I32, (nr, nk), 0)
            col = kj * tk + keys.start + lax.broadcasted_iota(I32, (nr, nk), 1)
            keep = col <= row
        for mm in range(2):
            q = q_ref[rows, mm * HEAD_DIM:(mm + 1) * HEAD_DIM]
            k = k_ref[keys, mm * HEAD_DIM:(mm + 1) * HEAD_DIM]
            s = lax.dot_general(q, k, (((1,), (1,)), ((), ())), preferred_element_type=F32)
            if masked:
                s = jnp.where(keep, s, NEG)
            m_prev = m_sc[mm, rows]
            m_new = jnp.maximum(m_prev, jnp.max(s, axis=-1, keepdims=True))
            alpha = jnp.exp2(m_prev - m_new)
            p = jnp.exp2(s - jnp.tile(m_new, (1, nk // LANES)))
            l_sc[mm, rows] = alpha * l_sc[mm, rows] + jnp.sum(p, axis=-1, keepdims=True)
            pv = jnp.dot(p.astype(BF16), v, preferred_element_type=F32)
            acc_sc[mm, rows] = jnp.tile(alpha, (1, v.shape[1] // LANES)) * acc_sc[mm, rows] + pv
            m_sc[mm, rows] = m_new

    @pl.when(kj < last_kj)
    def _():
        update(slice(0, tq), slice(0, tk), False)

    @pl.when(kj == last_kj)
    def _():
        if tq == tk:
            half = tq // 2
            update(slice(0, half), slice(0, half), True)
            update(slice(half, tq), slice(0, tk), True)
        else:
            update(slice(0, tq), slice(0, tk), True)
        lam_p = lam_ref[...]
        lam = (jnp.exp(jnp.sum(lam_p[0:1] * lam_p[1:2], axis=-1, keepdims=True))
               - jnp.exp(jnp.sum(lam_p[2:3] * lam_p[3:4], axis=-1, keepdims=True))
               + lambda_init)
        rep = acc_sc.shape[2] // LANES
        o = (acc_sc[0] / jnp.tile(l_sc[0], (1, rep))
             - lam * (acc_sc[1] / jnp.tile(l_sc[1], (1, rep))))
        o = o * lax.rsqrt(jnp.mean(o * o, axis=-1, keepdims=True) + LN_EPS) * g_ref[...]
        o_ref[...] = (o * (1.0 - lambda_init)).astype(o_ref.dtype)


def _diff_attention(proj, lam_p, norm_g, lambda_init):
    s = proj.shape[0]
    tq = min(ATTN_TQ, s)
    tk = min(ATTN_TK, s)
    pairs = [(i, j) for i in range(s // tq) for j in range(i * tq // tk + 1)]
    qi_tab = np.asarray([p[0] for p in pairs], np.int32)
    kj_tab = np.asarray([p[1] for p in pairs], np.int32)
    vw = 2 * HEAD_DIM
    grid_spec = pltpu.PrefetchScalarGridSpec(
        num_scalar_prefetch=2,
        grid=(DIFF_HEADS, len(pairs)),
        in_specs=[
            pl.BlockSpec((tq, vw), lambda h, st, qi, kj: (qi[st], h)),
            pl.BlockSpec((tk, vw), lambda h, st, qi, kj: (kj[st], DIFF_HEADS + h)),
            pl.BlockSpec((tk, vw), lambda h, st, qi, kj: (kj[st], 2 * DIFF_HEADS + h)),
            pl.BlockSpec((4, HEAD_DIM), lambda h, st, qi, kj: (0, 0)),
            pl.BlockSpec((1, vw), lambda h, st, qi, kj: (0, 0)),
        ],
        out_specs=pl.BlockSpec((tq, vw), lambda h, st, qi, kj: (qi[st], h)),
        scratch_shapes=[
            pltpu.VMEM((2, tq, LANES), F32),
            pltpu.VMEM((2, tq, LANES), F32),
            pltpu.VMEM((2, tq, vw), F32),
        ],
    )
    return pl.pallas_call(
        functools.partial(_diff_kernel, lambda_init=lambda_init),
        out_shape=jax.ShapeDtypeStruct((s, DIFF_HEADS * vw), BF16),
        grid_spec=grid_spec,
        compiler_params=_cparams(("arbitrary", "arbitrary"), VMEM_LIMIT),
        name="diffattn",
    )(jnp.asarray(qi_tab), jnp.asarray(kj_tab), proj, proj, proj, lam_p, norm_g)


def _dil_kernel(q_ref, kp_ref, kc_ref, vp_ref, vc_ref, o_ref, lse_ref):
    n = pl.program_id(1)
    t = kp_ref.shape[0]
    qi = lax.broadcasted_iota(I32, (t, t), 0)
    kj = lax.broadcasted_iota(I32, (t, t), 1)
    in_span = kj >= qi
    keep_cur = kj <= qi
    dn = (((1,), (1,)), ((), ()))
    for b in range(q_ref.shape[0] // t):
        rows = slice(b * t, (b + 1) * t)
        before = slice((b - 1) * t, b * t)
        for h in range(DIL_HEADS_PER_GROUP):
            sl = slice(h * HEAD_DIM, (h + 1) * HEAD_DIM)
            q = q_ref[rows, sl]
            k_prev, v_prev = (kp_ref[:, sl], vp_ref[:, sl]) if b == 0 else (kc_ref[before, sl],
                                                                              vc_ref[before, sl])
            keep_prev = (in_span & (n > 0)) if b == 0 else in_span
            sp = lax.dot_general(q, k_prev, dn, preferred_element_type=F32)
            sc = lax.dot_general(q, kc_ref[rows, sl], dn, preferred_element_type=F32)
            sp = jnp.where(keep_prev, sp, NEG)
            sc = jnp.where(keep_cur, sc, NEG)
            m = jnp.maximum(jnp.max(sp, axis=-1, keepdims=True),
                            jnp.max(sc, axis=-1, keepdims=True))
            ep = jnp.exp(sp - m)
            ec = jnp.exp(sc - m)
            den = jnp.sum(ep, axis=-1, keepdims=True) + jnp.sum(ec, axis=-1, keepdims=True)
            acc = (jnp.dot(ep.astype(BF16), v_prev, preferred_element_type=F32)
                   + jnp.dot(ec.astype(BF16), vc_ref[rows, sl], preferred_element_type=F32))
            o_ref[rows, sl] = acc / den
            lse_ref[rows, sl] = jnp.broadcast_to(m + jnp.log(den), (t, HEAD_DIM))


def _dilated_group(qkv, g):
    dilation, l, _ = qkv.shape
    t = DIL_T
    nb = l // t
    per_step = DIL_BLOCKS_PER_STEP if nb % DIL_BLOCKS_PER_STEP == 0 else 1
    cur = lambda part: (lambda r, n: (r, n, part))
    prev = lambda part: (lambda r, n: (r, jnp.maximum(n * per_step - 1, 0), part))
    wide = (None, per_step * t, PROJ_TILE)
    one = (None, t, PROJ_TILE)
    return pl.pallas_call(
        _dil_kernel,
        out_shape=[jax.ShapeDtypeStruct((dilation, l, DIL_OUT), F32)] * 2,
        grid=(dilation, nb // per_step),
        in_specs=[
            pl.BlockSpec(wide, cur(0)),
            pl.BlockSpec(one, prev(1)),
            pl.BlockSpec(wide, cur(1)),
            pl.BlockSpec(one, prev(2)),
            pl.BlockSpec(wide, cur(2)),
        ],
        out_specs=[pl.BlockSpec((None, per_step * t, DIL_OUT), lambda r, n: (r, n, 0))] * 2,
        compiler_params=_cparams(("arbitrary", "arbitrary")),
        name=f"dilattn{g}",
    )(qkv, qkv, qkv, qkv, qkv)


def _layer_norm(z, g, b):
    mu = jnp.mean(z, axis=-1, keepdims=True)
    zc = z - mu
    var = jnp.mean(zc * zc, axis=-1, keepdims=True)
    return zc * lax.rsqrt(var + LN_EPS) * g + b


def _merge_kernel(o0, o1, o2, l0, l1, l2, yd_ref, ga_ref, gb_ref, x_ref, wbd_ref, wbl_ref, wo_ref,
                  rwt_ref, g_ref, b_ref, h_ref, lt_ref, scr, *, alpha):
    def positions(ref, slot):
        d = ref.shape[0]
        if d == 1:
            return ref[0]
        heads = scr.shape[1]
        for r in range(d):
            for c in range(heads):
                scr[slot, c, pl.ds(r, ref.shape[1], stride=d), :] = (
                    ref[r, :, c * HEAD_DIM:(c + 1) * HEAD_DIM])
        return jnp.concatenate([scr[slot, c] for c in range(heads)], axis=1)

    la, lb, lc = positions(l0, 0), positions(l1, 0), positions(l2, 1)
    m = jnp.maximum(jnp.maximum(la, lb), lc)
    ea, eb, ec = jnp.exp(la - m), jnp.exp(lb - m), jnp.exp(lc - m)
    num = ea * positions(o0, 0) + eb * positions(o1, 2) + ec * positions(o2, 3)
    ydil = num / (ea + eb + ec)
    a = jnp.dot(yd_ref[...], wbd_ref[...], preferred_element_type=F32)
    b = jnp.dot(ydil.astype(BF16), wbl_ref[...], preferred_element_type=F32)
    merged = ga_ref[...].astype(F32) * a + gb_ref[...].astype(F32) * b
    z = alpha * x_ref[...] + jnp.dot(merged.astype(BF16), wo_ref[...], preferred_element_type=F32)
    h = _layer_norm(z, g_ref[...], b_ref[...])
    h_ref[...] = h
    lt_ref[...] = lax.dot_general(rwt_ref[...], h.astype(BF16), (((1,), (1,)), ((), ())),
                                  preferred_element_type=F32)


def _merge(dil_o, dil_l, y_diff, gates, x, wbd, wbl, wo, rwt, ln_g, ln_b, alpha):
    s, d = x.shape
    tm = min(MERGE_TM, s)
    row = lambda w: pl.BlockSpec((tm, w), lambda i: (i, 0))
    const = lambda a: pl.BlockSpec(a.shape, lambda i: (0, 0), pipeline_mode=pl.Buffered(1))
    res = lambda a: pl.BlockSpec((a.shape[0], tm // a.shape[0], DIL_OUT), lambda i: (0, i, 0))
    in_specs = ([res(a) for a in dil_o] + [res(a) for a in dil_l] + [
        row(y_diff.shape[1]),
        pl.BlockSpec((tm, d), lambda i: (i, 0)),
        pl.BlockSpec((tm, d), lambda i: (i, 1)),
        row(d), const(wbd), const(wbl), const(wo), const(rwt), const(ln_g), const(ln_b)])
    return pl.pallas_call(
        functools.partial(_merge_kernel, alpha=alpha),
        out_shape=[jax.ShapeDtypeStruct((s, d), F32),
                   jax.ShapeDtypeStruct((N_EXPERTS, s), F32)],
        grid=(s // tm,),
        in_specs=in_specs,
        out_specs=[row(d), pl.BlockSpec((N_EXPERTS, tm), lambda i: (0, i))],
        scratch_shapes=[pltpu.VMEM((4, DIL_HEADS_PER_GROUP, tm, HEAD_DIM), F32)],
        compiler_params=_cparams(("arbitrary",), VMEM_LIMIT),
        name="merge",
    )(*dil_o, *dil_l, y_diff, gates, gates, x, wbd, wbl, wo, rwt, ln_g, ln_b)


def _route_kernel(lt_ref, bias_ref, e_ref, w_ref, r_ref, cnt_ref, carry):
    i = pl.program_id(0)
    tm = lt_ref.shape[1]
    ninf = -jnp.inf

    @pl.when(i == 0)
    def _():
        carry[...] = jnp.zeros(carry.shape, F32)

    sc = jax.nn.sigmoid(lt_ref[...])
    biased = sc + bias_ref[...]
    iog = lax.broadcasted_iota(I32, (GROUP_SIZE, tm), 0)
    blocks, gscore = [], []
    for g in range(N_GROUPS):
        blk = biased[g * GROUP_SIZE:(g + 1) * GROUP_SIZE, :]
        m1 = jnp.max(blk, axis=0, keepdims=True)
        i1 = jnp.min(jnp.where(blk == m1, iog, GROUP_SIZE), axis=0, keepdims=True)
        m2 = jnp.max(jnp.where(iog == i1, ninf, blk), axis=0, keepdims=True)
        blocks.append(blk)
        gscore.append(m1 + m2)
    masked = []
    for g in range(N_GROUPS):
        ahead = jnp.zeros((1, tm), F32)
        for o in range(N_GROUPS):
            if o == g:
                continue
            wins = (gscore[o] >= gscore[g]) if o < g else (gscore[o] > gscore[g])
            ahead = ahead + jnp.where(wins, 1.0, 0.0)
        keep = jnp.broadcast_to(ahead, (GROUP_SIZE, tm)) < TOPK_GROUPS
        masked.append(jnp.where(keep, blocks[g], ninf))
    v = jnp.concatenate(masked, axis=0)
    ioe = lax.broadcasted_iota(I32, (N_EXPERTS, tm), 0)
    idxs, ws = [], []
    sel = jnp.zeros((N_EXPERTS, tm), jnp.bool_)
    for _ in range(TOP_K):
        mx = jnp.max(v, axis=0, keepdims=True)
        idx = jnp.min(jnp.where(v == mx, ioe, N_EXPERTS), axis=0, keepdims=True)
        hit = ioe == idx
        idxs.append(idx)
        ws.append(jnp.sum(jnp.where(hit, sc, 0.0), axis=0, keepdims=True))
        sel = sel | hit
        v = jnp.where(hit, ninf, v)
    wsum = ws[0]
    for wk in ws[1:]:
        wsum = wsum + wk

    self = jnp.where(sel, 1.0, 0.0)
    ta = lax.broadcasted_iota(I32, (tm, tm), 0)
    tb = lax.broadcasted_iota(I32, (tm, tm), 1)
    before = jnp.where(ta < tb, 1.0, 0.0).astype(BF16)
    pos = jnp.dot(self.astype(BF16), before, preferred_element_type=F32) + carry[...]
    for k in range(TOP_K):
        e_ref[k:k + 1, :] = idxs[k]
        w_ref[k:k + 1, :] = ws[k] / wsum * ROUTED_SCALE
        rk = jnp.sum(jnp.where(ioe == idxs[k], pos, 0.0), axis=0, keepdims=True)
        r_ref[k:k + 1, :] = rk.astype(I32)
    total = carry[...] + jnp.sum(self, axis=1, keepdims=True)
    carry[...] = total
    cnt_ref[...] = total.astype(I32)


def _route(logits_t, bias_col):
    e, s = logits_t.shape
    tm = min(ROUTE_TM, s)
    tok = pl.BlockSpec((TOP_K, tm), lambda i: (0, i))
    return pl.pallas_call(
        _route_kernel,
        out_shape=[jax.ShapeDtypeStruct((TOP_K, s), I32),
                   jax.ShapeDtypeStruct((TOP_K, s), F32),
                   jax.ShapeDtypeStruct((TOP_K, s), I32),
                   jax.ShapeDtypeStruct((e, 1), I32)],
        grid=(s // tm,),
        in_specs=[pl.BlockSpec((e, tm), lambda i: (0, i)),
                  pl.BlockSpec((e, 1), lambda i: (0, 0))],
        out_specs=[tok, tok, tok, pl.BlockSpec((e, 1), lambda i: (0, 0))],
        scratch_shapes=[pltpu.VMEM((e, 1), F32)],
        compiler_params=_cparams(("arbitrary",)),
        name="route",
    )(logits_t, bias_col)


def _slots_kernel(e_ref, r_ref, ps_ref, s_ref):
    tm = e_ref.shape[1]
    ioe = lax.broadcasted_iota(I32, (N_EXPERTS, tm), 0)
    ps = ps_ref[...]
    for k in range(TOP_K):
        start = jnp.sum(jnp.where(ioe == e_ref[k:k + 1, :], ps, 0.0), axis=0, keepdims=True)
        s_ref[k:k + 1, :] = start.astype(I32) + r_ref[k:k + 1, :]


def _slots(eidx, rank, pstart_col):
    s = eidx.shape[1]
    tm = min(ROUTE_TM, s)
    tok = pl.BlockSpec((TOP_K, tm), lambda i: (0, i))
    return pl.pallas_call(
        _slots_kernel,
        out_shape=jax.ShapeDtypeStruct((TOP_K, s), I32),
        grid=(s // tm,),
        in_specs=[tok, tok, pl.BlockSpec((N_EXPERTS, 1), lambda i: (0, 0))],
        out_specs=tok,
        compiler_params=_cparams(("arbitrary",)),
        name="slots",
    )(eidx, rank, pstart_col)


def _row_pieces(length, fn):
    off = jnp.int32(0)
    size = EXP_CHUNK // 2
    while size >= SUBLANES:
        @pl.when((length & size) != 0)
        def _(off=off, size=size):
            fn(off, size)
        off = off + (length & size)
        size //= 2


def _fill_tail(end, n_rows, piece):
    tail = n_rows - end
    n_full = tail // EXP_CHUNK

    def full(t, carry):
        piece(pl.multiple_of(end + t * EXP_CHUNK, SUBLANES), EXP_CHUNK)
        return carry

    lax.fori_loop(0, n_full, full, 0)
    base = end + n_full * EXP_CHUNK
    _row_pieces(tail & (EXP_CHUNK - 1),
                lambda off, size: piece(pl.multiple_of(base + off, SUBLANES), size))


def _round_up(n, m):
    return (n + (m - 1)) // m * m


def _scatter_kernel(rstart_ref, cnt_ref, slot_ref, h_ref, xs_ref, zbuf, sem, zsem):
    i = pl.program_id(0)
    tm = h_ref.shape[0]
    n_exp = cnt_ref.shape[0]

    def zero_fill(do):
        def piece(row, size):
            do(pltpu.make_async_copy(zbuf.at[pl.ds(0, size)], xs_ref.at[pl.ds(row, size)], zsem))

        def expert(ex, carry):
            cnt = cnt_ref[ex]
            row = rstart_ref[ex] + cnt
            for j in range(SUBLANES - 1):
                @pl.when(j < ((-cnt) & (SUBLANES - 1)))
                def _(row=row, j=j):
                    piece(row + j, 1)
            return carry

        lax.fori_loop(0, n_exp, expert, 0)
        end = rstart_ref[n_exp - 1] + _round_up(cnt_ref[n_exp - 1], SUBLANES)
        _fill_tail(end, xs_ref.shape[0], piece)

    def issue(r, carry):
        for k in range(TOP_K):
            dst = slot_ref[r * TOP_K + k]
            pltpu.make_async_copy(h_ref.at[pl.ds(r, 1)], xs_ref.at[pl.ds(dst, 1)],
                                  sem).start(priority=k % 2)
        return carry

    lax.fori_loop(0, tm, issue, 0)
    for k in range(TOP_K):
        pltpu.make_async_copy(h_ref, xs_ref.at[pl.ds(0, tm)], sem).wait()

    @pl.when(i == pl.num_programs(0) - 1)
    def _():
        zbuf[...] = jnp.zeros(zbuf.shape, F32)
        zero_fill(lambda cp: cp.start())
        zero_fill(lambda cp: cp.wait())


def _scatter_rows(rstart, counts, slots_flat, h, n_rows):
    s, d = h.shape
    tm = min(ROW_TM, s)
    grid_spec = pltpu.PrefetchScalarGridSpec(
        num_scalar_prefetch=2,
        grid=(s // tm,),
        in_specs=[pl.BlockSpec((tm * TOP_K,), lambda i, ps, cn: (i,), memory_space=pltpu.SMEM),
                  pl.BlockSpec((tm, d), lambda i, ps, cn: (i, 0))],
        out_specs=pl.BlockSpec(memory_space=pl.ANY),
        scratch_shapes=[pltpu.VMEM((EXP_CHUNK, d), F32), pltpu.SemaphoreType.DMA,
                        pltpu.SemaphoreType.DMA],
    )
    return pl.pallas_call(
        _scatter_kernel,
        out_shape=jax.ShapeDtypeStruct((n_rows, d), F32),
        grid_spec=grid_spec,
        compiler_params=_cparams(("arbitrary",)),
        name="scatter",
    )(rstart, counts, slots_flat, h)


def _experts_kernel(chunk0_ref, cnt_ref, row0_ref, valid_ref, xs_ref, wg_ref, wu_ref, wd_ref,
                    ys_ref, wg_buf, wu_buf, wd_buf, wgu_sc, wd_sc, xbuf, ybuf, w_sem, in_sem,
                    out_sem):
    e = pl.program_id(0)
    n_exp = pl.num_programs(0)
    last = n_exp - 1
    ff = wg_ref.shape[2]
    chunks_of = lambda ex: (cnt_ref[ex] + (EXP_CHUNK - 1)) // EXP_CHUNK

    def weight_copies(ex):
        ws = ex % W_RING
        half = wd_ref.shape[1] // 2
        lo, hi = pl.ds(0, half), pl.ds(half, half)
        parts = ((wg_ref.at[ex], wg_buf.at[ws]), (wu_ref.at[ex], wu_buf.at[ws]),
                 (wd_ref.at[ex, lo], wd_buf.at[ws, lo]), (wd_ref.at[ex, hi], wd_buf.at[ws, hi]))
        return [(pltpu.make_async_copy(src, dst, w_sem.at[i, ws]), 0)
                for i, (src, dst) in enumerate(parts)]

    @pl.when(e == 0)
    def _():
        for ex in range(W_RING - 1):
            for cp, queue in weight_copies(ex):
                cp.start(priority=queue)

    @pl.when(e + (W_RING - 1) < n_exp)
    def _():
        for cp, queue in weight_copies(e + (W_RING - 1)):
            cp.start(priority=queue)

    for cp, _ in weight_copies(e):
        cp.wait()
    wslot = e % W_RING
    g0 = chunk0_ref[e]
    nch = chunks_of(e)
    total = chunk0_ref[last] + chunks_of(last)

    def move(g, n_rows, hbm, buf, sem, do):
        row0 = pl.multiple_of(row0_ref[g], SUBLANES)

        def piece(off, size):
            src = hbm.at[pl.ds(row0 + off, size)]
            dst = buf.at[pl.ds(off, size)]
            do(pltpu.make_async_copy(src, dst, sem) if hbm is xs_ref
               else pltpu.make_async_copy(dst, src, sem))

        @pl.when(n_rows == EXP_CHUNK)
        def _():
            piece(0, EXP_CHUNK)

        @pl.when(n_rows < EXP_CHUNK)
        def _():
            _row_pieces(n_rows, lambda off, size: piece(pl.multiple_of(off, SUBLANES), size))
            whole = n_rows & ~(SUBLANES - 1)
            for j in range(SUBLANES - 1):
                @pl.when(j < (n_rows & (SUBLANES - 1)))
                def _(j=j):
                    piece(whole + j, 1)

    def fetch(g, slot, do):
        move(g, valid_ref[g], xs_ref, xbuf.at[slot], in_sem.at[slot], do)

    def write_back(g, slot, do):
        move(g, _round_up(valid_ref[g], SUBLANES), ys_ref, ybuf.at[slot], out_sem.at[slot], do)

    def start(cp):
        cp.start(priority=1)

    def wait(cp):
        cp.wait()

    n_x, n_y = xbuf.shape[0], ybuf.shape[0]
    ahead = n_x - 1

    @pl.when(e == 0)
    def _():
        xbuf[...] = jnp.zeros(xbuf.shape, F32)
        for j in range(ahead):
            @pl.when(j < total)
            def _(j=j):
                fetch(j, j, start)

    @pl.when(nch > 0)
    def _():
        wgu_sc[:, :ff] = wg_buf[wslot].astype(BF16)
        wgu_sc[:, ff:] = wu_buf[wslot].astype(BF16)
        wd_sc[...] = wd_buf[wslot].astype(BF16)

        def chunk(c, carry):
            g = g0 + c
            slot = g % n_x
            fetch(g, slot, wait)

            @pl.when(g + ahead < total)
            def _():
                fetch(g + ahead, (g + ahead) % n_x, start)

            gu = jnp.dot(xbuf[slot].astype(BF16), wgu_sc[...], preferred_element_type=F32)
            hmid = jax.nn.silu(gu[:, :ff]) * gu[:, ff:]
            y = jnp.dot(hmid.astype(BF16), wd_sc[...], preferred_element_type=F32)

            yslot = g % n_y

            @pl.when(g >= n_y)
            def _():
                write_back(g - n_y, yslot, wait)

            ybuf[yslot] = y
            write_back(g, yslot, start)
            return carry

        lax.fori_loop(0, nch, chunk, 0)

    @pl.when(e == last)
    def _():
        for j in range(n_y):
            @pl.when(total > j)
            def _(j=j):
                write_back(total - 1 - j, (total - 1 - j) % n_y, wait)

        tail_g = jnp.maximum(total - 1, 0)
        end = jnp.where(total > 0, row0_ref[tail_g] + _round_up(valid_ref[tail_g], SUBLANES), 0)
        ybuf[0] = jnp.zeros(ybuf.shape[1:], F32)

        def zero_fill(do):
            def piece(row, size):
                do(pltpu.make_async_copy(ybuf.at[0, pl.ds(0, size)], ys_ref.at[pl.ds(row, size)],
                                         out_sem.at[0]))
            _fill_tail(end, ys_ref.shape[0], piece)

        zero_fill(start)
        zero_fill(wait)


def _experts(chunk0, counts, chunk_row0, chunk_valid, xs, wg, wu, wd):
    n_rows, d = xs.shape
    ne, _, ff = wg.shape
    grid_spec = pltpu.PrefetchScalarGridSpec(
        num_scalar_prefetch=4,
        grid=(ne,),
        in_specs=[
            pl.BlockSpec(memory_space=pl.ANY),
            pl.BlockSpec(memory_space=pl.ANY),
            pl.BlockSpec(memory_space=pl.ANY),
            pl.BlockSpec(memory_space=pl.ANY),
        ],
        out_specs=pl.BlockSpec(memory_space=pl.ANY),
        scratch_shapes=[
            pltpu.VMEM((W_RING, d, ff), F32),
            pltpu.VMEM((W_RING, d, ff), F32),
            pltpu.VMEM((W_RING, ff, d), F32),
            pltpu.VMEM((d, 2 * ff), BF16),
            pltpu.VMEM((ff, d), BF16),
            pltpu.VMEM((X_SLOTS, EXP_CHUNK, d), F32),
            pltpu.VMEM((Y_SLOTS, EXP_CHUNK, d), F32),
            pltpu.SemaphoreType.DMA((4, W_RING)),
            pltpu.SemaphoreType.DMA((X_SLOTS,)),
            pltpu.SemaphoreType.DMA((Y_SLOTS,)),
        ],
    )
    return pl.pallas_call(
        _experts_kernel,
        out_shape=jax.ShapeDtypeStruct((n_rows, d), F32),
        grid_spec=grid_spec,
        compiler_params=_cparams(("arbitrary",), VMEM_LIMIT),
        name="experts",
    )(chunk0, counts, chunk_row0, chunk_valid, xs, wg, wu, wd)


def _combine_kernel(slot_ref, ys_ref, wt_ref, h_ref, wgu_ref, wd_ref, g_ref, b_ref, o_ref,
                    gbuf, sem, *, alpha):
    tm = h_ref.shape[0]
    ff = wd_ref.shape[0]

    def issue(r, carry):
        for k in range(TOP_K):
            src = slot_ref[r * TOP_K + k]
            pltpu.make_async_copy(ys_ref.at[pl.ds(src, 1)], gbuf.at[k, pl.ds(r, 1)],
                                  sem).start(priority=k % 2)
        return carry

    lax.fori_loop(0, tm, issue, 0)

    h = h_ref[...]
    gu = jnp.dot(h.astype(BF16), wgu_ref[...], preferred_element_type=F32)
    hmid = jax.nn.silu(gu[:, :ff]) * gu[:, ff:]
    acc = jnp.dot(hmid.astype(BF16), wd_ref[...], preferred_element_type=F32)

    for k in range(TOP_K):
        pltpu.make_async_copy(ys_ref.at[pl.ds(0, tm)], gbuf.at[k], sem).wait()
    wt = wt_ref[...]
    for k in range(TOP_K):
        acc = acc + wt[:, k:k + 1] * gbuf[k]
    o_ref[...] = _layer_norm(alpha * h + acc, g_ref[...], b_ref[...])


def _combine(slots_flat, ys, wt, h, wgu_s, wd_s, ln_g, ln_b, alpha):
    s, d = h.shape
    tm = min(ROW_TM, s)
    row = lambda c: pl.BlockSpec((tm, c), lambda i: (i, 0))
    const = lambda a: pl.BlockSpec(a.shape, lambda i: (0, 0), pipeline_mode=pl.Buffered(1))
    return pl.pallas_call(
        functools.partial(_combine_kernel, alpha=alpha),
        out_shape=jax.ShapeDtypeStruct((s, d), F32),
        grid=(s // tm,),
        in_specs=[pl.BlockSpec((tm * TOP_K,), lambda i: (i,), memory_space=pltpu.SMEM),
                  pl.BlockSpec(memory_space=pl.ANY),
                  row(TOP_K), row(d), const(wgu_s), const(wd_s), const(ln_g), const(ln_b)],
        out_specs=row(d),
        scratch_shapes=[pltpu.VMEM((TOP_K, tm, d), F32), pltpu.SemaphoreType.DMA],
        compiler_params=_cparams(("arbitrary",), VMEM_LIMIT),
        name="combine",
    )(slots_flat, ys, wt, h, wgu_s, wd_s, ln_g, ln_b)


def _rope_tables(s):
    half = HEAD_DIM // 2
    inv = ROPE_THETA ** (-jnp.arange(half, dtype=F32) / half)
    ang = jnp.arange(s).astype(F32)[:, None] * inv[None, :]
    cos, sin = jnp.cos(ang), jnp.sin(ang)
    cos2, sin2 = jnp.concatenate([cos, cos], axis=1), jnp.concatenate([-sin, sin], axis=1)
    return jnp.stack([cos2, jnp.ones_like(cos2)]), jnp.stack([sin2, jnp.zeros_like(sin2)])


def _layer(x, l, depth, w_in, b_gate, lambda_q1, lambda_k1, lambda_q2, lambda_k2, diff_norm_g,
           w_branch_diff, w_branch_dil, w_out, ln1_g, ln1_b, router_w, router_bias,
           w_gate_e, w_up_e, w_down_e, w_gate_s, w_up_s, w_down_s, ln2_g, ln2_b):
    s, d = x.shape
    alpha = (2.0 * depth) ** 0.25
    lambda_init = 0.8 - 0.6 * math.exp(-0.3 * l)
    cos, sin = _rope_tables(s)
    x_bf = x.astype(BF16)
    w_bf = w_in[l].astype(BF16)

    gates = _inproj_gates(x_bf, w_bf, b_gate[l].reshape(1, 2 * d))
    qkv = _inproj_qkv(x_bf, w_bf, cos, sin)
    lam_p = jnp.stack([lambda_q1[l], lambda_k1[l], lambda_q2[l], lambda_k2[l]]).astype(F32)
    y_diff = _diff_attention(qkv, lam_p, diff_norm_g[l].reshape(1, -1), lambda_init)

    dil = [_dilated_group(_inproj_dil(x_bf, w_bf, cos, sin, g, dilation), g)
           for g, (_, dilation) in enumerate(DIL_CONFIGS)]
    h, logits_t = _merge(
        [o for o, _ in dil], [ls for _, ls in dil], y_diff, gates, x,
        w_branch_diff[l].astype(BF16), w_branch_dil[l].astype(BF16), w_out[l].astype(BF16),
        router_w[l].T.astype(BF16), ln1_g[l].reshape(1, d), ln1_b[l].reshape(1, d), alpha)

    eidx, wts, rank, counts = _route(logits_t, router_bias[l].reshape(N_EXPERTS, 1).astype(F32))
    counts = counts.reshape(N_EXPERTS)
    padded = _round_up(counts, SUBLANES)
    rstart = (jnp.cumsum(padded) - padded).astype(I32)
    n_chunks = (counts + (EXP_CHUNK - 1)) // EXP_CHUNK
    chunk0 = (jnp.cumsum(n_chunks) - n_chunks).astype(I32)
    max_chunks = s * TOP_K // EXP_CHUNK + N_EXPERTS
    cid = jnp.arange(max_chunks, dtype=I32)
    chunk_end = chunk0 + n_chunks
    chunk_expert = jnp.minimum(jnp.sum(chunk_end[None, :] <= cid[:, None], axis=1), N_EXPERTS - 1)
    local = cid - chunk0[chunk_expert]
    chunk_valid = jnp.clip(counts[chunk_expert] - local * EXP_CHUNK, 0, EXP_CHUNK).astype(I32)
    chunk_row0 = (rstart[chunk_expert] + local * EXP_CHUNK).astype(I32)
    n_rows = s * TOP_K + N_EXPERTS * SUBLANES

    slots = _slots(eidx, rank, rstart.astype(F32).reshape(N_EXPERTS, 1))
    slots_flat = slots.T.reshape(-1)
    xs = _scatter_rows(rstart, counts, slots_flat, h, n_rows)
    ys = _experts(chunk0, counts, chunk_row0, chunk_valid, xs,
                  w_gate_e[l], w_up_e[l], w_down_e[l])
    wgu_s = jnp.concatenate([w_gate_s[l], w_up_s[l]], axis=1).astype(BF16)
    return _combine(slots_flat, ys, wts.T, h, wgu_s, w_down_s[l].astype(BF16),
                    ln2_g[l].reshape(1, d), ln2_b[l].reshape(1, d), alpha)


def kernel(x, w_in, b_gate, lambda_q1, lambda_k1, lambda_q2, lambda_k2, diff_norm_g, w_branch_diff,
           w_branch_dil, w_out, ln1_g, ln1_b, router_w, router_bias, w_gate_e, w_up_e, w_down_e,
           w_gate_s, w_up_s, w_down_s, ln2_g, ln2_b):
    b, s, d = x.shape
    depth = w_in.shape[0]
    outs = []
    for bi in range(b):
        xb = x[bi]
        for l in range(depth):
            xb = _layer(xb, l, depth, w_in, b_gate, lambda_q1, lambda_k1, lambda_q2, lambda_k2,
                        diff_norm_g, w_branch_diff, w_branch_dil, w_out, ln1_g, ln1_b, router_w,
                        router_bias, w_gate_e, w_up_e, w_down_e, w_gate_s, w_up_s, w_down_s,
                        ln2_g, ln2_b)
        outs.append(xb)
    return jnp.stack(outs)
```

```python
import functools
import math

import jax
import jax.numpy as jnp
import numpy as np
from jax import lax
from jax.experimental import pallas as pl
from jax.experimental.pallas import tpu as pltpu

F32 = jnp.float32
BF16 = jnp.bfloat16
I32 = jnp.int32

HEAD_DIM = 128
ROPE_THETA = 10000.0
LN_EPS = 1e-5
DIFF_HEADS = 4
DIL_CONFIGS = ((128, 1), (512, 4), (2048, 16))
DIL_HEADS_PER_GROUP = 4
DIL_OUT = DIL_HEADS_PER_GROUP * HEAD_DIM
N_EXPERTS = 256
TOP_K = 8
N_GROUPS = 8
TOPK_GROUPS = 4
GROUP_SIZE = N_EXPERTS // N_GROUPS
ROUTED_SCALE = 2.5
ATTN_SCALE = HEAD_DIM ** -0.5
LOG2E = math.log2(math.e)

PROJ_TILE = 512
N_QKV_TILES = 6
DIL_SRC_TILE0 = 6
GATE_SRC_TILE0 = 15
N_GATE_TILES = 8

INPROJ_TM = 1024
ATTN_TQ = 1024
ATTN_TK = 1024
LANES = 128
DIL_T = 128
DIL_BLOCKS_PER_STEP = 4
MERGE_TM = 256
ROUTE_TM = 512
ROW_TM = 256
EXP_CHUNK = 128
W_RING = 2
X_SLOTS = 10
Y_SLOTS = 6
SUBLANES = 8
NEG = -1e30
VMEM_LIMIT = 56 * 1024 * 1024


def _cparams(sem, vmem=None):
    return pltpu.CompilerParams(dimension_semantics=sem, vmem_limit_bytes=vmem)


def _rope(a, cos, sin):
    outs = []
    for h in range(a.shape[1] // HEAD_DIM):
        ah = a[:, h * HEAD_DIM:(h + 1) * HEAD_DIM]
        outs.append(ah * cos + pltpu.roll(ah, HEAD_DIM // 2, 1) * sin)
    return jnp.concatenate(outs, axis=1)


def _weight_tile(w_ref, wb):
    @pl.when(pl.program_id(1) == 0)
    def _():
        wb[...] = w_ref[...].astype(BF16)
    return wb[...]


def _gates_kernel(x_ref, w_ref, b_ref, o_ref, wb):
    acc = jnp.dot(x_ref[...], _weight_tile(w_ref, wb), preferred_element_type=F32)
    o_ref[...] = jax.nn.sigmoid(acc + b_ref[...]).astype(BF16)


def _inproj_gates(x_bf, w, b_gate):
    s, d = x_bf.shape
    tm = min(INPROJ_TM, s)
    return pl.pallas_call(
        _gates_kernel,
        out_shape=jax.ShapeDtypeStruct((s, N_GATE_TILES * PROJ_TILE), BF16),
        grid=(N_GATE_TILES, s // tm),
        in_specs=[
            pl.BlockSpec((tm, d), lambda j, i: (i, 0)),
            pl.BlockSpec((d, PROJ_TILE), lambda j, i: (0, GATE_SRC_TILE0 + j)),
            pl.BlockSpec((1, PROJ_TILE), lambda j, i: (0, j)),
        ],
        out_specs=pl.BlockSpec((tm, PROJ_TILE), lambda j, i: (i, j)),
        scratch_shapes=[pltpu.VMEM((d, PROJ_TILE), BF16)],
        compiler_params=_cparams(("arbitrary", "arbitrary"), VMEM_LIMIT),
        name="inproj_gates",
    )(x_bf, w, b_gate)


def _qkv_kernel(x_ref, w_ref, cos_ref, sin_ref, o_ref, wb):
    j = pl.program_id(0)
    acc = jnp.dot(x_ref[...], _weight_tile(w_ref, wb), preferred_element_type=F32)
    scale = jnp.where(j < 2, ATTN_SCALE * LOG2E, 1.0).astype(F32)
    o_ref[...] = _rope(acc, cos_ref[...] * scale, sin_ref[...] * scale).astype(BF16)


def _inproj_qkv(x_bf, w, cos_tab, sin_tab):
    s, d = x_bf.shape
    tm = min(INPROJ_TM, s)
    tab = pl.BlockSpec((None, tm, HEAD_DIM), lambda j, i: ((j >= 4).astype(I32), i, 0))
    return pl.pallas_call(
        _qkv_kernel,
        out_shape=jax.ShapeDtypeStruct((s, N_QKV_TILES * PROJ_TILE), BF16),
        grid=(N_QKV_TILES, s // tm),
        in_specs=[
            pl.BlockSpec((tm, d), lambda j, i: (i, 0)),
            pl.BlockSpec((d, PROJ_TILE), lambda j, i: (0, j)),
            tab, tab,
        ],
        out_specs=pl.BlockSpec((tm, PROJ_TILE), lambda j, i: (i, j)),
        scratch_shapes=[pltpu.VMEM((d, PROJ_TILE), BF16)],
        compiler_params=_cparams(("arbitrary", "arbitrary"), VMEM_LIMIT),
        name="inproj_qkv",
    )(x_bf, w, cos_tab, sin_tab)


def _inproj_dil_kernel(x_ref, w_ref, cos_ref, sin_ref, o_ref, scr, wb, *, dilation):
    part = pl.program_id(0)
    acc = jnp.dot(x_ref[...], _weight_tile(w_ref, wb), preferred_element_type=F32)
    scale = jnp.where(part == 0, ATTN_SCALE, 1.0).astype(F32)
    val = _rope(acc, cos_ref[...] * scale, sin_ref[...] * scale)
    if dilation == 1:
        o_ref[0] = val.astype(BF16)
        return
    for c in range(scr.shape[0]):
        scr[c] = val[:, c * HEAD_DIM:(c + 1) * HEAD_DIM]
    n = scr.shape[1] // dilation
    for r in range(dilation):
        for c in range(scr.shape[0]):
            o_ref[r, :, c * HEAD_DIM:(c + 1) * HEAD_DIM] = (
                scr[c, pl.ds(r, n, stride=dilation), :].astype(BF16))


def _inproj_dil(x_bf, w, cos_tab, sin_tab, g, dilation):
    s, d = x_bf.shape
    tm = min(INPROJ_TM, s)
    n = tm // dilation
    tab = pl.BlockSpec((None, tm, HEAD_DIM), lambda p, i: ((p == 2).astype(I32), i, 0))
    return pl.pallas_call(
        functools.partial(_inproj_dil_kernel, dilation=dilation),
        out_shape=jax.ShapeDtypeStruct((dilation, s // dilation, 3 * PROJ_TILE), BF16),
        grid=(3, s // tm),
        in_specs=[
            pl.BlockSpec((tm, d), lambda p, i: (i, 0)),
            pl.BlockSpec((d, PROJ_TILE), lambda p, i: (0, DIL_SRC_TILE0 + g + 3 * p)),
            tab, tab,
        ],
        out_specs=pl.BlockSpec((dilation, n, PROJ_TILE), lambda p, i: (0, i, p)),
        scratch_shapes=[pltpu.VMEM((PROJ_TILE // HEAD_DIM, tm, HEAD_DIM), F32),
                        pltpu.VMEM((d, PROJ_TILE), BF16)],
        compiler_params=_cparams(("arbitrary", "arbitrary"), VMEM_LIMIT),
        name=f"inproj_dil{g}",
    )(x_bf, w, cos_tab, sin_tab)


def _diff_kernel(qi_ref, kj_ref, q_ref, k_ref, v_ref, lam_ref, g_ref, o_ref, m_sc, l_sc, acc_sc,
                 *, lambda_init):
    step = pl.program_id(1)
    qi = qi_ref[step]
    kj = kj_ref[step]
    tq, tk = q_ref.shape[0], k_ref.shape[0]
    last_kj = qi // (tk // tq)

    @pl.when(kj == 0)
    def _():
        m_sc[...] = jnp.full(m_sc.shape, NEG, F32)
        l_sc[...] = jnp.zeros(l_sc.shape, F32)
        acc_sc[...] = jnp.zeros(acc_sc.shape, F32)

    def update(rows, keys, masked):
        nr, nk = rows.stop - rows.start, keys.stop - keys.start
        v = v_ref[keys, :]
        if masked:
            row = qi * tq + rows.start + lax.broadcasted_iota(I32, (nr, nk), 0)
            col = kj * tk + keys.start + lax.broadcasted_iota(I32, (nr, nk), 1)
            keep = col <= row
        for mm in range(2):
            q = q_ref[rows, mm * HEAD_DIM:(mm + 1) * HEAD_DIM]
            k = k_ref[keys, mm * HEAD_DIM:(mm + 1) * HEAD_DIM]
            s = lax.dot_general(q, k, (((1,), (1,)), ((), ())), preferred_element_type=F32)
            if masked:
                s = jnp.where(keep, s, NEG)
            m_prev = m_sc[mm, rows]
            m_new = jnp.maximum(m_prev, jnp.max(s, axis=-1, keepdims=True))
            alpha = jnp.exp2(m_prev - m_new)
            p = jnp.exp2(s - jnp.tile(m_new, (1, nk // LANES)))
            l_sc[mm, rows] = alpha * l_sc[mm, rows] + jnp.sum(p, axis=-1, keepdims=True)
            pv = jnp.dot(p.astype(BF16), v, preferred_element_type=F32)
            acc_sc[mm, rows] = jnp.tile(alpha, (1, v.shape[1] // LANES)) * acc_sc[mm, rows] + pv
            m_sc[mm, rows] = m_new

    @pl.when(kj < last_kj)
    def _():
        update(slice(0, tq), slice(0, tk), False)

    @pl.when(kj == last_kj)
    def _():
        if tq == tk:
            half = tq // 2
            update(slice(0, half), slice(0, half), True)
            update(slice(half, tq), slice(0, tk), True)
        else:
            update(slice(0, tq), slice(0, tk), True)
        lam_p = lam_ref[...]
        lam = (jnp.exp(jnp.sum(lam_p[0:1] * lam_p[1:2], axis=-1, keepdims=True))
               - jnp.exp(jnp.sum(lam_p[2:3] * lam_p[3:4], axis=-1, keepdims=True))
               + lambda_init)
        rep = acc_sc.shape[2] // LANES
        o = (acc_sc[0] / jnp.tile(l_sc[0], (1, rep))
             - lam * (acc_sc[1] / jnp.tile(l_sc[1], (1, rep))))
        o = o * lax.rsqrt(jnp.mean(o * o, axis=-1, keepdims=True) + LN_EPS) * g_ref[...]
        o_ref[...] = (o * (1.0 - lambda_init)).astype(o_ref.dtype)


def _diff_attention(proj, lam_p, norm_g, lambda_init):
    s = proj.shape[0]
    tq = min(ATTN_TQ, s)
    tk = min(ATTN_TK, s)
    pairs = [(i, j) for i in range(s // tq) for j in range(i * tq // tk + 1)]
    qi_tab = np.asarray([p[0] for p in pairs], np.int32)
    kj_tab = np.asarray([p[1] for p in pairs], np.int32)
    vw = 2 * HEAD_DIM
    grid_spec = pltpu.PrefetchScalarGridSpec(
        num_scalar_prefetch=2,
        grid=(DIFF_HEADS, len(pairs)),
        in_specs=[
            pl.BlockSpec((tq, vw), lambda h, st, qi, kj: (qi[st], h)),
            pl.BlockSpec((tk, vw), lambda h, st, qi, kj: (kj[st], DIFF_HEADS + h)),
            pl.BlockSpec((tk, vw), lambda h, st, qi, kj: (kj[st], 2 * DIFF_HEADS + h)),
            pl.BlockSpec((4, HEAD_DIM), lambda h, st, qi, kj: (0, 0)),
            pl.BlockSpec((1, vw), lambda h, st, qi, kj: (0, 0)),
        ],
        out_specs=pl.BlockSpec((tq, vw), lambda h, st, qi, kj: (qi[st], h)),
        scratch_shapes=[
            pltpu.VMEM((2, tq, LANES), F32),
            pltpu.VMEM((2, tq, LANES), F32),
            pltpu.VMEM((2, tq, vw), F32),
        ],
    )
    return pl.pallas_call(
        functools.partial(_diff_kernel, lambda_init=lambda_init),
        out_shape=jax.ShapeDtypeStruct((s, DIFF_HEADS * vw), BF16),
        grid_spec=grid_spec,
        compiler_params=_cparams(("arbitrary", "arbitrary"), VMEM_LIMIT),
        name="diffattn",
    )(jnp.asarray(qi_tab), jnp.asarray(kj_tab), proj, proj, proj, lam_p, norm_g)


def _dil_kernel(q_ref, kp_ref, kc_ref, vp_ref, vc_ref, o_ref, lse_ref):
    n = pl.program_id(1)
    t = kp_ref.shape[0]
    qi = lax.broadcasted_iota(I32, (t, t), 0)
    kj = lax.broadcasted_iota(I32, (t, t), 1)
    in_span = kj >= qi
    keep_cur = kj <= qi
    dn = (((1,), (1,)), ((), ()))
    for b in range(q_ref.shape[0] // t):
        rows = slice(b * t, (b + 1) * t)
        before = slice((b - 1) * t, b * t)
        for h in range(DIL_HEADS_PER_GROUP):
            sl = slice(h * HEAD_DIM, (h + 1) * HEAD_DIM)
            q = q_ref[rows, sl]
            k_prev, v_prev = (kp_ref[:, sl], vp_ref[:, sl]) if b == 0 else (kc_ref[before, sl],
                                                                              vc_ref[before, sl])
            keep_prev = (in_span & (n > 0)) if b == 0 else in_span
            sp = lax.dot_general(q, k_prev, dn, preferred_element_type=F32)
            sc = lax.dot_general(q, kc_ref[rows, sl], dn, preferred_element_type=F32)
            sp = jnp.where(keep_prev, sp, NEG)
            sc = jnp.where(keep_cur, sc, NEG)
            m = jnp.maximum(jnp.max(sp, axis=-1, keepdims=True),
                            jnp.max(sc, axis=-1, keepdims=True))
            ep = jnp.exp(sp - m)
            ec = jnp.exp(sc - m)
            den = jnp.sum(ep, axis=-1, keepdims=True) + jnp.sum(ec, axis=-1, keepdims=True)
            acc = (jnp.dot(ep.astype(BF16), v_prev, preferred_element_type=F32)
                   + jnp.dot(ec.astype(BF16), vc_ref[rows, sl], preferred_element_type=F32))
            o_ref[rows, sl] = acc / den
            lse_ref[rows, sl] = jnp.broadcast_to(m + jnp.log(den), (t, HEAD_DIM))


def _dilated_group(qkv, g):
    dilation, l, _ = qkv.shape
    t = DIL_T
    nb = l // t
    per_step = DIL_BLOCKS_PER_STEP if nb % DIL_BLOCKS_PER_STEP == 0 else 1
    cur = lambda part: (lambda r, n: (r, n, part))
    prev = lambda part: (lambda r, n: (r, jnp.maximum(n * per_step - 1, 0), part))
    wide = (None, per_step * t, PROJ_TILE)
    one = (None, t, PROJ_TILE)
    return pl.pallas_call(
        _dil_kernel,
        out_shape=[jax.ShapeDtypeStruct((dilation, l, DIL_OUT), F32)] * 2,
        grid=(dilation, nb // per_step),
        in_specs=[
            pl.BlockSpec(wide, cur(0)),
            pl.BlockSpec(one, prev(1)),
            pl.BlockSpec(wide, cur(1)),
            pl.BlockSpec(one, prev(2)),
            pl.BlockSpec(wide, cur(2)),
        ],
        out_specs=[pl.BlockSpec((None, per_step * t, DIL_OUT), lambda r, n: (r, n, 0))] * 2,
        compiler_params=_cparams(("arbitrary", "arbitrary")),
        name=f"dilattn{g}",
    )(qkv, qkv, qkv, qkv, qkv)


def _layer_norm(z, g, b):
    mu = jnp.mean(z, axis=-1, keepdims=True)
    zc = z - mu
    var = jnp.mean(zc * zc, axis=-1, keepdims=True)
    return zc * lax.rsqrt(var + LN_EPS) * g + b


def _merge_kernel(o0, o1, o2, l0, l1, l2, yd_ref, ga_ref, gb_ref, x_ref, wbd_ref, wbl_ref, wo_ref,
                  rwt_ref, g_ref, b_ref, h_ref, lt_ref, scr, *, alpha):
    def positions(ref, slot):
        d = ref.shape[0]
        if d == 1:
            return ref[0]
        heads = scr.shape[1]
        for r in range(d):
            for c in range(heads):
                scr[slot, c, pl.ds(r, ref.shape[1], stride=d), :] = (
                    ref[r, :, c * HEAD_DIM:(c + 1) * HEAD_DIM])
        return jnp.concatenate([scr[slot, c] for c in range(heads)], axis=1)

    la, lb, lc = positions(l0, 0), positions(l1, 0), positions(l2, 1)
    m = jnp.maximum(jnp.maximum(la, lb), lc)
    ea, eb, ec = jnp.exp(la - m), jnp.exp(lb - m), jnp.exp(lc - m)
    num = ea * positions(o0, 0) + eb * positions(o1, 2) + ec * positions(o2, 3)
    ydil = num / (ea + eb + ec)
    a = jnp.dot(yd_ref[...], wbd_ref[...], preferred_element_type=F32)
    b = jnp.dot(ydil.astype(BF16), wbl_ref[...], preferred_element_type=F32)
    merged = ga_ref[...].astype(F32) * a + gb_ref[...].astype(F32) * b
    z = alpha * x_ref[...] + jnp.dot(merged.astype(BF16), wo_ref[...], preferred_element_type=F32)
    h = _layer_norm(z, g_ref[...], b_ref[...])
    h_ref[...] = h
    lt_ref[...] = lax.dot_general(rwt_ref[...], h.astype(BF16), (((1,), (1,)), ((), ())),
                                  preferred_element_type=F32)


def _merge(dil_o, dil_l, y_diff, gates, x, wbd, wbl, wo, rwt, ln_g, ln_b, alpha):
    s, d = x.shape
    tm = min(MERGE_TM, s)
    row = lambda w: pl.BlockSpec((tm, w), lambda i: (i, 0))
    const = lambda a: pl.BlockSpec(a.shape, lambda i: (0, 0), pipeline_mode=pl.Buffered(1))
    res = lambda a: pl.BlockSpec((a.shape[0], tm // a.shape[0], DIL_OUT), lambda i: (0, i, 0))
    in_specs = ([res(a) for a in dil_o] + [res(a) for a in dil_l] + [
        row(y_diff.shape[1]),
        pl.BlockSpec((tm, d), lambda i: (i, 0)),
        pl.BlockSpec((tm, d), lambda i: (i, 1)),
        row(d), const(wbd), const(wbl), const(wo), const(rwt), const(ln_g), const(ln_b)])
    return pl.pallas_call(
        functools.partial(_merge_kernel, alpha=alpha),
        out_shape=[jax.ShapeDtypeStruct((s, d), F32),
                   jax.ShapeDtypeStruct((N_EXPERTS, s), F32)],
        grid=(s // tm,),
        in_specs=in_specs,
        out_specs=[row(d), pl.BlockSpec((N_EXPERTS, tm), lambda i: (0, i))],
        scratch_shapes=[pltpu.VMEM((4, DIL_HEADS_PER_GROUP, tm, HEAD_DIM), F32)],
        compiler_params=_cparams(("arbitrary",), VMEM_LIMIT),
        name="merge",
    )(*dil_o, *dil_l, y_diff, gates, gates, x, wbd, wbl, wo, rwt, ln_g, ln_b)


def _route_kernel(lt_ref, bias_ref, e_ref, w_ref, r_ref, cnt_ref, carry):
    i = pl.program_id(0)
    tm = lt_ref.shape[1]
    ninf = -jnp.inf

    @pl.when(i == 0)
    def _():
        carry[...] = jnp.zeros(carry.shape, F32)

    sc = jax.nn.sigmoid(lt_ref[...])
    biased = sc + bias_ref[...]
    iog = lax.broadcasted_iota(I32, (GROUP_SIZE, tm), 0)
    blocks, gscore = [], []
    for g in range(N_GROUPS):
        blk = biased[g * GROUP_SIZE:(g + 1) * GROUP_SIZE, :]
        m1 = jnp.max(blk, axis=0, keepdims=True)
        i1 = jnp.min(jnp.where(blk == m1, iog, GROUP_SIZE), axis=0, keepdims=True)
        m2 = jnp.max(jnp.where(iog == i1, ninf, blk), axis=0, keepdims=True)
        blocks.append(blk)
        gscore.append(m1 + m2)
    masked = []
    for g in range(N_GROUPS):
        ahead = jnp.zeros((1, tm), F32)
        for o in range(N_GROUPS):
            if o == g:
                continue
            wins = (gscore[o] >= gscore[g]) if o < g else (gscore[o] > gscore[g])
            ahead = ahead + jnp.where(wins, 1.0, 0.0)
        keep = jnp.broadcast_to(ahead, (GROUP_SIZE, tm)) < TOPK_GROUPS
        masked.append(jnp.where(keep, blocks[g], ninf))
    v = jnp.concatenate(masked, axis=0)
    ioe = lax.broadcasted_iota(I32, (N_EXPERTS, tm), 0)
    idxs, ws = [], []
    sel = jnp.zeros((N_EXPERTS, tm), jnp.bool_)
    for _ in range(TOP_K):
        mx = jnp.max(v, axis=0, keepdims=True)
        idx = jnp.min(jnp.where(v == mx, ioe, N_EXPERTS), axis=0, keepdims=True)
        hit = ioe == idx
        idxs.append(idx)
        ws.append(jnp.sum(jnp.where(hit, sc, 0.0), axis=0, keepdims=True))
        sel = sel | hit
        v = jnp.where(hit, ninf, v)
    wsum = ws[0]
    for wk in ws[1:]:
        wsum = wsum + wk

    self = jnp.where(sel, 1.0, 0.0)
    ta = lax.broadcasted_iota(I32, (tm, tm), 0)
    tb = lax.broadcasted_iota(I32, (tm, tm), 1)
    before = jnp.where(ta < tb, 1.0, 0.0).astype(BF16)
    pos = jnp.dot(self.astype(BF16), before, preferred_element_type=F32) + carry[...]
    for k in range(TOP_K):
        e_ref[k:k + 1, :] = idxs[k]
        w_ref[k:k + 1, :] = ws[k] / wsum * ROUTED_SCALE
        rk = jnp.sum(jnp.where(ioe == idxs[k], pos, 0.0), axis=0, keepdims=True)
        r_ref[k:k + 1, :] = rk.astype(I32)
    total = carry[...] + jnp.sum(self, axis=1, keepdims=True)
    carry[...] = total
    cnt_ref[...] = total.astype(I32)


def _route(logits_t, bias_col):
    e, s = logits_t.shape
    tm = min(ROUTE_TM, s)
    tok = pl.BlockSpec((TOP_K, tm), lambda i: (0, i))
    return pl.pallas_call(
        _route_kernel,
        out_shape=[jax.ShapeDtypeStruct((TOP_K, s), I32),
                   jax.ShapeDtypeStruct((TOP_K, s), F32),
                   jax.ShapeDtypeStruct((TOP_K, s), I32),
                   jax.ShapeDtypeStruct((e, 1), I32)],
        grid=(s // tm,),
        in_specs=[pl.BlockSpec((e, tm), lambda i: (0, i)),
                  pl.BlockSpec((e, 1), lambda i: (0, 0))],
        out_specs=[tok, tok, tok, pl.BlockSpec((e, 1), lambda i: (0, 0))],
        scratch_shapes=[pltpu.VMEM((e, 1), F32)],
        compiler_params=_cparams(("arbitrary",)),
        name="route",
    )(logits_t, bias_col)


def _slots_kernel(e_ref, r_ref, ps_ref, s_ref):
    tm = e_ref.shape[1]
    ioe = lax.broadcasted_iota(I32, (N_EXPERTS, tm), 0)
    ps = ps_ref[...]
    for k in range(TOP_K):
        start = jnp.sum(jnp.where(ioe == e_ref[k:k + 1, :], ps, 0.0), axis=0, keepdims=True)
        s_ref[k:k + 1, :] = start.astype(I32) + r_ref[k:k + 1, :]


def _slots(eidx, rank, pstart_col):
    s = eidx.shape[1]
    tm = min(ROUTE_TM, s)
    tok = pl.BlockSpec((TOP_K, tm), lambda i: (0, i))
    return pl.pallas_call(
        _slots_kernel,
        out_shape=jax.ShapeDtypeStruct((TOP_K, s), I32),
        grid=(s // tm,),
        in_specs=[tok, tok, pl.BlockSpec((N_EXPERTS, 1), lambda i: (0, 0))],
        out_specs=tok,
        compiler_params=_cparams(("arbitrary",)),
        name="slots",
    )(eidx, rank, pstart_col)


def _row_pieces(length, fn):
    off = jnp.int32(0)
    size = EXP_CHUNK // 2
    while size >= SUBLANES:
        @pl.when((length & size) != 0)
        def _(off=off, size=size):
            fn(off, size)
        off = off + (length & size)
        size //= 2


def _fill_tail(end, n_rows, piece):
    tail = n_rows - end
    n_full = tail // EXP_CHUNK

    def full(t, carry):
        piece(pl.multiple_of(end + t * EXP_CHUNK, SUBLANES), EXP_CHUNK)
        return carry

    lax.fori_loop(0, n_full, full, 0)
    base = end + n_full * EXP_CHUNK
    _row_pieces(tail & (EXP_CHUNK - 1),
                lambda off, size: piece(pl.multiple_of(base + off, SUBLANES), size))


def _round_up(n, m):
    return (n + (m - 1)) // m * m


def _scatter_kernel(rstart_ref, cnt_ref, slot_ref, h_ref, xs_ref, zbuf, sem, zsem):
    i = pl.program_id(0)
    tm = h_ref.shape[0]
    n_exp = cnt_ref.shape[0]

    def zero_fill(do):
        def piece(row, size):
            do(pltpu.make_async_copy(zbuf.at[pl.ds(0, size)], xs_ref.at[pl.ds(row, size)], zsem))

        def expert(ex, carry):
            cnt = cnt_ref[ex]
            row = rstart_ref[ex] + cnt
            for j in range(SUBLANES - 1):
                @pl.when(j < ((-cnt) & (SUBLANES - 1)))
                def _(row=row, j=j):
                    piece(row + j, 1)
            return carry

        lax.fori_loop(0, n_exp, expert, 0)
        end = rstart_ref[n_exp - 1] + _round_up(cnt_ref[n_exp - 1], SUBLANES)
        _fill_tail(end, xs_ref.shape[0], piece)

    def issue(r, carry):
        for k in range(TOP_K):
            dst = slot_ref[r * TOP_K + k]
            pltpu.make_async_copy(h_ref.at[pl.ds(r, 1)], xs_ref.at[pl.ds(dst, 1)],
                                  sem).start(priority=k % 2)
        return carry

    lax.fori_loop(0, tm, issue, 0)
    for k in range(TOP_K):
        pltpu.make_async_copy(h_ref, xs_ref.at[pl.ds(0, tm)], sem).wait()

    @pl.when(i == pl.num_programs(0) - 1)
    def _():
        zbuf[...] = jnp.zeros(zbuf.shape, F32)
        zero_fill(lambda cp: cp.start())
        zero_fill(lambda cp: cp.wait())


def _scatter_rows(rstart, counts, slots_flat, h, n_rows):
    s, d = h.shape
    tm = min(ROW_TM, s)
    grid_spec = pltpu.PrefetchScalarGridSpec(
        num_scalar_prefetch=2,
        grid=(s // tm,),
        in_specs=[pl.BlockSpec((tm * TOP_K,), lambda i, ps, cn: (i,), memory_space=pltpu.SMEM),
                  pl.BlockSpec((tm, d), lambda i, ps, cn: (i, 0))],
        out_specs=pl.BlockSpec(memory_space=pl.ANY),
        scratch_shapes=[pltpu.VMEM((EXP_CHUNK, d), F32), pltpu.SemaphoreType.DMA,
                        pltpu.SemaphoreType.DMA],
    )
    return pl.pallas_call(
        _scatter_kernel,
        out_shape=jax.ShapeDtypeStruct((n_rows, d), F32),
        grid_spec=grid_spec,
        compiler_params=_cparams(("arbitrary",)),
        name="scatter",
    )(rstart, counts, slots_flat, h)


def _experts_kernel(chunk0_ref, cnt_ref, row0_ref, valid_ref, xs_ref, wg_ref, wu_ref, wd_ref,
                    ys_ref, wg_buf, wu_buf, wd_buf, wgu_sc, wd_sc, xbuf, ybuf, w_sem, in_sem,
                    out_sem):
    e = pl.program_id(0)
    n_exp = pl.num_programs(0)
    last = n_exp - 1
    ff = wg_ref.shape[2]
    chunks_of = lambda ex: (cnt_ref[ex] + (EXP_CHUNK - 1)) // EXP_CHUNK

    def weight_copies(ex):
        ws = ex % W_RING
        half = wd_ref.shape[1] // 2
        lo, hi = pl.ds(0, half), pl.ds(half, half)
        parts = ((wg_ref.at[ex], wg_buf.at[ws]), (wu_ref.at[ex], wu_buf.at[ws]),
                 (wd_ref.at[ex, lo], wd_buf.at[ws, lo]), (wd_ref.at[ex, hi], wd_buf.at[ws, hi]))
        return [(pltpu.make_async_copy(src, dst, w_sem.at[i, ws]), 0)
                for i, (src, dst) in enumerate(parts)]

    @pl.when(e == 0)
    def _():
        for ex in range(W_RING - 1):
            for cp, queue in weight_copies(ex):
                cp.start(priority=queue)

    @pl.when(e + (W_RING - 1) < n_exp)
    def _():
        for cp, queue in weight_copies(e + (W_RING - 1)):
            cp.start(priority=queue)

    for cp, _ in weight_copies(e):
        cp.wait()
    wslot = e % W_RING
    g0 = chunk0_ref[e]
    nch = chunks_of(e)
    total = chunk0_ref[last] + chunks_of(last)

    def move(g, n_rows, hbm, buf, sem, do):
        row0 = pl.multiple_of(row0_ref[g], SUBLANES)

        def piece(off, size):
            src = hbm.at[pl.ds(row0 + off, size)]
            dst = buf.at[pl.ds(off, size)]
            do(pltpu.make_async_copy(src, dst, sem) if hbm is xs_ref
               else pltpu.make_async_copy(dst, src, sem))

        @pl.when(n_rows == EXP_CHUNK)
        def _():
            piece(0, EXP_CHUNK)

        @pl.when(n_rows < EXP_CHUNK)
        def _():
            _row_pieces(n_rows, lambda off, size: piece(pl.multiple_of(off, SUBLANES), size))
            whole = n_rows & ~(SUBLANES - 1)
            for j in range(SUBLANES - 1):
                @pl.when(j < (n_rows & (SUBLANES - 1)))
                def _(j=j):
                    piece(whole + j, 1)

    def fetch(g, slot, do):
        move(g, valid_ref[g], xs_ref, xbuf.at[slot], in_sem.at[slot], do)

    def write_back(g, slot, do):
        move(g, _round_up(valid_ref[g], SUBLANES), ys_ref, ybuf.at[slot], out_sem.at[slot], do)

    def start(cp):
        cp.start(priority=1)

    def wait(cp):
        cp.wait()

    n_x, n_y = xbuf.shape[0], ybuf.shape[0]
    ahead = n_x - 1

    @pl.when(e == 0)
    def _():
        xbuf[...] = jnp.zeros(xbuf.shape, F32)
        for j in range(ahead):
            @pl.when(j < total)
            def _(j=j):
                fetch(j, j, start)

    @pl.when(nch > 0)
    def _():
        wgu_sc[:, :ff] = wg_buf[wslot].astype(BF16)
        wgu_sc[:, ff:] = wu_buf[wslot].astype(BF16)
        wd_sc[...] = wd_buf[wslot].astype(BF16)

        def chunk(c, carry):
            g = g0 + c
            slot = g % n_x
            fetch(g, slot, wait)

            @pl.when(g + ahead < total)
            def _():
                fetch(g + ahead, (g + ahead) % n_x, start)

            gu = jnp.dot(xbuf[slot].astype(BF16), wgu_sc[...], preferred_element_type=F32)
            hmid = jax.nn.silu(gu[:, :ff]) * gu[:, ff:]
            y = jnp.dot(hmid.astype(BF16), wd_sc[...], preferred_element_type=F32)

            yslot = g % n_y

            @pl.when(g >= n_y)
            def _():
                write_back(g - n_y, yslot, wait)

            ybuf[yslot] = y
            write_back(g, yslot, start)
            return carry

        lax.fori_loop(0, nch, chunk, 0)

    @pl.when(e == last)
    def _():
        for j in range(n_y):
            @pl.when(total > j)
            def _(j=j):
                write_back(total - 1 - j, (total - 1 - j) % n_y, wait)

        tail_g = jnp.maximum(total - 1, 0)
        end = jnp.where(total > 0, row0_ref[tail_g] + _round_up(valid_ref[tail_g], SUBLANES), 0)
        ybuf[0] = jnp.zeros(ybuf.shape[1:], F32)

        def zero_fill(do):
            def piece(row, size):
                do(pltpu.make_async_copy(ybuf.at[0, pl.ds(0, size)], ys_ref.at[pl.ds(row, size)],
                                         out_sem.at[0]))
            _fill_tail(end, ys_ref.shape[0], piece)

        zero_fill(start)
        zero_fill(wait)


def _experts(chunk0, counts, chunk_row0, chunk_valid, xs, wg, wu, wd):
    n_rows, d = xs.shape
    ne, _, ff = wg.shape
    grid_spec = pltpu.PrefetchScalarGridSpec(
        num_scalar_prefetch=4,
        grid=(ne,),
        in_specs=[
            pl.BlockSpec(memory_space=pl.ANY),
            pl.BlockSpec(memory_space=pl.ANY),
            pl.BlockSpec(memory_space=pl.ANY),
            pl.BlockSpec(memory_space=pl.ANY),
        ],
        out_specs=pl.BlockSpec(memory_space=pl.ANY),
        scratch_shapes=[
            pltpu.VMEM((W_RING, d, ff), F32),
            pltpu.VMEM((W_RING, d, ff), F32),
            pltpu.VMEM((W_RING, ff, d), F32),
            pltpu.VMEM((d, 2 * ff), BF16),
            pltpu.VMEM((ff, d), BF16),
            pltpu.VMEM((X_SLOTS, EXP_CHUNK, d), F32),
            pltpu.VMEM((Y_SLOTS, EXP_CHUNK, d), F32),
            pltpu.SemaphoreType.DMA((4, W_RING)),
            pltpu.SemaphoreType.DMA((X_SLOTS,)),
            pltpu.SemaphoreType.DMA((Y_SLOTS,)),
        ],
    )
    return pl.pallas_call(
        _experts_kernel,
        out_shape=jax.ShapeDtypeStruct((n_rows, d), F32),
        grid_spec=grid_spec,
        compiler_params=_cparams(("arbitrary",), VMEM_LIMIT),
        name="experts",
    )(chunk0, counts, chunk_row0, chunk_valid, xs, wg, wu, wd)


def _combine_kernel(slot_ref, ys_ref, wt_ref, h_ref, wgu_ref, wd_ref, g_ref, b_ref, o_ref,
                    gbuf, sem, *, alpha):
    tm = h_ref.shape[0]
    ff = wd_ref.shape[0]

    def issue(r, carry):
        for k in range(TOP_K):
            src = slot_ref[r * TOP_K + k]
            pltpu.make_async_copy(ys_ref.at[pl.ds(src, 1)], gbuf.at[k, pl.ds(r, 1)],
                                  sem).start(priority=k % 2)
        return carry

    lax.fori_loop(0, tm, issue, 0)

    h = h_ref[...]
    gu = jnp.dot(h.astype(BF16), wgu_ref[...], preferred_element_type=F32)
    hmid = jax.nn.silu(gu[:, :ff]) * gu[:, ff:]
    acc = jnp.dot(hmid.astype(BF16), wd_ref[...], preferred_element_type=F32)

    for k in range(TOP_K):
        pltpu.make_async_copy(ys_ref.at[pl.ds(0, tm)], gbuf.at[k], sem).wait()
    wt = wt_ref[...]
    for k in range(TOP_K):
        acc = acc + wt[:, k:k + 1] * gbuf[k]
    o_ref[...] = _layer_norm(alpha * h + acc, g_ref[...], b_ref[...])


def _combine(slots_flat, ys, wt, h, wgu_s, wd_s, ln_g, ln_b, alpha):
    s, d = h.shape
    tm = min(ROW_TM, s)
    row = lambda c: pl.BlockSpec((tm, c), lambda i: (i, 0))
    const = lambda a: pl.BlockSpec(a.shape, lambda i: (0, 0), pipeline_mode=pl.Buffered(1))
    return pl.pallas_call(
        functools.partial(_combine_kernel, alpha=alpha),
        out_shape=jax.ShapeDtypeStruct((s, d), F32),
        grid=(s // tm,),
        in_specs=[pl.BlockSpec((tm * TOP_K,), lambda i: (i,), memory_space=pltpu.SMEM),
                  pl.BlockSpec(memory_space=pl.ANY),
                  row(TOP_K), row(d), const(wgu_s), const(wd_s), const(ln_g), const(ln_b)],
        out_specs=row(d),
        scratch_shapes=[pltpu.VMEM((TOP_K, tm, d), F32), pltpu.SemaphoreType.DMA],
        compiler_params=_cparams(("arbitrary",), VMEM_LIMIT),
        name="combine",
    )(slots_flat, ys, wt, h, wgu_s, wd_s, ln_g, ln_b)


def _rope_tables(s):
    half = HEAD_DIM // 2
    inv = ROPE_THETA ** (-jnp.arange(half, dtype=F32) / half)
    ang = jnp.arange(s).astype(F32)[:, None] * inv[None, :]
    cos, sin = jnp.cos(ang), jnp.sin(ang)
    cos2, sin2 = jnp.concatenate([cos, cos], axis=1), jnp.concatenate([-sin, sin], axis=1)
    return jnp.stack([cos2, jnp.ones_like(cos2)]), jnp.stack([sin2, jnp.zeros_like(sin2)])


def _layer(x, l, depth, w_in, b_gate, lambda_q1, lambda_k1, lambda_q2, lambda_k2, diff_norm_g,
           w_branch_diff, w_branch_dil, w_out, ln1_g, ln1_b, router_w, router_bias,
           w_gate_e, w_up_e, w_down_e, w_gate_s, w_up_s, w_down_s, ln2_g, ln2_b):
    s, d = x.shape
    alpha = (2.0 * depth) ** 0.25
    lambda_init = 0.8 - 0.6 * math.exp(-0.3 * l)
    cos, sin = _rope_tables(s)
    x_bf = x.astype(BF16)
    w_proj = w_in[l]

    gates = _inproj_gates(x_bf, w_proj, b_gate[l].reshape(1, 2 * d))
    qkv = _inproj_qkv(x_bf, w_proj, cos, sin)
    lam_p = jnp.stack([lambda_q1[l], lambda_k1[l], lambda_q2[l], lambda_k2[l]]).astype(F32)
    y_diff = _diff_attention(qkv, lam_p, diff_norm_g[l].reshape(1, -1), lambda_init)

    dil = [_dilated_group(_inproj_dil(x_bf, w_proj, cos, sin, g, dilation), g)
           for g, (_, dilation) in enumerate(DIL_CONFIGS)]
    h, logits_t = _merge(
        [o for o, _ in dil], [ls for _, ls in dil], y_diff, gates, x,
        w_branch_diff[l].astype(BF16), w_branch_dil[l].astype(BF16), w_out[l].astype(BF16),
        router_w[l].T.astype(BF16), ln1_g[l].reshape(1, d), ln1_b[l].reshape(1, d), alpha)

    eidx, wts, rank, counts = _route(logits_t, router_bias[l].reshape(N_EXPERTS, 1).astype(F32))
    counts = counts.reshape(N_EXPERTS)
    padded = _round_up(counts, SUBLANES)
    rstart = (jnp.cumsum(padded) - padded).astype(I32)
    n_chunks = (counts + (EXP_CHUNK - 1)) // EXP_CHUNK
    chunk0 = (jnp.cumsum(n_chunks) - n_chunks).astype(I32)
    max_chunks = s * TOP_K // EXP_CHUNK + N_EXPERTS
    cid = jnp.arange(max_chunks, dtype=I32)
    chunk_end = chunk0 + n_chunks
    chunk_expert = jnp.minimum(jnp.sum(chunk_end[None, :] <= cid[:, None], axis=1), N_EXPERTS - 1)
    local = cid - chunk0[chunk_expert]
    chunk_valid = jnp.clip(counts[chunk_expert] - local * EXP_CHUNK, 0, EXP_CHUNK).astype(I32)
    chunk_row0 = (rstart[chunk_expert] + local * EXP_CHUNK).astype(I32)
    n_rows = s * TOP_K + N_EXPERTS * SUBLANES

    slots = _slots(eidx, rank, rstart.astype(F32).reshape(N_EXPERTS, 1))
    slots_flat = slots.T.reshape(-1)
    xs = _scatter_rows(rstart, counts, slots_flat, h, n_rows)
    ys = _experts(chunk0, counts, chunk_row0, chunk_valid, xs,
                  w_gate_e[l], w_up_e[l], w_down_e[l])
    wgu_s = jnp.concatenate([w_gate_s[l], w_up_s[l]], axis=1).astype(BF16)
    return _combine(slots_flat, ys, wts.T, h, wgu_s, w_down_s[l].astype(BF16),
                    ln2_g[l].reshape(1, d), ln2_b[l].reshape(1, d), alpha)


def kernel(x, w_in, b_gate, lambda_q1, lambda_k1, lambda_q2, lambda_k2, diff_norm_g, w_branch_diff,
           w_branch_dil, w_out, ln1_g, ln1_b, router_w, router_bias, w_gate_e, w_up_e, w_down_e,
           w_gate_s, w_up_s, w_down_s, ln2_g, ln2_b):
    b, s, d = x.shape
    depth = w_in.shape[0]
    outs = []
    for bi in range(b):
        xb = x[bi]
        for l in range(depth):
            xb = _layer(xb, l, depth, w_in, b_gate, lambda_q1, lambda_k1, lambda_q2, lambda_k2,
                        diff_norm_g, w_branch_diff, w_branch_dil, w_out, ln1_g, ln1_b, router_w,
                        router_bias, w_gate_e, w_up_e, w_down_e, w_gate_s, w_up_s, w_down_s,
                        ln2_g, ln2_b)
        outs.append(xb)
    return jnp.stack(outs)
```
